```python
import math
import jax, jax.numpy as jnp
from jax import lax
import numpy as np

D_MODEL = 1024
BATCH = 16
SEQ = 256
DEPTH = 1
DEC_BATCH = 4
DEC_SEQ = 4096
PAST_LEN = 512

GRID_W = 64
SSD_EXPAND = 2
D_INNER = SSD_EXPAND * D_MODEL
SSD_HEADDIM = 64
SSD_HEADS = D_INNER // SSD_HEADDIM
SSD_GROUPS = 4
SSD_HPG = SSD_HEADS // SSD_GROUPS
SSD_STATE = 128
SSD_CONV = 5
SSD_CHUNK = 128
XBC_DIM = D_INNER + 2 * SSD_GROUPS * SSD_STATE
SC_CONV = 3
PROJ_SIZES = (D_INNER, XBC_DIM, SSD_HEADS, D_MODEL, D_MODEL, D_MODEL, D_MODEL, D_MODEL)
PROJ_DIM = D_INNER + XBC_DIM + SSD_HEADS + 5 * D_MODEL
N_EXPERTS = 32
TOP_K = 4
D_EXPERT = D_MODEL
SWIGLU_LIMIT = 7.0
SWIGLU_ALPHA = 1.702
LN_EPS = 1e-5
RMS_EPS = 1e-5
DEEPNORM_ALPHA = (2 * DEPTH) ** 0.25
DEEPNORM_BETA = (8 * DEPTH) ** -0.25

kernel_name = "hybrid_ssd_shortconv_moe_diffusion_step"


def _layer_norm(x, g, b):
    xf = x.astype(jnp.float32)
    mu = jnp.mean(xf, axis=-1, keepdims=True)
    var = jnp.mean(jnp.square(xf - mu), axis=-1, keepdims=True)
    return ((xf - mu) * lax.rsqrt(var + LN_EPS)).astype(x.dtype) * g + b


def _modulate(x, shift, scale):
    return x * (1 + scale[:, None, :]) + shift[:, None, :]


def _dwconv_rows(x, w, rows, row_len):
    b, L, C = x.shape
    k_w = w.shape[0]
    pad = k_w // 2
    xr = jnp.pad(x.reshape(b, rows, row_len, C), ((0, 0), (0, 0), (pad, pad), (0, 0)))
    y = xr[:, :, 0:row_len] * w[0]
    for k in range(1, k_w):
        y = y + xr[:, :, k:k + row_len] * w[k]
    return y.reshape(b, L, C)


def _split_proj(proj):
    idx, acc = [], 0
    for s in PROJ_SIZES[:-1]:
        acc += s
        idx.append(acc)
    return jnp.split(proj, idx, axis=-1)


def _ssd_scan(xh, dt, a, bm, cm, h0):
    f32 = jnp.float32
    b, L = xh.shape[:2]
    q = SSD_CHUNK
    nc = L // q
    x_c = xh.astype(f32).reshape(b, nc, q, SSD_GROUPS, SSD_HPG, SSD_HEADDIM)
    dt_c = dt.reshape(b, nc, q, SSD_GROUPS, SSD_HPG)
    b_c = bm.astype(f32).reshape(b, nc, q, SSD_GROUPS, SSD_STATE)
    c_c = cm.astype(f32).reshape(b, nc, q, SSD_GROUPS, SSD_STATE)
    a_cs = jnp.cumsum(dt_c * a, axis=2)
    seg = a_cs[:, :, :, None] - a_cs[:, :, None]
    causal = jnp.tril(jnp.ones((q, q), dtype=bool))[:, :, None, None]
    decay = jnp.exp(jnp.where(causal, seg, -jnp.inf))
    cb = jnp.einsum('bctgn,bcsgn->bctsg', c_c, b_c)
    w_ts = cb[..., None] * decay * dt_c[:, :, None]
    y_diag = jnp.einsum('bctsgh,bcsghp->bctghp', w_ts, x_c)
    decay_to_end = jnp.exp(a_cs[:, :, -1:] - a_cs)
    dtx = x_c * (decay_to_end * dt_c)[..., None]
    states = jnp.einsum('bcsgn,bcsghp->bcghpn', b_c, dtx)
    chunk_decay = jnp.exp(a_cs[:, :, -1])

    def step(h, inp):
        s, d = inp
        return d[..., None, None] * h + s, h

    h_final, h_in = lax.scan(step, h0.astype(f32),
                             (jnp.moveaxis(states, 1, 0), jnp.moveaxis(chunk_decay, 1, 0)))
    h_in = jnp.moveaxis(h_in, 0, 1)
    y_off = jnp.einsum('bctgn,bcghpn->bctghp', c_c, h_in) * jnp.exp(a_cs)[..., None]
    y = (y_diag + y_off).reshape(b, L, SSD_GROUPS, SSD_HPG, SSD_HEADDIM)
    return y, h_final


def _ssd_branch(z, xbc, dt_raw, h0_f, h0_b, rows, row_len, p):
    f32 = jnp.float32
    b, L, _ = z.shape
    xbc = jax.nn.silu(_dwconv_rows(xbc, p['ssd_conv_w'], rows, row_len) + p['ssd_conv_b'])
    xs, bm, cm = jnp.split(xbc, [D_INNER, D_INNER + SSD_GROUPS * SSD_STATE], axis=-1)
    xh = xs.reshape(b, L, SSD_GROUPS, SSD_HPG, SSD_HEADDIM)
    bm = bm.reshape(b, L, SSD_GROUPS, SSD_STATE)
    cm = cm.reshape(b, L, SSD_GROUPS, SSD_STATE)
    dt_raw = dt_raw.astype(f32).reshape(b, L, SSD_GROUPS, SSD_HPG)
    a = -jnp.exp(p['ssd_a_log'].astype(f32)).reshape(2, SSD_GROUPS, SSD_HPG)
    dt_bias = p['ssd_dt_bias'].astype(f32).reshape(2, SSD_GROUPS, SSD_HPG)
    dt_f = jax.nn.softplus(dt_raw + dt_bias[0])
    dt_b = jax.nn.softplus(dt_raw + dt_bias[1])
    y_f, h_f = _ssd_scan(xh, dt_f, a[0], bm, cm, h0_f)
    y_b, h_b = _ssd_scan(jnp.flip(xh, 1), jnp.flip(dt_b, 1), a[1],
                         jnp.flip(bm, 1), jnp.flip(cm, 1), h0_b)
    y = y_f + jnp.flip(y_b, 1) + p['ssd_d'].astype(f32).reshape(SSD_GROUPS, SSD_HPG, 1) * xh.astype(f32)
    y = y.reshape(b, L, D_INNER) * jax.nn.silu(z.astype(f32))
    yg = y.reshape(b, L, SSD_GROUPS, D_INNER // SSD_GROUPS)
    yg = yg * lax.rsqrt(jnp.mean(jnp.square(yg), axis=-1, keepdims=True) + RMS_EPS)
    y = yg.reshape(b, L, D_INNER).astype(z.dtype) * p['ssd_norm_w']
    return y @ p['ssd_w_out'], h_f, h_b


def _moe(h, p):
    b, L, d = h.shape
    t = h.reshape(b * L, d)
    logits = (t @ p['w_router'] + p['b_router']).astype(jnp.float32)
    top_v, top_i = lax.top_k(logits, TOP_K)
    probs = jax.nn.softmax(top_v, axis=-1)
    gates = jnp.einsum('nk,nke->ne', probs,
                       jax.nn.one_hot(top_i, N_EXPERTS, dtype=jnp.float32)).astype(t.dtype)
    out = jnp.zeros_like(t)
    for e in range(N_EXPERTS):
        gu = t @ p['w_gate_up'][e] + p['b_gate_up'][e]
        g_, u_ = jnp.split(gu, 2, axis=-1)
        g_ = jnp.minimum(g_, SWIGLU_LIMIT)
        u_ = jnp.clip(u_, -SWIGLU_LIMIT, SWIGLU_LIMIT)
        act = (u_ + 1) * g_ * jax.nn.sigmoid(SWIGLU_ALPHA * g_)
        out = out + gates[:, e:e + 1] * (act @ p['w_down'][e] + p['b_down'][e])
    return out.reshape(b, L, d)


def _layer(x, mod, h0_f, h0_b, rows, row_len, p):
    shift1, scale1, gate1, shift2, scale2, gate2 = jnp.split(mod, 6, axis=-1)
    h = _modulate(x, shift1, scale1)
    proj = h @ p['w_in']
    z, xbc, dt_raw, sc_b, sc_c, sc_v, g_ssd, g_sc = _split_proj(proj)
    y_ssd, h_f, h_b = _ssd_branch(z, xbc, dt_raw, h0_f, h0_b, rows, row_len, p)
    y_sc = (sc_b * _dwconv_rows(sc_c * sc_v, p['sc_conv_w'], rows, row_len)) @ p['sc_w_out']
    mixed = jax.nn.sigmoid(g_ssd) * y_ssd + jax.nn.sigmoid(g_sc) * y_sc
    x = _layer_norm(DEEPNORM_ALPHA * x + gate1[:, None, :] * (mixed @ p['w_o']), p['ln1_g'], p['ln1_b'])
    h = _modulate(x, shift2, scale2)
    x = _layer_norm(DEEPNORM_ALPHA * x + gate2[:, None, :] * _moe(h, p), p['ln2_g'], p['ln2_b'])
    return x, h_f, h_b


def setup_inputs(seed: int = 0) -> dict:
    key = jax.random.key(seed)
    ks = jax.random.split(key, 32)
    f32 = jnp.float32

    def nrm(k, shape, scale):
        return jax.random.normal(k, shape, f32) * scale

    state_shape = (DEC_BATCH, DEPTH, SSD_HEADS, SSD_HEADDIM, SSD_STATE)
    dt0 = jnp.exp(jax.random.uniform(ks[12], (DEPTH, 2, SSD_HEADS), f32,
                                     math.log(1e-3), math.log(1e-1)))
    return {
        'x_prompt': nrm(ks[0], (BATCH, SEQ, D_MODEL), 1.0),
        'x_sample': nrm(ks[1], (DEC_BATCH, DEC_SEQ, D_MODEL), 1.0),
        'c': nrm(ks[2], (DEC_BATCH, D_MODEL), 1.0),
        'state_ssd_fwd': nrm(ks[3], state_shape, 0.1),
        'state_ssd_bwd': nrm(ks[4], state_shape, 0.1),
        'c_ctx': nrm(ks[5], (D_MODEL,), 1.0),
        'w_ada': nrm(ks[6], (DEPTH, D_MODEL, 6 * D_MODEL), D_MODEL ** -0.5),
        'b_ada': nrm(ks[7], (DEPTH, 6 * D_MODEL), 0.02),
        'w_in': nrm(ks[8], (DEPTH, D_MODEL, PROJ_DIM), D_MODEL ** -0.5),
        'ssd_conv_w': nrm(ks[9], (DEPTH, SSD_CONV, XBC_DIM), SSD_CONV ** -0.5),
        'ssd_conv_b': nrm(ks[10], (DEPTH, XBC_DIM), 0.02),
        'ssd_a_log': jnp.log(jax.random.uniform(ks[11], (DEPTH, 2, SSD_HEADS), f32, 1.0, 16.0)),
        'ssd_dt_bias': dt0 + jnp.log(-jnp.expm1(-dt0)),
        'ssd_d': 1.0 + nrm(ks[13], (DEPTH, SSD_HEADS), 0.1),
        'ssd_norm_w': 1.0 + nrm(ks[14], (DEPTH, D_INNER), 0.05),
        'ssd_w_out': nrm(ks[15], (DEPTH, D_INNER, D_MODEL), DEEPNORM_BETA * D_INNER ** -0.5),
        'sc_conv_w': nrm(ks[16], (DEPTH, SC_CONV, D_MODEL), SC_CONV ** -0.5),
        'sc_w_out': nrm(ks[17], (DEPTH, D_MODEL, D_MODEL), DEEPNORM_BETA * D_MODEL ** -0.5),
        'w_o': nrm(ks[18], (DEPTH, D_MODEL, D_MODEL), DEEPNORM_BETA * D_MODEL ** -0.5),
        'ln1_g': 1.0 + nrm(ks[19], (DEPTH, D_MODEL), 0.05),
        'ln1_b': nrm(ks[20], (DEPTH, D_MODEL), 0.02),
        'w_router': nrm(ks[21], (DEPTH, D_MODEL, N_EXPERTS), D_MODEL ** -0.5),
        'b_router': nrm(ks[22], (DEPTH, N_EXPERTS), 0.01),
        'w_gate_up': nrm(ks[23], (DEPTH, N_EXPERTS, D_MODEL, 2 * D_EXPERT), D_MODEL ** -0.5),
        'b_gate_up': nrm(ks[24], (DEPTH, N_EXPERTS, 2 * D_EXPERT), 0.02),
        'w_down': nrm(ks[25], (DEPTH, N_EXPERTS, D_EXPERT, D_MODEL), DEEPNORM_BETA * D_EXPERT ** -0.5),
        'b_down': nrm(ks[26], (DEPTH, N_EXPERTS, D_MODEL), 0.02),
        'ln2_g': 1.0 + nrm(ks[27], (DEPTH, D_MODEL), 0.05),
        'ln2_b': nrm(ks[28], (DEPTH, D_MODEL), 0.02),
    }


def reference(x_prompt, x_sample, c, state_ssd_fwd, state_ssd_bwd, c_ctx,
              w_ada, b_ada, w_in, ssd_conv_w, ssd_conv_b, ssd_a_log, ssd_dt_bias, ssd_d,
              ssd_norm_w, ssd_w_out, sc_conv_w, sc_w_out, w_o, ln1_g, ln1_b,
              w_router, b_router, w_gate_up, b_gate_up, w_down, b_down, ln2_g, ln2_b):
    n_ctx_req, ctx_len = x_prompt.shape[0], x_prompt.shape[1]
    n_lat_req = x_sample.shape[0]
    rows = x_sample.shape[1] // GRID_W
    state_split = (SSD_GROUPS, SSD_HPG, SSD_HEADDIM, SSD_STATE)
    xp, xs = x_prompt, x_sample
    new_f, new_b = [], []
    for l in range(DEPTH):
        p = {
            'w_in': w_in[l], 'ssd_conv_w': ssd_conv_w[l], 'ssd_conv_b': ssd_conv_b[l],
            'ssd_a_log': ssd_a_log[l], 'ssd_dt_bias': ssd_dt_bias[l], 'ssd_d': ssd_d[l],
            'ssd_norm_w': ssd_norm_w[l], 'ssd_w_out': ssd_w_out[l],
            'sc_conv_w': sc_conv_w[l], 'sc_w_out': sc_w_out[l], 'w_o': w_o[l],
            'ln1_g': ln1_g[l], 'ln1_b': ln1_b[l],
            'w_router': w_router[l], 'b_router': b_router[l],
            'w_gate_up': w_gate_up[l], 'b_gate_up': b_gate_up[l],
            'w_down': w_down[l], 'b_down': b_down[l],
            'ln2_g': ln2_g[l], 'ln2_b': ln2_b[l],
        }
        mod_ctx = (jax.nn.silu(c_ctx) @ w_ada[l] + b_ada[l])[None, :]
        zeros = jnp.zeros((n_ctx_req,) + state_split, dtype=xp.dtype)
        xp, h_f, h_b = _layer(xp, mod_ctx, zeros, zeros, 1, ctx_len, p)
        new_f.append(h_f.reshape(n_ctx_req, SSD_HEADS, SSD_HEADDIM, SSD_STATE).astype(xp.dtype))
        new_b.append(h_b.reshape(n_ctx_req, SSD_HEADS, SSD_HEADDIM, SSD_STATE).astype(xp.dtype))
        mod_lat = jax.nn.silu(c) @ w_ada[l] + b_ada[l]
        h0_f = state_ssd_fwd[:, l].reshape((n_lat_req,) + state_split)
        h0_b = state_ssd_bwd[:, l].reshape((n_lat_req,) + state_split)
        xs, _, _ = _layer(xs, mod_lat, h0_f, h0_b, rows, GRID_W, p)
    new_state_ssd_fwd = jnp.stack(new_f, axis=1)
    new_state_ssd_bwd = jnp.stack(new_b, axis=1)
    return (xp, xs, new_state_ssd_fwd, new_state_ssd_bwd)
```

```python
import functools

import jax
import jax.numpy as jnp
from jax import lax
from jax.experimental import pallas as pl
from jax.experimental.pallas import tpu as pltpu

F32 = jnp.float32
BF16 = jnp.bfloat16

D_MODEL = 1024
GRID_W = 64
D_INNER = 2048
SSD_HEADDIM = 64
SSD_HEADS = 32
SSD_GROUPS = 4
SSD_HPG = 8
SSD_STATE = 128
SSD_CONV = 5
SSD_CHUNK = 128
XBC_DIM = D_INNER + 2 * SSD_GROUPS * SSD_STATE
SC_CONV = 3
N_EXPERTS = 32
TOP_K = 4
D_EXPERT = 1024
SWIGLU_LIMIT = 7.0
SWIGLU_ALPHA = 1.702
LN_EPS = 1e-5
RMS_EPS = 1e-5
DEEPNORM_ALPHA = 2.0 ** 0.25

LANES = 128
MAIN_COLS = 10240
COL_Z = 0
COL_XBC = 2048
COL_SCB = 5120
COL_SCC = 6144
COL_SCV = 7168
COL_GSSD = 8192
COL_GSC = 9216

VMEM_LIMIT = 56 * 1024 * 1024

TM_PROJ = 1024
TN_PROJ = 2048
TB_CONV = 256
TS_SSD = 256
TM_POST = 256
TB_MOVE = 256
TM_FFN = 256


def _cparams(sem):
    return pltpu.CompilerParams(dimension_semantics=sem, vmem_limit_bytes=VMEM_LIMIT)


def _split3(v):
    hi = v.astype(BF16)
    r1 = v - hi.astype(F32)
    mid = r1.astype(BF16)
    lo = (r1 - mid.astype(F32)).astype(BF16)
    return hi, mid, lo


def _dot(a, b):
    return jnp.dot(a, b, preferred_element_type=F32)


def _dot_exact_lhs(m_bf16, v_f32):
    hi, mid, lo = _split3(v_f32)
    return _dot(m_bf16, hi) + _dot(m_bf16, mid) + _dot(m_bf16, lo)


def _dot_x3(a_f32, b_f32):
    a_hi = a_f32.astype(BF16)
    a_lo = (a_f32 - a_hi.astype(F32)).astype(BF16)
    b_hi = b_f32.astype(BF16)
    b_lo = (b_f32 - b_hi.astype(F32)).astype(BF16)
    return _dot(a_hi, b_hi) + _dot(a_lo, b_hi) + _dot(a_hi, b_lo)


def _silu(v):
    return v * jax.nn.sigmoid(v)


def _softplus(v):
    return jnp.maximum(v, 0.0) + jnp.log(1.0 + jnp.exp(-jnp.abs(v)))


def _mod_kernel(c_ref, w_ref, b_ref, o_ref):
    o_ref[...] = _dot_x3(_silu(c_ref[...]), w_ref[...]) + b_ref[...]


def _mod_call(cvec, w_ada, b_ada):
    n = w_ada.shape[1]
    tn = 1536
    return pl.pallas_call(
        _mod_kernel,
        out_shape=jax.ShapeDtypeStruct((cvec.shape[0], n), F32),
        grid=(n // tn,),
        in_specs=[
            pl.BlockSpec(cvec.shape, lambda j: (0, 0)),
            pl.BlockSpec((D_MODEL, tn), lambda j: (0, j)),
            pl.BlockSpec((1, tn), lambda j: (0, j)),
        ],
        out_specs=pl.BlockSpec((cvec.shape[0], tn), lambda j: (0, j)),
        compiler_params=_cparams(("arbitrary",)),
        name="mod",
    )(cvec, w_ada, b_ada.reshape(1, n))


def _mod_row(block, rows_per_block, n_ctx_tokens, lat_len):
    tok = block * rows_per_block
    return jnp.where(tok < n_ctx_tokens, 0, 1 + (tok - n_ctx_tokens) // lat_len)


def _inproj_kernel(x_ref, mod_ref, w_ref, wdt_ref, o_ref, dt_ref, h_scr, *, n_ctx, lat_len):
    i = pl.program_id(0)
    j = pl.program_id(1)

    @pl.when(j == 0)
    def _():
        r = _mod_row(i, TM_PROJ, n_ctx, lat_len)
        shift = mod_ref[pl.ds(r, 1), 0:D_MODEL]
        scale = mod_ref[pl.ds(r, 1), D_MODEL:2 * D_MODEL]
        h = (x_ref[...] * (1.0 + scale) + shift).astype(BF16)
        h_scr[...] = h
        dt_ref[...] = _dot(h, wdt_ref[...])

    o_ref[...] = _dot(h_scr[...], w_ref[...]).astype(BF16)


def _inproj_call(x_all, mod, w_main, w_dt, n_ctx, lat_len):
    t = x_all.shape[0]
    kern = functools.partial(_inproj_kernel, n_ctx=n_ctx, lat_len=lat_len)
    return pl.pallas_call(
        kern,
        out_shape=(jax.ShapeDtypeStruct((t, MAIN_COLS), BF16),
                   jax.ShapeDtypeStruct((t, LANES), F32)),
        grid=(t // TM_PROJ, MAIN_COLS // TN_PROJ),
        in_specs=[
            pl.BlockSpec((TM_PROJ, D_MODEL), lambda i, j: (i, 0)),
            pl.BlockSpec(mod.shape, lambda i, j: (0, 0)),
            pl.BlockSpec((D_MODEL, TN_PROJ), lambda i, j: (0, j)),
            pl.BlockSpec((D_MODEL, LANES), lambda i, j: (0, 0)),
        ],
        out_specs=(pl.BlockSpec((TM_PROJ, TN_PROJ), lambda i, j: (i, j)),
                   pl.BlockSpec((TM_PROJ, LANES), lambda i, j: (i, 0))),
        scratch_shapes=[pltpu.VMEM((TM_PROJ, D_MODEL), BF16)],
        compiler_params=_cparams(("arbitrary", "arbitrary")),
        name="inproj",
    )(x_all, mod, w_main, w_dt)


def _shift_rows(x, off, pos, row_len):
    if off == 0:
        return x
    n = x.shape[0]
    rolled = pltpu.roll(x, (-off) % n, 0)
    ok = (pos + off >= 0) & (pos + off < row_len)
    return jnp.where(ok, rolled, 0.0)


def _row_pos(n, row_len):
    t = lax.broadcasted_iota(jnp.int32, (n, 1), 0)
    return jnp.bitwise_and(t, row_len - 1)


def _conv_kernel(x_ref, w_ref, b_ref, o_ref, *, n_ctx, ctx_len):
    i = pl.program_id(0)
    row_len = jnp.where(i * TB_CONV < n_ctx, ctx_len, GRID_W)
    pos = _row_pos(TB_CONV, row_len)
    x = x_ref[...].astype(F32)
    half = SSD_CONV // 2
    acc = x * w_ref[half:half + 1, :]
    for k in range(SSD_CONV):
        if k != half:
            acc = acc + _shift_rows(x, k - half, pos, row_len) * w_ref[k:k + 1, :]
    o_ref[...] = _silu(acc + b_ref[...]).astype(BF16)


def _conv_call(proj, conv_w, conv_b, n_ctx, ctx_len):
    t = proj.shape[0]
    tc = 1024
    kern = functools.partial(_conv_kernel, n_ctx=n_ctx, ctx_len=ctx_len)
    return pl.pallas_call(
        kern,
        out_shape=jax.ShapeDtypeStruct((t, XBC_DIM), BF16),
        grid=(t // TB_CONV, XBC_DIM // tc),
        in_specs=[
            pl.BlockSpec((TB_CONV, tc), lambda i, j: (i, COL_XBC // tc + j)),
            pl.BlockSpec((SSD_CONV, tc), lambda i, j: (0, j)),
            pl.BlockSpec((1, tc), lambda i, j: (0, j)),
        ],
        out_specs=pl.BlockSpec((TB_CONV, tc), lambda i, j: (i, j)),
        compiler_params=_cparams(("arbitrary", "arbitrary")),
        name="conv",
    )(proj, conv_w, conv_b.reshape(1, XBC_DIM))


def _ssd_kernel(*refs, has_h0, want_final):
    xs_ref, b_ref, c_ref, dt_ref = refs[:4]
    rest = list(refs[4:])
    h0_ref = rest.pop(0) if has_h0 else None
    alog_ref, bias_ref, dvec_ref, y_ref = rest[:4]
    hfin_ref = rest[4] if want_final else None
    h_scr = rest[-1]
    d = pl.program_id(1)
    st = pl.program_id(2)
    q = SSD_CHUNK
    n_chunks = TS_SSD // q
    is_fwd = d == 0

    @pl.when(st == 0)
    def _():
        if has_h0:
            h_scr[...] = h0_ref[0, 0]
        else:
            h_scr[...] = jnp.zeros_like(h_scr)

    a_neg = -jnp.exp(alog_ref[0])
    bias = bias_ref[0]
    row = lax.broadcasted_iota(jnp.int32, (q, q), 0)
    col = lax.broadcasted_iota(jnp.int32, (q, q), 1)
    keep = (row - col) * jnp.where(is_fwd, 1, -1) >= 0
    tri = keep.astype(BF16)
    lane_lo = lax.broadcasted_iota(jnp.int32, (q, LANES), 1) < SSD_HEADDIM
    row_lo = lax.broadcasted_iota(jnp.int32, (LANES, SSD_STATE), 0) < SSD_HEADDIM
    d_on = jnp.where(is_fwd, 1.0, 0.0)

    def chunk_body(ci, carry):
        c = jnp.where(is_fwd, ci, n_chunks - 1 - ci)
        c0 = pl.multiple_of(c * q, q)
        dt = _softplus(dt_ref[pl.ds(c0, q), :] + bias)
        da = dt * a_neg
        acs = _dot_exact_lhs(tri, da)
        acs_t = acs.T
        dt_t = dt.T
        tot_row = jnp.where(is_fwd, acs[q - 1:q, :], acs[0:1, :])
        tot_col = jnp.where(is_fwd, acs_t[:, q - 1:q], acs_t[:, 0:1])
        e_acs = jnp.exp(acs)
        dte = jnp.exp(tot_row - acs) * dt
        e_tot = jnp.exp(tot_col)

        for g in range(SSD_GROUPS):
            bg = b_ref[pl.ds(c0, q), g * SSD_STATE:(g + 1) * SSD_STATE]
            cg = c_ref[pl.ds(c0, q), g * SSD_STATE:(g + 1) * SSD_STATE]
            cb = lax.dot_general(cg, bg, (((1,), (1,)), ((), ())),
                                 preferred_element_type=F32)
            for p in range(SSD_HPG // 2):
                h0 = g * SSD_HPG + 2 * p
                lo = h0 * SSD_HEADDIM
                xp = xs_ref[pl.ds(c0, q), lo:lo + LANES]
                ws = []
                for hh in (h0, h0 + 1):
                    seg = acs[:, hh:hh + 1] - acs_t[hh:hh + 1, :]
                    wd = jnp.exp(jnp.where(keep, seg, -jnp.inf))
                    ws.append((cb * wd * dt_t[hh:hh + 1, :]).astype(BF16))
                y_diag = jnp.where(lane_lo, _dot(ws[0], xp), _dot(ws[1], xp))
                hp = h_scr[lo:lo + LANES, :]
                y_off = lax.dot_general(cg, hp.astype(BF16), (((1,), (1,)), ((), ())),
                                        preferred_element_type=F32)
                e_pair = jnp.where(lane_lo, e_acs[:, h0:h0 + 1], e_acs[:, h0 + 1:h0 + 2])
                xpf = xp.astype(F32)
                y = y_diag + y_off * e_pair + (d_on * dvec_ref[:, lo:lo + LANES]) * xpf
                y_ref[0, pl.ds(c0, q), lo:lo + LANES] = y.astype(BF16)
                w_pair = jnp.where(lane_lo, dte[:, h0:h0 + 1], dte[:, h0 + 1:h0 + 2])
                dtx = (xpf * w_pair).astype(BF16)
                s_pair = lax.dot_general(dtx, bg, (((0,), (0,)), ((), ())),
                                         preferred_element_type=F32)
                dec = jnp.where(row_lo, e_tot[h0:h0 + 1, :], e_tot[h0 + 1:h0 + 2, :])
                h_scr[lo:lo + LANES, :] = hp * dec + s_pair
        return carry

    lax.fori_loop(0, n_chunks, chunk_body, 0)

    if want_final:
        @pl.when(st == pl.num_programs(2) - 1)
        def _():
            hfin_ref[0, 0] = h_scr[...]


def _ssd_call(xbc, dt_raw, h0, a_log, dt_bias, dvec, tok0, n_seq, seq_len, want_final):
    n_steps = seq_len // TS_SSD
    blk0 = tok0 // TS_SSD

    def tok_blk(b, d, s):
        return blk0 + b * n_steps + jnp.where(d == 0, s, n_steps - 1 - s)

    state_spec = pl.BlockSpec((1, 1, D_INNER, SSD_STATE), lambda b, d, s: (b, d, 0, 0))
    in_specs = [
        pl.BlockSpec((TS_SSD, D_INNER), lambda b, d, s: (tok_blk(b, d, s), 0)),
        pl.BlockSpec((TS_SSD, 512), lambda b, d, s: (tok_blk(b, d, s), 4)),
        pl.BlockSpec((TS_SSD, 512), lambda b, d, s: (tok_blk(b, d, s), 5)),
        pl.BlockSpec((TS_SSD, LANES), lambda b, d, s: (tok_blk(b, d, s), 0)),
    ]
    args = [xbc, xbc, xbc, dt_raw]
    if h0 is not None:
        in_specs.append(state_spec)
        args.append(h0)
    in_specs += [
        pl.BlockSpec((1, 1, LANES), lambda b, d, s: (d, 0, 0)),
        pl.BlockSpec((1, 1, LANES), lambda b, d, s: (d, 0, 0)),
        pl.BlockSpec((1, D_INNER), lambda b, d, s: (0, 0)),
    ]
    args += [a_log, dt_bias, dvec]
    out_shape = [jax.ShapeDtypeStruct((2, n_seq * seq_len, D_INNER), BF16)]
    out_specs = [pl.BlockSpec((1, TS_SSD, D_INNER),
                              lambda b, d, s: (d, tok_blk(b, d, s) - blk0, 0))]
    if want_final:
        out_shape.append(jax.ShapeDtypeStruct((n_seq, 2, D_INNER, SSD_STATE), F32))
        out_specs.append(state_spec)
    kern = functools.partial(_ssd_kernel, has_h0=h0 is not None, want_final=want_final)
    return pl.pallas_call(
        kern,
        out_shape=tuple(out_shape),
        grid=(n_seq, 2, n_steps),
        in_specs=in_specs,
        out_specs=tuple(out_specs),
        scratch_shapes=[pltpu.VMEM((D_INNER, SSD_STATE), F32)],
        compiler_params=_cparams(("arbitrary", "arbitrary", "arbitrary")),
        name="ssd_final" if want_final else "ssd",
    )(*args)


def _post_kernel(x_ref, mod_ref, z_ref, scb_ref, scc_ref, scv_ref, gssd_ref, gsc_ref, y_ref,
                 normw_ref, wssd_ref, scw_ref, wsc_ref, wo_ref, g1_ref, b1_ref, wr_ref, br_ref,
                 x1_ref, h2_ref, ti_ref, tp_ref, *, n_ctx, ctx_len, lat_len):
    i = pl.program_id(0)
    tm = TM_POST
    r = _mod_row(i, tm, n_ctx, lat_len)

    def mod_vec(k):
        return mod_ref[pl.ds(r, 1), k * D_MODEL:(k + 1) * D_MODEL]

    gate1, shift2, scale2 = mod_vec(2), mod_vec(3), mod_vec(4)

    y = (y_ref[0].astype(F32) + y_ref[1].astype(F32)) * _silu(z_ref[...].astype(F32))
    gw = D_INNER // SSD_GROUPS
    parts = []
    for g in range(SSD_GROUPS):
        yg = y[:, g * gw:(g + 1) * gw]
        ms = jnp.mean(yg * yg, axis=-1, keepdims=True)
        parts.append(yg * lax.rsqrt(ms + RMS_EPS))
    yn = (jnp.concatenate(parts, axis=-1) * normw_ref[...]).astype(BF16)
    y_ssd = _dot(yn, wssd_ref[...])

    row_len = jnp.where(i * tm < n_ctx, ctx_len, GRID_W)
    pos = _row_pos(tm, row_len)
    u = scc_ref[...].astype(F32) * scv_ref[...].astype(F32)
    half = SC_CONV // 2
    cv = u * scw_ref[half:half + 1, :]
    for k in range(SC_CONV):
        if k != half:
            cv = cv + _shift_rows(u, k - half, pos, row_len) * scw_ref[k:k + 1, :]
    y_sc = _dot((scb_ref[...].astype(F32) * cv).astype(BF16), wsc_ref[...])

    mixed = (jax.nn.sigmoid(gssd_ref[...].astype(F32)) * y_ssd
             + jax.nn.sigmoid(gsc_ref[...].astype(F32)) * y_sc)
    o = _dot(mixed.astype(BF16), wo_ref[...])

    res = DEEPNORM_ALPHA * x_ref[...] + gate1 * o
    mu = jnp.mean(res, axis=-1, keepdims=True)
    cen = res - mu
    var = jnp.mean(cen * cen, axis=-1, keepdims=True)
    x1 = cen * lax.rsqrt(var + LN_EPS) * g1_ref[...] + b1_ref[...]
    x1_ref[...] = x1
    h2 = x1 * (1.0 + scale2) + shift2
    h2_ref[...] = h2

    logits = _dot_x3(h2, wr_ref[...]) + br_ref[...]
    lane = lax.broadcasted_iota(jnp.int32, (tm, LANES), 1)
    neg = jnp.float32(-jnp.inf)
    work = jnp.where(lane < N_EXPERTS, logits, neg)
    vals, idxs = [], []
    for _ in range(TOP_K):
        m = jnp.max(work, axis=-1, keepdims=True)
        idx = jnp.min(jnp.where(work == m, lane, LANES), axis=-1, keepdims=True)
        vals.append(m)
        idxs.append(idx)
        work = jnp.where(lane == idx, neg, work)
    es = [jnp.exp(v - vals[0]) for v in vals]
    denom = es[0] + es[1] + es[2] + es[3]
    ti = jnp.zeros((tm, LANES), jnp.int32)
    tp = jnp.zeros((tm, LANES), F32)
    for k in range(TOP_K):
        ti = jnp.where(lane == k, idxs[k], ti)
        tp = jnp.where(lane == k, es[k] / denom, tp)
    ti_ref[...] = ti
    tp_ref[...] = tp


def _post_call(x_all, mod, proj, y2, norm_w, w_ssd, sc_w, w_sc, w_o, ln_g, ln_b, w_r, b_r,
               n_ctx, ctx_len, lat_len):
    t = x_all.shape[0]
    tm = TM_POST
    kern = functools.partial(_post_kernel, n_ctx=n_ctx, ctx_len=ctx_len, lat_len=lat_len)

    def colblk(off, width):
        return pl.BlockSpec((tm, width), lambda i: (i, off // width))

    def whole(a):
        return pl.BlockSpec(a.shape, lambda i: (0,) * a.ndim)

    return pl.pallas_call(
        kern,
        out_shape=(jax.ShapeDtypeStruct((t, D_MODEL), F32),
                   jax.ShapeDtypeStruct((t, D_MODEL), F32),
                   jax.ShapeDtypeStruct((t, LANES), jnp.int32),
                   jax.ShapeDtypeStruct((t, LANES), F32)),
        grid=(t // tm,),
        in_specs=[
            pl.BlockSpec((tm, D_MODEL), lambda i: (i, 0)),
            whole(mod),
            colblk(COL_Z, D_INNER),
            colblk(COL_SCB, D_MODEL), colblk(COL_SCC, D_MODEL), colblk(COL_SCV, D_MODEL),
            colblk(COL_GSSD, D_MODEL), colblk(COL_GSC, D_MODEL),
            pl.BlockSpec((2, tm, D_INNER), lambda i: (0, i, 0)),
            whole(norm_w), whole(w_ssd), whole(sc_w), whole(w_sc), whole(w_o),
            whole(ln_g), whole(ln_b), whole(w_r), whole(b_r),
        ],
        out_specs=(pl.BlockSpec((tm, D_MODEL), lambda i: (i, 0)),
                   pl.BlockSpec((tm, D_MODEL), lambda i: (i, 0)),
                   pl.BlockSpec((tm, LANES), lambda i: (i, 0)),
                   pl.BlockSpec((tm, LANES), lambda i: (i, 0))),
        compiler_params=_cparams(("arbitrary",)),
        name="post",
    )(x_all, mod, proj, proj, proj, proj, proj, proj, y2,
      norm_w, w_ssd, sc_w, w_sc, w_o, ln_g, ln_b, w_r, b_r)


def _dispatch_kernel(pos_ref, h_ref, o_hbm, sem):
    tb = TB_MOVE

    def issue(t, carry):
        for k in range(TOP_K):
            dst = pos_ref[0, 0, t * TOP_K + k]
            pltpu.make_async_copy(h_ref.at[pl.ds(t, 1)], o_hbm.at[pl.ds(dst, 1)], sem).start()
        return carry

    lax.fori_loop(0, tb, issue, 0)
    for _ in range(TOP_K):
        pltpu.make_async_copy(h_ref, o_hbm.at[pl.ds(0, tb)], sem).wait()


def _dispatch_call(pos_blocks, h2):
    t = h2.shape[0]
    tb = TB_MOVE
    return pl.pallas_call(
        _dispatch_kernel,
        out_shape=jax.ShapeDtypeStruct((t * TOP_K, D_MODEL), F32),
        grid=(t // tb,),
        in_specs=[
            pl.BlockSpec((1, 1, tb * TOP_K), lambda i: (i, 0, 0), memory_space=pltpu.SMEM),
            pl.BlockSpec((tb, D_MODEL), lambda i: (i, 0)),
        ],
        out_specs=pl.BlockSpec(memory_space=pl.ANY),
        scratch_shapes=[pltpu.SemaphoreType.DMA(())],
        compiler_params=pltpu.CompilerParams(dimension_semantics=("arbitrary",),
                                             vmem_limit_bytes=VMEM_LIMIT,
                                             has_side_effects=True),
        name="dispatch",
    )(pos_blocks, h2)


def _ffn_kernel(tile_ref, exp_ref, lo_ref, hi_ref, x_ref, wgu_ref, bgu_ref, wd_ref, bd_ref,
                o_ref, wgu_scr, wd_scr):
    s = pl.program_id(0)
    prev = jnp.maximum(s - 1, 0)
    new_tile = (s == 0) | (tile_ref[s] != tile_ref[prev])
    new_expert = (s == 0) | (exp_ref[s] != exp_ref[prev])
    lo = lo_ref[s]
    hi = hi_ref[s]

    @pl.when(new_tile)
    def _():
        o_ref[...] = jnp.zeros_like(o_ref)

    @pl.when(hi > lo)
    def _():
        @pl.when(new_expert)
        def _():
            wgu_scr[...] = wgu_ref[0].astype(BF16)
            wd_scr[...] = wd_ref[0].astype(BF16)

        gu = _dot(x_ref[...].astype(BF16), wgu_scr[...]) + bgu_ref[0]
        g = jnp.minimum(gu[:, :D_EXPERT], SWIGLU_LIMIT)
        u = jnp.clip(gu[:, D_EXPERT:], -SWIGLU_LIMIT, SWIGLU_LIMIT)
        act = (u + 1.0) * g * jax.nn.sigmoid(SWIGLU_ALPHA * g)
        y = _dot(act.astype(BF16), wd_scr[...]) + bd_ref[0]
        rowi = lax.broadcasted_iota(jnp.int32, (TM_FFN, 1), 0)
        mine = (rowi >= lo) & (rowi < hi)
        o_ref[...] = jnp.where(mine, y, o_ref[...])


def _ffn_call(tile_id, exp_id, lo, hi, x_sorted, w_gu, b_gu, w_d, b_d):
    n_rows = x_sorted.shape[0]
    n_steps = tile_id.shape[0]
    tm = TM_FFN
    grid_spec = pltpu.PrefetchScalarGridSpec(
        num_scalar_prefetch=4,
        grid=(n_steps,),
        in_specs=[
            pl.BlockSpec((tm, D_MODEL), lambda s, ti, ei, lo_, hi_: (ti[s], 0)),
            pl.BlockSpec((1, D_MODEL, 2 * D_EXPERT), lambda s, ti, ei, lo_, hi_: (ei[s], 0, 0)),
            pl.BlockSpec((1, 1, 2 * D_EXPERT), lambda s, ti, ei, lo_, hi_: (ei[s], 0, 0)),
            pl.BlockSpec((1, D_EXPERT, D_MODEL), lambda s, ti, ei, lo_, hi_: (ei[s], 0, 0)),
            pl.BlockSpec((1, 1, D_MODEL), lambda s, ti, ei, lo_, hi_: (ei[s], 0, 0)),
        ],
        out_specs=pl.BlockSpec((tm, D_MODEL), lambda s, ti, ei, lo_, hi_: (ti[s], 0)),
        scratch_shapes=[pltpu.VMEM((D_MODEL, 2 * D_EXPERT), BF16),
                        pltpu.VMEM((D_EXPERT, D_MODEL), BF16)],
    )
    return pl.pallas_call(
        _ffn_kernel,
        out_shape=jax.ShapeDtypeStruct((n_rows, D_MODEL), F32),
        grid_spec=grid_spec,
        compiler_params=_cparams(("arbitrary",)),
        name="ffn",
    )(tile_id, exp_id, lo, hi, x_sorted, w_gu, b_gu, w_d, b_d)


def _combine_kernel(pos_ref, x1_ref, tp_ref, mod_ref, g2_ref, b2_ref, y_hbm, o_ref,
                    rows_scr, sem, *, n_ctx, lat_len):
    i = pl.program_id(0)
    tb = TB_MOVE

    def issue(t, carry):
        for k in range(TOP_K):
            src = pos_ref[0, 0, t * TOP_K + k]
            pltpu.make_async_copy(y_hbm.at[pl.ds(src, 1)], rows_scr.at[k, pl.ds(t, 1)], sem).start()
        return carry

    lax.fori_loop(0, tb, issue, 0)
    for k in range(TOP_K):
        pltpu.make_async_copy(y_hbm.at[pl.ds(0, tb)], rows_scr.at[k], sem).wait()

    r = _mod_row(i, tb, n_ctx, lat_len)
    gate2 = mod_ref[pl.ds(r, 1), 5 * D_MODEL:6 * D_MODEL]
    tp = tp_ref[...]
    moe = tp[:, 0:1] * rows_scr[0]
    for k in range(1, TOP_K):
        moe = moe + tp[:, k:k + 1] * rows_scr[k]
    res = DEEPNORM_ALPHA * x1_ref[...] + gate2 * moe
    mu = jnp.mean(res, axis=-1, keepdims=True)
    cen = res - mu
    var = jnp.mean(cen * cen, axis=-1, keepdims=True)
    o_ref[...] = cen * lax.rsqrt(var + LN_EPS) * g2_ref[...] + b2_ref[...]


def _combine_call(pos_blocks, x1, tp, mod, ln_g, ln_b, y_sorted, n_ctx, lat_len):
    t = x1.shape[0]
    tb = TB_MOVE
    kern = functools.partial(_combine_kernel, n_ctx=n_ctx, lat_len=lat_len)
    return pl.pallas_call(
        kern,
        out_shape=jax.ShapeDtypeStruct((t, D_MODEL), F32),
        grid=(t // tb,),
        in_specs=[
            pl.BlockSpec((1, 1, tb * TOP_K), lambda i: (i, 0, 0), memory_space=pltpu.SMEM),
            pl.BlockSpec((tb, D_MODEL), lambda i: (i, 0)),
            pl.BlockSpec((tb, LANES), lambda i: (i, 0)),
            pl.BlockSpec(mod.shape, lambda i: (0, 0)),
            pl.BlockSpec((1, D_MODEL), lambda i: (0, 0)),
            pl.BlockSpec((1, D_MODEL), lambda i: (0, 0)),
            pl.BlockSpec(memory_space=pl.ANY),
        ],
        out_specs=pl.BlockSpec((tb, D_MODEL), lambda i: (i, 0)),
        scratch_shapes=[pltpu.VMEM((TOP_K, tb, D_MODEL), F32), pltpu.SemaphoreType.DMA(())],
        compiler_params=_cparams(("arbitrary",)),
        name="combine",
    )(pos_blocks, x1, tp, mod, ln_g, ln_b, y_sorted)


def _routing_tables(top_i, n_tokens):
    onehot = (top_i[:, :, None] == jnp.arange(N_EXPERTS, dtype=jnp.int32)).astype(jnp.int32)
    per_tok = onehot.sum(axis=1)
    counts = per_tok.sum(axis=0)
    offs = jnp.cumsum(counts) - counts
    rank = jnp.cumsum(per_tok, axis=0) - per_tok
    pos = offs[top_i] + jnp.take_along_axis(rank, top_i, axis=1)

    n_rows = n_tokens * TOP_K
    n_tiles = n_rows // TM_FFN
    n_steps = n_tiles + N_EXPERTS - 1
    first = offs // TM_FFN
    last = jnp.where(counts > 0, (offs + counts - 1) // TM_FFN, first - 1)
    items = last - first + 1
    item_end = jnp.cumsum(items)
    item_start = item_end - items
    step = jnp.arange(n_steps, dtype=jnp.int32)
    total = item_end[-1]
    e_of = jnp.searchsorted(item_end, jnp.minimum(step, total - 1), side="right").astype(jnp.int32)
    e_of = jnp.minimum(e_of, N_EXPERTS - 1)
    tile = first[e_of] + (step - item_start[e_of])
    live = step < total
    tile = jnp.where(live, tile, n_tiles - 1).astype(jnp.int32)
    lo = jnp.clip(offs[e_of] - tile * TM_FFN, 0, TM_FFN)
    hi = jnp.clip(offs[e_of] + counts[e_of] - tile * TM_FFN, 0, TM_FFN)
    lo = jnp.where(live, lo, 0).astype(jnp.int32)
    hi = jnp.where(live, hi, 0).astype(jnp.int32)
    return pos.astype(jnp.int32), tile, e_of, lo, hi


def kernel(x_prompt, x_sample, c, state_ssd_fwd, state_ssd_bwd, c_ctx, w_ada, b_ada, w_in,
           ssd_conv_w, ssd_conv_b, ssd_a_log, ssd_dt_bias, ssd_d, ssd_norm_w, ssd_w_out,
           sc_conv_w, sc_w_out, w_o, ln1_g, ln1_b, w_router, b_router, w_gate_up, b_gate_up,
           w_down, b_down, ln2_g, ln2_b):
    n_ctx_req, ctx_len, _ = x_prompt.shape
    n_lat_req, lat_len, _ = x_sample.shape
    n_ctx = n_ctx_req * ctx_len
    n_lat = n_lat_req * lat_len
    t = n_ctx + n_lat
    assert w_ada.shape[0] == 1, "single trunk layer"
    assert ctx_len % TS_SSD == 0 and lat_len % TS_SSD == 0 and TS_SSD % GRID_W == 0
    assert n_ctx % TM_PROJ == 0 and lat_len % TM_PROJ == 0 and TM_POST % ctx_len == 0
    assert TB_CONV % ctx_len == 0 and ctx_len & (ctx_len - 1) == 0

    x_all = jnp.concatenate([x_prompt.reshape(n_ctx, D_MODEL), x_sample.reshape(n_lat, D_MODEL)], 0)

    cvec = jnp.concatenate([c_ctx[None, :], c, jnp.zeros((7 - n_lat_req, D_MODEL), F32)], 0)
    mod = _mod_call(cvec, w_ada[0], b_ada[0])

    w = w_in[0]
    o_dt = D_INNER + XBC_DIM
    w_main = jnp.concatenate([w[:, :o_dt], w[:, o_dt + SSD_HEADS:]], axis=1).astype(BF16)
    w_dt = jnp.pad(w[:, o_dt:o_dt + SSD_HEADS], ((0, 0), (0, LANES - SSD_HEADS))).astype(BF16)
    proj, dt_raw = _inproj_call(x_all, mod, w_main, w_dt, n_ctx, lat_len)

    xbc = _conv_call(proj, ssd_conv_w[0], ssd_conv_b[0], n_ctx, ctx_len)

    pad_h = ((0, 0), (0, LANES - SSD_HEADS))
    a_log = jnp.pad(ssd_a_log[0], pad_h).reshape(2, 1, LANES)
    dt_bias = jnp.pad(ssd_dt_bias[0], pad_h).reshape(2, 1, LANES)
    dvec = jnp.repeat(ssd_d[0], SSD_HEADDIM).reshape(1, D_INNER)
    h0_lat = jnp.stack([state_ssd_fwd[:, 0].reshape(n_lat_req, D_INNER, SSD_STATE),
                        state_ssd_bwd[:, 0].reshape(n_lat_req, D_INNER, SSD_STATE)], axis=1)
    y_ctx, h_ctx = _ssd_call(xbc, dt_raw, None, a_log, dt_bias, dvec, 0, n_ctx_req, ctx_len, True)
    (y_lat,) = _ssd_call(xbc, dt_raw, h0_lat, a_log, dt_bias, dvec, n_ctx, n_lat_req, lat_len, False)
    y2 = jnp.concatenate([y_ctx, y_lat], axis=1)

    w_r = jnp.pad(w_router[0], ((0, 0), (0, LANES - N_EXPERTS)))
    b_r = jnp.pad(b_router[0], (0, LANES - N_EXPERTS)).reshape(1, LANES)
    x1, h2, top_i, top_p = _post_call(
        x_all, mod, proj, y2, ssd_norm_w[0].reshape(1, D_INNER), ssd_w_out[0].astype(BF16),
        sc_conv_w[0], sc_w_out[0].astype(BF16), w_o[0].astype(BF16),
        ln1_g[0].reshape(1, D_MODEL), ln1_b[0].reshape(1, D_MODEL), w_r, b_r,
        n_ctx, ctx_len, lat_len)

    pos, tile_id, exp_id, lo, hi = _routing_tables(top_i[:, :TOP_K], t)
    pos_blocks = pos.reshape(t // TB_MOVE, 1, TB_MOVE * TOP_K)
    x_sorted = _dispatch_call(pos_blocks, h2)
    y_sorted = _ffn_call(tile_id, exp_id, lo, hi, x_sorted,
                         w_gate_up[0], b_gate_up[0].reshape(N_EXPERTS, 1, 2 * D_EXPERT),
                         w_down[0], b_down[0].reshape(N_EXPERTS, 1, D_MODEL))
    out = _combine_call(pos_blocks, x1, top_p, mod, ln2_g[0].reshape(1, D_MODEL),
                        ln2_b[0].reshape(1, D_MODEL), y_sorted, n_ctx, lat_len)

    y_prompt = out[:n_ctx].reshape(n_ctx_req, ctx_len, D_MODEL)
    y_sample = out[n_ctx:].reshape(n_lat_req, lat_len, D_MODEL)
    new_f = h_ctx[:, 0].reshape(n_ctx_req, 1, SSD_HEADS, SSD_HEADDIM, SSD_STATE)
    new_b = h_ctx[:, 1].reshape(n_ctx_req, 1, SSD_HEADS, SSD_HEADDIM, SSD_STATE)
    return (y_prompt, y_sample, new_f, new_b)
```

```python
import functools

import jax
import jax.numpy as jnp
from jax import lax
from jax.experimental import pallas as pl
from jax.experimental.pallas import tpu as pltpu

F32 = jnp.float32
BF16 = jnp.bfloat16

D_MODEL = 1024
GRID_W = 64
D_INNER = 2048
SSD_HEADDIM = 64
SSD_HEADS = 32
SSD_GROUPS = 4
SSD_HPG = 8
SSD_STATE = 128
SSD_CONV = 5
SSD_CHUNK = 128
XBC_DIM = D_INNER + 2 * SSD_GROUPS * SSD_STATE
SC_CONV = 3
N_EXPERTS = 32
TOP_K = 4
D_EXPERT = 1024
SWIGLU_LIMIT = 7.0
SWIGLU_ALPHA = 1.702
LN_EPS = 1e-5
RMS_EPS = 1e-5
DEEPNORM_ALPHA = 2.0 ** 0.25
LOG2E = 1.4426950408889634

LANES = 128
MAIN_COLS = 10240
COL_Z = 0
COL_XBC = 2048
COL_SCB = 5120
COL_SCC = 6144
COL_SCV = 7168
COL_GSSD = 8192
COL_GSC = 9216

VMEM_LIMIT = 56 * 1024 * 1024

TM_PROJ = 1024
TN_PROJ = 2048
TB_CONV = 256
TS_SSD = 256
TM_POST = 256
TB_MOVE = 256
TM_FFN = 256


def _cparams(sem):
    return pltpu.CompilerParams(dimension_semantics=sem, vmem_limit_bytes=VMEM_LIMIT)


def _split3(v):
    hi = v.astype(BF16)
    r1 = v - hi.astype(F32)
    mid = r1.astype(BF16)
    lo = (r1 - mid.astype(F32)).astype(BF16)
    return hi, mid, lo


def _dot(a, b):
    return jnp.dot(a, b, preferred_element_type=F32)


def _dot_exact_lhs(m_bf16, v_f32):
    hi, mid, lo = _split3(v_f32)
    return _dot(m_bf16, hi) + _dot(m_bf16, mid) + _dot(m_bf16, lo)


def _dot_x3(a_f32, b_f32):
    a_hi = a_f32.astype(BF16)
    a_lo = (a_f32 - a_hi.astype(F32)).astype(BF16)
    b_hi = b_f32.astype(BF16)
    b_lo = (b_f32 - b_hi.astype(F32)).astype(BF16)
    return _dot(a_hi, b_hi) + _dot(a_lo, b_hi) + _dot(a_hi, b_lo)


def _silu(v):
    return v * jax.nn.sigmoid(v)


def _softplus(v):
    return jnp.maximum(v, 0.0) + jnp.log(1.0 + jnp.exp(-jnp.abs(v)))


def _mod_kernel(c_ref, w_ref, b_ref, o_ref):
    o_ref[...] = _dot_x3(_silu(c_ref[...]), w_ref[...]) + b_ref[...]


def _mod_call(cvec, w_ada, b_ada):
    n = w_ada.shape[1]
    tn = 1536
    return pl.pallas_call(
        _mod_kernel,
        out_shape=jax.ShapeDtypeStruct((cvec.shape[0], n), F32),
        grid=(n // tn,),
        in_specs=[
            pl.BlockSpec(cvec.shape, lambda j: (0, 0)),
            pl.BlockSpec((D_MODEL, tn), lambda j: (0, j)),
            pl.BlockSpec((1, tn), lambda j: (0, j)),
        ],
        out_specs=pl.BlockSpec((cvec.shape[0], tn), lambda j: (0, j)),
        compiler_params=_cparams(("arbitrary",)),
        name="mod",
    )(cvec, w_ada, b_ada.reshape(1, n))


def _mod_row(block, rows_per_block, n_ctx_tokens, lat_len):
    tok = block * rows_per_block
    return jnp.where(tok < n_ctx_tokens, 0, 1 + (tok - n_ctx_tokens) // lat_len)


def _ctx_lat_specs(block_shape, n_ctx_blocks, lead=()):
    tail = (0,) * (len(block_shape) - len(lead) - 1)
    ctx = pl.BlockSpec(block_shape, lambda i, *_: lead + (jnp.minimum(i, n_ctx_blocks - 1),) + tail)
    lat = pl.BlockSpec(block_shape, lambda i, *_: lead + (jnp.maximum(i - n_ctx_blocks, 0),) + tail)
    return ctx, lat


def _inproj_kernel(xc_ref, xl_ref, mod_ref, w_ref, wdt_ref, o_ref, dt_ref, h_scr, *, n_ctx, lat_len):
    i = pl.program_id(0)
    j = pl.program_id(1)

    @pl.when(j == 0)
    def _():
        r = _mod_row(i, TM_PROJ, n_ctx, lat_len)
        shift = mod_ref[pl.ds(r, 1), 0:D_MODEL]
        scale = mod_ref[pl.ds(r, 1), D_MODEL:2 * D_MODEL]
        x = jnp.where(i * TM_PROJ < n_ctx, xc_ref[...], xl_ref[...])
        h = (x * (1.0 + scale) + shift).astype(BF16)
        h_scr[...] = h
        dt_ref[...] = _dot(h, wdt_ref[...])

    o_ref[...] = _dot(h_scr[...], w_ref[...]).astype(BF16)


def _inproj_call(x_ctx, x_lat, mod, w_main, w_dt, lat_len):
    n_ctx = x_ctx.shape[0]
    t = n_ctx + x_lat.shape[0]
    kern = functools.partial(_inproj_kernel, n_ctx=n_ctx, lat_len=lat_len)
    return pl.pallas_call(
        kern,
        out_shape=(jax.ShapeDtypeStruct((t, MAIN_COLS), BF16),
                   jax.ShapeDtypeStruct((t, LANES), F32)),
        grid=(t // TM_PROJ, MAIN_COLS // TN_PROJ),
        in_specs=[
            *_ctx_lat_specs((TM_PROJ, D_MODEL), n_ctx // TM_PROJ),
            pl.BlockSpec(mod.shape, lambda i, j: (0, 0)),
            pl.BlockSpec((D_MODEL, TN_PROJ), lambda i, j: (0, j)),
            pl.BlockSpec((D_MODEL, LANES), lambda i, j: (0, 0)),
        ],
        out_specs=(pl.BlockSpec((TM_PROJ, TN_PROJ), lambda i, j: (i, j)),
                   pl.BlockSpec((TM_PROJ, LANES), lambda i, j: (i, 0))),
        scratch_shapes=[pltpu.VMEM((TM_PROJ, D_MODEL), BF16)],
        compiler_params=_cparams(("arbitrary", "arbitrary")),
        name="inproj",
    )(x_ctx, x_lat, mod, w_main, w_dt)


def _shift_rows(x, off, pos, row_len):
    if off == 0:
        return x
    n = x.shape[0]
    rolled = pltpu.roll(x, (-off) % n, 0)
    ok = (pos + off >= 0) & (pos + off < row_len)
    return jnp.where(ok, rolled, 0.0)


def _row_pos(n, row_len):
    t = lax.broadcasted_iota(jnp.int32, (n, 1), 0)
    return jnp.bitwise_and(t, row_len - 1)


def _conv_kernel(x_ref, w_ref, b_ref, o_ref, *, n_ctx, ctx_len):
    i = pl.program_id(0)
    row_len = jnp.where(i * TB_CONV < n_ctx, ctx_len, GRID_W)
    pos = _row_pos(TB_CONV, row_len)
    x = x_ref[...].astype(F32)
    half = SSD_CONV // 2
    acc = x * w_ref[half:half + 1, :]
    for k in range(SSD_CONV):
        if k != half:
            acc = acc + _shift_rows(x, k - half, pos, row_len) * w_ref[k:k + 1, :]
    o_ref[...] = _silu(acc + b_ref[...]).astype(BF16)


def _conv_call(proj, conv_w, conv_b, n_ctx, ctx_len):
    t = proj.shape[0]
    tc = 1024
    kern = functools.partial(_conv_kernel, n_ctx=n_ctx, ctx_len=ctx_len)
    return pl.pallas_call(
        kern,
        out_shape=jax.ShapeDtypeStruct((t, XBC_DIM), BF16),
        grid=(t // TB_CONV, XBC_DIM // tc),
        in_specs=[
            pl.BlockSpec((TB_CONV, tc), lambda i, j: (i, COL_XBC // tc + j)),
            pl.BlockSpec((SSD_CONV, tc), lambda i, j: (0, j)),
            pl.BlockSpec((1, tc), lambda i, j: (0, j)),
        ],
        out_specs=pl.BlockSpec((TB_CONV, tc), lambda i, j: (i, j)),
        compiler_params=_cparams(("arbitrary", "arbitrary")),
        name="conv",
    )(proj, conv_w, conv_b.reshape(1, XBC_DIM))


def _ssd_kernel(*refs, has_h0, want_final):
    xs_ref, b_ref, c_ref, dt_ref = refs[:4]
    rest = list(refs[4:])
    h0_ref = rest.pop(0) if has_h0 else None
    alog_ref, bias_ref, dvec_ref, y_ref = rest[:4]
    hfin_ref = rest[4] if want_final else None
    ht_scr = rest[-1]
    d = pl.program_id(1)
    st = pl.program_id(2)
    q = SSD_CHUNK
    n_chunks = TS_SSD // q
    n_blk = D_INNER // LANES
    is_fwd = d == 0

    @pl.when(st == 0)
    def _():
        if has_h0:
            for j in range(n_blk):
                ht_scr[:, j * LANES:(j + 1) * LANES] = h0_ref[0, 0, j * LANES:(j + 1) * LANES, :].T
        else:
            ht_scr[...] = jnp.zeros_like(ht_scr)

    a2_neg = -jnp.exp(alog_ref[0]) * LOG2E
    bias = bias_ref[0]
    row = lax.broadcasted_iota(jnp.int32, (q, q), 0)
    col = lax.broadcasted_iota(jnp.int32, (q, q), 1)
    keep = (row - col) * jnp.where(is_fwd, 1, -1) >= 0
    tri = keep.astype(BF16)
    lane_lo = lax.broadcasted_iota(jnp.int32, (q, LANES), 1) < SSD_HEADDIM
    lane_lo1 = lane_lo[0:1, :]
    d_on = jnp.where(is_fwd, 1.0, 0.0)
    neg_inf = jnp.float32(-jnp.inf)

    def chunk_body(ci, carry):
        c = jnp.where(is_fwd, ci, n_chunks - 1 - ci)
        c0 = pl.multiple_of(c * q, q)
        dt = _softplus(dt_ref[pl.ds(c0, q), :] + bias)
        acs = _dot_exact_lhs(tri, dt * a2_neg)
        acs_t = acs.T
        dt_t = dt.T
        tot_row = jnp.where(is_fwd, acs[q - 1:q, :], acs[0:1, :])
        tot_col = jnp.where(is_fwd, acs_t[:, q - 1:q], acs_t[:, 0:1])
        e_tot = jnp.exp2(tot_row)
        src_t = acs_t - jnp.log2(dt_t)
        dte_t = jnp.exp2(tot_col - acs_t) * dt_t

        for g in range(SSD_GROUPS):
            bg = b_ref[pl.ds(c0, q), g * SSD_STATE:(g + 1) * SSD_STATE]
            cg = c_ref[pl.ds(c0, q), g * SSD_STATE:(g + 1) * SSD_STATE]
            cb = lax.dot_general(cg, bg, (((1,), (1,)), ((), ())),
                                 preferred_element_type=F32)
            cgf = cg.astype(F32)
            bg_t = bg.astype(F32).T
            for p in range(SSD_HPG // 2):
                h0 = g * SSD_HPG + 2 * p
                lo = h0 * SSD_HEADDIM
                xp = xs_ref[pl.ds(c0, q), lo:lo + LANES]
                htp = ht_scr[:, lo:lo + LANES]
                rhs = jnp.concatenate([xp, htp.astype(BF16)], axis=0)
                ys, sts = [], []
                for hh in (h0, h0 + 1):
                    tgt = jnp.broadcast_to(acs[:, hh:hh + 1], (q, q))
                    seg = jnp.where(keep, tgt - src_t[hh:hh + 1, :], neg_inf)
                    w = (cb * jnp.exp2(seg)).astype(BF16)
                    ce = (cgf * jnp.exp2(tgt)).astype(BF16)
                    ys.append(_dot(jnp.concatenate([w, ce], axis=1), rhs))
                    sts.append(_dot((bg_t * dte_t[hh:hh + 1, :]).astype(BF16), xp))
                y = (jnp.where(lane_lo, ys[0], ys[1])
                     + (d_on * dvec_ref[:, lo:lo + LANES]) * xp.astype(F32))
                y_ref[0, pl.ds(c0, q), lo:lo + LANES] = y.astype(BF16)
                dec = jnp.where(lane_lo1, e_tot[:, h0:h0 + 1], e_tot[:, h0 + 1:h0 + 2])
                ht_scr[:, lo:lo + LANES] = htp * dec + jnp.where(lane_lo, sts[0], sts[1])
        return carry

    lax.fori_loop(0, n_chunks, chunk_body, 0)

    if want_final:
        @pl.when(st == pl.num_programs(2) - 1)
        def _():
            for j in range(n_blk):
                hfin_ref[0, 0, j * LANES:(j + 1) * LANES, :] = ht_scr[:, j * LANES:(j + 1) * LANES].T


def _ssd_call(xbc, dt_raw, h0, a_log, dt_bias, dvec, tok0, n_seq, seq_len, want_final):
    n_steps = seq_len // TS_SSD
    blk0 = tok0 // TS_SSD

    def tok_blk(b, d, s):
        return blk0 + b * n_steps + jnp.where(d == 0, s, n_steps - 1 - s)

    state_spec = pl.BlockSpec((1, 1, D_INNER, SSD_STATE), lambda b, d, s: (b, d, 0, 0))
    in_specs = [
        pl.BlockSpec((TS_SSD, D_INNER), lambda b, d, s: (tok_blk(b, d, s), 0)),
        pl.BlockSpec((TS_SSD, 512), lambda b, d, s: (tok_blk(b, d, s), 4)),
        pl.BlockSpec((TS_SSD, 512), lambda b, d, s: (tok_blk(b, d, s), 5)),
        pl.BlockSpec((TS_SSD, LANES), lambda b, d, s: (tok_blk(b, d, s), 0)),
    ]
    args = [xbc, xbc, xbc, dt_raw]
    if h0 is not None:
        in_specs.append(state_spec)
        args.append(h0)
    in_specs += [
        pl.BlockSpec((1, 1, LANES), lambda b, d, s: (d, 0, 0)),
        pl.BlockSpec((1, 1, LANES), lambda b, d, s: (d, 0, 0)),
        pl.BlockSpec((1, D_INNER), lambda b, d, s: (0, 0)),
    ]
    args += [a_log, dt_bias, dvec]
    out_shape = [jax.ShapeDtypeStruct((2, n_seq * seq_len, D_INNER), BF16)]
    out_specs = [pl.BlockSpec((1, TS_SSD, D_INNER),
                              lambda b, d, s: (d, tok_blk(b, d, s) - blk0, 0))]
    if want_final:
        out_shape.append(jax.ShapeDtypeStruct((2, n_seq, D_INNER, SSD_STATE), F32))
        out_specs.append(pl.BlockSpec((1, 1, D_INNER, SSD_STATE), lambda b, d, s: (d, b, 0, 0)))
    kern = functools.partial(_ssd_kernel, has_h0=h0 is not None, want_final=want_final)
    return pl.pallas_call(
        kern,
        out_shape=tuple(out_shape),
        grid=(n_seq, 2, n_steps),
        in_specs=in_specs,
        out_specs=tuple(out_specs),
        scratch_shapes=[pltpu.VMEM((SSD_STATE, D_INNER), F32)],
        compiler_params=_cparams(("arbitrary", "arbitrary", "arbitrary")),
        name="ssd_final" if want_final else "ssd",
    )(*args)


def _post_kernel(xc_ref, xl_ref, mod_ref, z_ref, scb_ref, scc_ref, scv_ref, gssd_ref, gsc_ref,
                 yc_ref, yl_ref, normw_ref, wssd_ref, scw_ref, wsc_ref, wo_ref, g1_ref, b1_ref,
                 wr_ref, br_ref, x1_ref, h2_ref, ti_ref, tp_ref, cnt_ref, cnt_scr,
                 *, n_ctx, ctx_len, lat_len):
    i = pl.program_id(0)
    tm = TM_POST
    r = _mod_row(i, tm, n_ctx, lat_len)
    is_ctx = i * tm < n_ctx

    def mod_vec(k):
        return mod_ref[pl.ds(r, 1), k * D_MODEL:(k + 1) * D_MODEL]

    gate1, shift2, scale2 = mod_vec(2), mod_vec(3), mod_vec(4)

    y_f = jnp.where(is_ctx, yc_ref[0], yl_ref[0]).astype(F32)
    y_b = jnp.where(is_ctx, yc_ref[1], yl_ref[1]).astype(F32)
    y = (y_f + y_b) * _silu(z_ref[...].astype(F32))
    gw = D_INNER // SSD_GROUPS
    parts = []
    for g in range(SSD_GROUPS):
        yg = y[:, g * gw:(g + 1) * gw]
        ms = jnp.mean(yg * yg, axis=-1, keepdims=True)
        parts.append(yg * lax.rsqrt(ms + RMS_EPS))
    yn = (jnp.concatenate(parts, axis=-1) * normw_ref[...]).astype(BF16)
    y_ssd = _dot(yn, wssd_ref[...])

    row_len = jnp.where(i * tm < n_ctx, ctx_len, GRID_W)
    pos = _row_pos(tm, row_len)
    u = scc_ref[...].astype(F32) * scv_ref[...].astype(F32)
    half = SC_CONV // 2
    cv = u * scw_ref[half:half + 1, :]
    for k in range(SC_CONV):
        if k != half:
            cv = cv + _shift_rows(u, k - half, pos, row_len) * scw_ref[k:k + 1, :]
    y_sc = _dot((scb_ref[...].astype(F32) * cv).astype(BF16), wsc_ref[...])

    mixed = (jax.nn.sigmoid(gssd_ref[...].astype(F32)) * y_ssd
             + jax.nn.sigmoid(gsc_ref[...].astype(F32)) * y_sc)
    o = _dot(mixed.astype(BF16), wo_ref[...])

    x = jnp.where(is_ctx, xc_ref[...], xl_ref[...])
    res = DEEPNORM_ALPHA * x + gate1 * o
    mu = jnp.mean(res, axis=-1, keepdims=True)
    cen = res - mu
    var = jnp.mean(cen * cen, axis=-1, keepdims=True)
    x1 = cen * lax.rsqrt(var + LN_EPS) * g1_ref[...] + b1_ref[...]
    x1_ref[...] = x1
    h2 = x1 * (1.0 + scale2) + shift2
    h2_ref[...] = h2

    logits = _dot_x3(h2, wr_ref[...]) + br_ref[...]
    lane = lax.broadcasted_iota(jnp.int32, (tm, LANES), 1)
    lane_f = lane.astype(F32)
    neg = jnp.float32(-jnp.inf)
    work = jnp.where(lane < N_EXPERTS, logits, neg)
    vals, idxs, hits = [], [], []
    for _ in range(TOP_K):
        m = jnp.max(work, axis=-1, keepdims=True)
        idx = jnp.min(jnp.where(work == m, lane_f, float(LANES)), axis=-1, keepdims=True)
        hit = lane_f == idx
        vals.append(m)
        idxs.append(idx)
        hits.append(hit)
        work = jnp.where(hit, neg, work)
    es = [jnp.exp(v - vals[0]) for v in vals]
    denom = es[0] + es[1] + es[2] + es[3]

    @pl.when(i == 0)
    def _():
        cnt_scr[...] = jnp.zeros_like(cnt_scr)

    chosen = jnp.where(hits[0] | hits[1] | hits[2] | hits[3], 1.0, 0.0)
    rr = lax.broadcasted_iota(jnp.int32, (tm, tm), 0)
    cc = lax.broadcasted_iota(jnp.int32, (tm, tm), 1)
    before = _dot((rr > cc).astype(BF16), chosen.astype(BF16)) + cnt_scr[...]
    ranks = [jnp.sum(jnp.where(hits[k], before, 0.0), axis=-1, keepdims=True) for k in range(TOP_K)]
    cnt_scr[...] = cnt_scr[...] + jnp.sum(chosen, axis=0, keepdims=True)
    cnt_ref[...] = cnt_scr[...].astype(jnp.int32)

    ti = jnp.zeros((tm, LANES), F32)
    tp = jnp.zeros((tm, LANES), F32)
    for k in range(TOP_K):
        ti = jnp.where(lane == k, idxs[k], ti)
        ti = jnp.where(lane == TOP_K + k, ranks[k], ti)
        tp = jnp.where(lane == k, es[k] / denom, tp)
    ti_ref[...] = ti.astype(jnp.int32)
    tp_ref[...] = tp


def _post_call(x_ctx, x_lat, mod, proj, y_ctx, y_lat, norm_w, w_ssd, sc_w, w_sc, w_o, ln_g, ln_b,
               w_r, b_r, ctx_len, lat_len):
    n_ctx = x_ctx.shape[0]
    t = n_ctx + x_lat.shape[0]
    tm = TM_POST
    kern = functools.partial(_post_kernel, n_ctx=n_ctx, ctx_len=ctx_len, lat_len=lat_len)

    def colblk(off, width):
        return pl.BlockSpec((tm, width), lambda i: (i, off // width))

    def whole(a):
        return pl.BlockSpec(a.shape, lambda i: (0,) * a.ndim)

    return pl.pallas_call(
        kern,
        out_shape=(jax.ShapeDtypeStruct((t, D_MODEL), F32),
                   jax.ShapeDtypeStruct((t, D_MODEL), F32),
                   jax.ShapeDtypeStruct((t, LANES), jnp.int32),
                   jax.ShapeDtypeStruct((t, LANES), F32),
                   jax.ShapeDtypeStruct((1, LANES), jnp.int32)),
        grid=(t // tm,),
        in_specs=[
            *_ctx_lat_specs((tm, D_MODEL), n_ctx // tm),
            whole(mod),
            colblk(COL_Z, D_INNER),
            colblk(COL_SCB, D_MODEL), colblk(COL_SCC, D_MODEL), colblk(COL_SCV, D_MODEL),
            colblk(COL_GSSD, D_MODEL), colblk(COL_GSC, D_MODEL),
            *_ctx_lat_specs((2, tm, D_INNER), n_ctx // tm, lead=(0,)),
            whole(norm_w), whole(w_ssd), whole(sc_w), whole(w_sc), whole(w_o),
            whole(ln_g), whole(ln_b), whole(w_r), whole(b_r),
        ],
        out_specs=(pl.BlockSpec((tm, D_MODEL), lambda i: (i, 0)),
                   pl.BlockSpec((tm, D_MODEL), lambda i: (i, 0)),
                   pl.BlockSpec((tm, LANES), lambda i: (i, 0)),
                   pl.BlockSpec((tm, LANES), lambda i: (i, 0)),
                   pl.BlockSpec((1, LANES), lambda i: (0, 0))),
        scratch_shapes=[pltpu.VMEM((1, LANES), F32)],
        compiler_params=_cparams(("arbitrary",)),
        name="post",
    )(x_ctx, x_lat, mod, proj, proj, proj, proj, proj, proj, y_ctx, y_lat,
      norm_w, w_ssd, sc_w, w_sc, w_o, ln_g, ln_b, w_r, b_r)


def _dispatch_kernel(pos_ref, h_ref, o_hbm, sem):
    tb = TB_MOVE

    def issue(t, carry):
        for k in range(TOP_K):
            dst = pos_ref[0, 0, t * TOP_K + k]
            pltpu.make_async_copy(h_ref.at[pl.ds(t, 1)], o_hbm.at[pl.ds(dst, 1)], sem).start()
        return carry

    lax.fori_loop(0, tb, issue, 0)
    for _ in range(TOP_K):
        pltpu.make_async_copy(h_ref, o_hbm.at[pl.ds(0, tb)], sem).wait()


def _dispatch_call(pos_blocks, h2):
    t = h2.shape[0]
    tb = TB_MOVE
    return pl.pallas_call(
        _dispatch_kernel,
        out_shape=jax.ShapeDtypeStruct((t * TOP_K, D_MODEL), F32),
        grid=(t // tb,),
        in_specs=[
            pl.BlockSpec((1, 1, tb * TOP_K), lambda i: (i, 0, 0), memory_space=pltpu.SMEM),
            pl.BlockSpec((tb, D_MODEL), lambda i: (i, 0)),
        ],
        out_specs=pl.BlockSpec(memory_space=pl.ANY),
        scratch_shapes=[pltpu.SemaphoreType.DMA(())],
        compiler_params=pltpu.CompilerParams(dimension_semantics=("arbitrary",),
                                             vmem_limit_bytes=VMEM_LIMIT,
                                             has_side_effects=True),
        name="dispatch",
    )(pos_blocks, h2)


def _ffn_kernel(tile_ref, exp_ref, lo_ref, hi_ref, x_ref, wgu_ref, bgu_ref, wd_ref, bd_ref,
                o_ref, wgu_scr, wd_scr):
    s = pl.program_id(0)
    prev = jnp.maximum(s - 1, 0)
    new_tile = (s == 0) | (tile_ref[s] != tile_ref[prev])
    new_expert = (s == 0) | (exp_ref[s] != exp_ref[prev])
    lo = lo_ref[s]
    hi = hi_ref[s]

    @pl.when(new_tile)
    def _():
        o_ref[...] = jnp.zeros_like(o_ref)

    @pl.when(hi > lo)
    def _():
        @pl.when(new_expert)
        def _():
            wgu_scr[...] = wgu_ref[0].astype(BF16)
            wd_scr[...] = wd_ref[0].astype(BF16)

        gu = _dot(x_ref[...].astype(BF16), wgu_scr[...]) + bgu_ref[0]
        g = jnp.minimum(gu[:, :D_EXPERT], SWIGLU_LIMIT)
        u = jnp.clip(gu[:, D_EXPERT:], -SWIGLU_LIMIT, SWIGLU_LIMIT)
        act = (u + 1.0) * g * jax.nn.sigmoid(SWIGLU_ALPHA * g)
        y = _dot(act.astype(BF16), wd_scr[...]) + bd_ref[0]
        rowi = lax.broadcasted_iota(jnp.int32, (TM_FFN, 1), 0)
        mine = (rowi >= lo) & (rowi < hi)
        o_ref[...] = jnp.where(mine, y, o_ref[...])


def _ffn_call(tile_id, exp_id, lo, hi, x_sorted, w_gu, b_gu, w_d, b_d):
    n_rows = x_sorted.shape[0]
    n_steps = tile_id.shape[0]
    tm = TM_FFN
    grid_spec = pltpu.PrefetchScalarGridSpec(
        num_scalar_prefetch=4,
        grid=(n_steps,),
        in_specs=[
            pl.BlockSpec((tm, D_MODEL), lambda s, ti, ei, lo_, hi_: (ti[s], 0)),
            pl.BlockSpec((1, D_MODEL, 2 * D_EXPERT), lambda s, ti, ei, lo_, hi_: (ei[s], 0, 0)),
            pl.BlockSpec((1, 1, 2 * D_EXPERT), lambda s, ti, ei, lo_, hi_: (ei[s], 0, 0)),
            pl.BlockSpec((1, D_EXPERT, D_MODEL), lambda s, ti, ei, lo_, hi_: (ei[s], 0, 0)),
            pl.BlockSpec((1, 1, D_MODEL), lambda s, ti, ei, lo_, hi_: (ei[s], 0, 0)),
        ],
        out_specs=pl.BlockSpec((tm, D_MODEL), lambda s, ti, ei, lo_, hi_: (ti[s], 0)),
        scratch_shapes=[pltpu.VMEM((D_MODEL, 2 * D_EXPERT), BF16),
                        pltpu.VMEM((D_EXPERT, D_MODEL), BF16)],
    )
    return pl.pallas_call(
        _ffn_kernel,
        out_shape=jax.ShapeDtypeStruct((n_rows, D_MODEL), F32),
        grid_spec=grid_spec,
        compiler_params=_cparams(("arbitrary",)),
        name="ffn",
    )(tile_id, exp_id, lo, hi, x_sorted, w_gu, b_gu, w_d, b_d)


def _combine_kernel(pos_ref, x1_ref, tp_ref, mod_ref, g2_ref, b2_ref, y_hbm, oc_ref, ol_ref,
                    rows_scr, sem, *, n_ctx, lat_len):
    i = pl.program_id(0)
    tb = TB_MOVE

    def issue(t, carry):
        for k in range(TOP_K):
            src = pos_ref[0, 0, t * TOP_K + k]
            pltpu.make_async_copy(y_hbm.at[pl.ds(src, 1)], rows_scr.at[k, pl.ds(t, 1)], sem).start()
        return carry

    lax.fori_loop(0, tb, issue, 0)
    for k in range(TOP_K):
        pltpu.make_async_copy(y_hbm.at[pl.ds(0, tb)], rows_scr.at[k], sem).wait()

    r = _mod_row(i, tb, n_ctx, lat_len)
    gate2 = mod_ref[pl.ds(r, 1), 5 * D_MODEL:6 * D_MODEL]
    tp = tp_ref[...]
    moe = tp[:, 0:1] * rows_scr[0]
    for k in range(1, TOP_K):
        moe = moe + tp[:, k:k + 1] * rows_scr[k]
    res = DEEPNORM_ALPHA * x1_ref[...] + gate2 * moe
    mu = jnp.mean(res, axis=-1, keepdims=True)
    cen = res - mu
    var = jnp.mean(cen * cen, axis=-1, keepdims=True)
    out = cen * lax.rsqrt(var + LN_EPS) * g2_ref[...] + b2_ref[...]

    @pl.when(i * tb < n_ctx)
    def _():
        oc_ref[...] = out

    @pl.when(i * tb >= n_ctx)
    def _():
        ol_ref[...] = out


def _combine_call(pos_blocks, x1, tp, mod, ln_g, ln_b, y_sorted, n_ctx, lat_len):
    t = x1.shape[0]
    tb = TB_MOVE
    kern = functools.partial(_combine_kernel, n_ctx=n_ctx, lat_len=lat_len)
    return pl.pallas_call(
        kern,
        out_shape=(jax.ShapeDtypeStruct((n_ctx, D_MODEL), F32),
                   jax.ShapeDtypeStruct((t - n_ctx, D_MODEL), F32)),
        grid=(t // tb,),
        in_specs=[
            pl.BlockSpec((1, 1, tb * TOP_K), lambda i: (i, 0, 0), memory_space=pltpu.SMEM),
            pl.BlockSpec((tb, D_MODEL), lambda i: (i, 0)),
            pl.BlockSpec((tb, LANES), lambda i: (i, 0)),
            pl.BlockSpec(mod.shape, lambda i: (0, 0)),
            pl.BlockSpec((1, D_MODEL), lambda i: (0, 0)),
            pl.BlockSpec((1, D_MODEL), lambda i: (0, 0)),
            pl.BlockSpec(memory_space=pl.ANY),
        ],
        out_specs=_ctx_lat_specs((tb, D_MODEL), n_ctx // tb),
        scratch_shapes=[pltpu.VMEM((TOP_K, tb, D_MODEL), F32), pltpu.SemaphoreType.DMA(())],
        compiler_params=_cparams(("arbitrary",)),
        name="combine",
    )(pos_blocks, x1, tp, mod, ln_g, ln_b, y_sorted)


def _routing_tables(top_i, rank, counts, n_tokens):
    experts = jnp.arange(N_EXPERTS, dtype=jnp.int32)
    incl = jnp.cumsum(counts)
    offs = incl - counts
    pos = rank + jnp.sum(jnp.where(top_i[:, :, None] == experts, offs, 0), axis=-1)

    n_rows = n_tokens * TOP_K
    n_tiles = n_rows // TM_FFN
    n_steps = n_tiles + N_EXPERTS - 1
    first = offs // TM_FFN
    last = jnp.where(counts > 0, (incl - 1) // TM_FFN, first - 1)
    items = last - first + 1
    item_end = jnp.cumsum(items)
    item_start = item_end - items
    step = jnp.arange(n_steps, dtype=jnp.int32)
    total = item_end[-1]
    live = step < total
    e_of = jnp.sum(item_end[None, :] <= jnp.minimum(step, total - 1)[:, None], axis=1)
    e_of = jnp.minimum(e_of, N_EXPERTS - 1).astype(jnp.int32)
    sel = e_of[:, None] == experts

    def pick(v):
        return jnp.sum(jnp.where(sel, v[None, :], 0), axis=1)

    tile = jnp.where(live, pick(first) + step - pick(item_start), n_tiles - 1).astype(jnp.int32)
    lo = jnp.clip(pick(offs) - tile * TM_FFN, 0, TM_FFN)
    hi = jnp.clip(pick(incl) - tile * TM_FFN, 0, TM_FFN)
    lo = jnp.where(live, lo, 0).astype(jnp.int32)
    hi = jnp.where(live, hi, 0).astype(jnp.int32)
    return pos.astype(jnp.int32), tile, e_of, lo, hi


def kernel(x_prompt, x_sample, c, state_ssd_fwd, state_ssd_bwd, c_ctx, w_ada, b_ada, w_in,
           ssd_conv_w, ssd_conv_b, ssd_a_log, ssd_dt_bias, ssd_d, ssd_norm_w, ssd_w_out,
           sc_conv_w, sc_w_out, w_o, ln1_g, ln1_b, w_router, b_router, w_gate_up, b_gate_up,
           w_down, b_down, ln2_g, ln2_b):
    n_ctx_req, ctx_len, _ = x_prompt.shape
    n_lat_req, lat_len, _ = x_sample.shape
    n_ctx = n_ctx_req * ctx_len
    n_lat = n_lat_req * lat_len
    t = n_ctx + n_lat
    assert w_ada.shape[0] == 1, "single trunk layer"
    assert ctx_len % TS_SSD == 0 and lat_len % TS_SSD == 0 and TS_SSD % GRID_W == 0
    assert n_ctx % TM_PROJ == 0 and lat_len % TM_PROJ == 0 and TM_POST % ctx_len == 0
    assert TB_CONV % ctx_len == 0 and ctx_len & (ctx_len - 1) == 0

    x_ctx = x_prompt.reshape(n_ctx, D_MODEL)
    x_lat = x_sample.reshape(n_lat, D_MODEL)

    cvec = jnp.concatenate([c_ctx[None, :], c, jnp.zeros((7 - n_lat_req, D_MODEL), F32)], 0)
    mod = _mod_call(cvec, w_ada[0], b_ada[0])

    w = w_in[0]
    o_dt = D_INNER + XBC_DIM
    w_main = jnp.concatenate([w[:, :o_dt], w[:, o_dt + SSD_HEADS:]], axis=1).astype(BF16)
    w_dt = jnp.pad(w[:, o_dt:o_dt + SSD_HEADS], ((0, 0), (0, LANES - SSD_HEADS))).astype(BF16)
    proj, dt_raw = _inproj_call(x_ctx, x_lat, mod, w_main, w_dt, lat_len)

    xbc = _conv_call(proj, ssd_conv_w[0], ssd_conv_b[0], n_ctx, ctx_len)

    pad_h = ((0, 0), (0, LANES - SSD_HEADS))
    a_log = jnp.pad(ssd_a_log[0], pad_h).reshape(2, 1, LANES)
    dt_bias = jnp.pad(ssd_dt_bias[0], pad_h).reshape(2, 1, LANES)
    dvec = jnp.repeat(ssd_d[0], SSD_HEADDIM).reshape(1, D_INNER)
    h0_lat = jnp.stack([state_ssd_fwd[:, 0].reshape(n_lat_req, D_INNER, SSD_STATE),
                        state_ssd_bwd[:, 0].reshape(n_lat_req, D_INNER, SSD_STATE)], axis=1)
    y_ctx, h_ctx = _ssd_call(xbc, dt_raw, None, a_log, dt_bias, dvec, 0, n_ctx_req, ctx_len, True)
    (y_lat,) = _ssd_call(xbc, dt_raw, h0_lat, a_log, dt_bias, dvec, n_ctx, n_lat_req, lat_len, False)

    w_r = jnp.pad(w_router[0], ((0, 0), (0, LANES - N_EXPERTS)))
    b_r = jnp.pad(b_router[0], (0, LANES - N_EXPERTS)).reshape(1, LANES)
    x1, h2, route, top_p, counts = _post_call(
        x_ctx, x_lat, mod, proj, y_ctx, y_lat, ssd_norm_w[0].reshape(1, D_INNER),
        ssd_w_out[0].astype(BF16), sc_conv_w[0], sc_w_out[0].astype(BF16), w_o[0].astype(BF16),
        ln1_g[0].reshape(1, D_MODEL), ln1_b[0].reshape(1, D_MODEL), w_r, b_r, ctx_len, lat_len)

    pos, tile_id, exp_id, lo, hi = _routing_tables(
        route[:, :TOP_K], route[:, TOP_K:2 * TOP_K], counts[0, :N_EXPERTS], t)
    pos_blocks = pos.reshape(t // TB_MOVE, 1, TB_MOVE * TOP_K)
    x_sorted = _dispatch_call(pos_blocks, h2)
    y_sorted = _ffn_call(tile_id, exp_id, lo, hi, x_sorted,
                         w_gate_up[0], b_gate_up[0].reshape(N_EXPERTS, 1, 2 * D_EXPERT),
                         w_down[0], b_down[0].reshape(N_EXPERTS, 1, D_MODEL))
    out_ctx, out_lat = _combine_call(pos_blocks, x1, top_p, mod, ln2_g[0].reshape(1, D_MODEL),
                                     ln2_b[0].reshape(1, D_MODEL), y_sorted, n_ctx, lat_len)

    state_shape = (n_ctx_req, 1, SSD_HEADS, SSD_HEADDIM, SSD_STATE)
    return (out_ctx.reshape(n_ctx_req, ctx_len, D_MODEL), out_lat.reshape(n_lat_req, lat_len, D_MODEL),
            h_ctx[0].reshape(state_shape), h_ctx[1].reshape(state_shape))
```

```python
import functools

import jax
import jax.numpy as jnp
from jax import lax
from jax.experimental import pallas as pl
from jax.experimental.pallas import tpu as pltpu

F32 = jnp.float32
BF16 = jnp.bfloat16

D_MODEL = 1024
GRID_W = 64
D_INNER = 2048
SSD_HEADDIM = 64
SSD_HEADS = 32
SSD_GROUPS = 4
SSD_HPG = 8
SSD_STATE = 128
SSD_CONV = 5
SSD_CHUNK = 128
XBC_DIM = D_INNER + 2 * SSD_GROUPS * SSD_STATE
SC_CONV = 3
N_EXPERTS = 32
TOP_K = 4
D_EXPERT = 1024
SWIGLU_LIMIT = 7.0
SWIGLU_ALPHA = 1.702
LN_EPS = 1e-5
RMS_EPS = 1e-5
DEEPNORM_ALPHA = 2.0 ** 0.25
LOG2E = 1.4426950408889634

LANES = 128
MAIN_COLS = 10240
COL_Z = 0
COL_XBC = 2048
COL_SCB = 5120
COL_SCC = 6144
COL_SCV = 7168
COL_GSSD = 8192
COL_GSC = 9216

VMEM_LIMIT = 56 * 1024 * 1024

TM_PROJ = 1024
TN_PROJ = 2048
TB_CONV = 256
TS_SSD = 256
TM_POST = 256
TM_FFN = 256
RUN_ALIGN = 8
CHUNKS_PER_TILE = TM_FFN // RUN_ALIGN
LOCAL_ROWS = TM_POST * TOP_K + N_EXPERTS * RUN_ALIGN
LOCAL_CHUNKS = LOCAL_ROWS // RUN_ALIGN


def _cparams(sem):
    return pltpu.CompilerParams(dimension_semantics=sem, vmem_limit_bytes=VMEM_LIMIT)


def _split3(v):
    hi = v.astype(BF16)
    r1 = v - hi.astype(F32)
    mid = r1.astype(BF16)
    lo = (r1 - mid.astype(F32)).astype(BF16)
    return hi, mid, lo


def _dot(a, b):
    return jnp.dot(a, b, preferred_element_type=F32)


def _dot_exact_lhs(m_bf16, v_f32):
    hi, mid, lo = _split3(v_f32)
    return _dot(m_bf16, hi) + _dot(m_bf16, mid) + _dot(m_bf16, lo)


def _dot_x3(a_f32, b_f32):
    a_hi = a_f32.astype(BF16)
    a_lo = (a_f32 - a_hi.astype(F32)).astype(BF16)
    b_hi = b_f32.astype(BF16)
    b_lo = (b_f32 - b_hi.astype(F32)).astype(BF16)
    return _dot(a_hi, b_hi) + _dot(a_lo, b_hi) + _dot(a_hi, b_lo)


def _silu(v):
    return v * jax.nn.sigmoid(v)


def _softplus(v):
    return jnp.maximum(v, 0.0) + jnp.log(1.0 + jnp.exp(-jnp.abs(v)))


def _mod_kernel(c_ref, w_ref, b_ref, o_ref):
    o_ref[...] = _dot_x3(_silu(c_ref[...]), w_ref[...]) + b_ref[...]


def _mod_call(cvec, w_ada, b_ada):
    n = w_ada.shape[1]
    tn = 1536
    return pl.pallas_call(
        _mod_kernel,
        out_shape=jax.ShapeDtypeStruct((cvec.shape[0], n), F32),
        grid=(n // tn,),
        in_specs=[
            pl.BlockSpec(cvec.shape, lambda j: (0, 0)),
            pl.BlockSpec((D_MODEL, tn), lambda j: (0, j)),
            pl.BlockSpec((1, tn), lambda j: (0, j)),
        ],
        out_specs=pl.BlockSpec((cvec.shape[0], tn), lambda j: (0, j)),
        compiler_params=_cparams(("arbitrary",)),
        name="mod",
    )(cvec, w_ada, b_ada.reshape(1, n))


def _mod_row(block, rows_per_block, n_ctx_tokens, lat_len):
    tok = block * rows_per_block
    return jnp.where(tok < n_ctx_tokens, 0, 1 + (tok - n_ctx_tokens) // lat_len)


def _ctx_lat_specs(block_shape, n_ctx_blocks, lead=()):
    tail = (0,) * (len(block_shape) - len(lead) - 1)
    ctx = pl.BlockSpec(block_shape, lambda i, *_: lead + (jnp.minimum(i, n_ctx_blocks - 1),) + tail)
    lat = pl.BlockSpec(block_shape, lambda i, *_: lead + (jnp.maximum(i - n_ctx_blocks, 0),) + tail)
    return ctx, lat


def _inproj_kernel(xc_ref, xl_ref, mod_ref, w_ref, wdt_ref, o_ref, dt_ref, h_scr, *, n_ctx, lat_len):
    i = pl.program_id(0)
    j = pl.program_id(1)

    @pl.when(j == 0)
    def _():
        r = _mod_row(i, TM_PROJ, n_ctx, lat_len)
        shift = mod_ref[pl.ds(r, 1), 0:D_MODEL]
        scale = mod_ref[pl.ds(r, 1), D_MODEL:2 * D_MODEL]
        x = jnp.where(i * TM_PROJ < n_ctx, xc_ref[...], xl_ref[...])
        h = (x * (1.0 + scale) + shift).astype(BF16)
        h_scr[...] = h
        dt_ref[...] = _dot(h, wdt_ref[...])

    o_ref[...] = _dot(h_scr[...], w_ref[...]).astype(BF16)


def _inproj_call(x_ctx, x_lat, mod, w_main, w_dt, lat_len):
    n_ctx = x_ctx.shape[0]
    t = n_ctx + x_lat.shape[0]
    kern = functools.partial(_inproj_kernel, n_ctx=n_ctx, lat_len=lat_len)
    return pl.pallas_call(
        kern,
        out_shape=(jax.ShapeDtypeStruct((t, MAIN_COLS), BF16),
                   jax.ShapeDtypeStruct((t, LANES), F32)),
        grid=(t // TM_PROJ, MAIN_COLS // TN_PROJ),
        in_specs=[
            *_ctx_lat_specs((TM_PROJ, D_MODEL), n_ctx // TM_PROJ),
            pl.BlockSpec(mod.shape, lambda i, j: (0, 0)),
            pl.BlockSpec((D_MODEL, TN_PROJ), lambda i, j: (0, j)),
            pl.BlockSpec((D_MODEL, LANES), lambda i, j: (0, 0)),
        ],
        out_specs=(pl.BlockSpec((TM_PROJ, TN_PROJ), lambda i, j: (i, j)),
                   pl.BlockSpec((TM_PROJ, LANES), lambda i, j: (i, 0))),
        scratch_shapes=[pltpu.VMEM((TM_PROJ, D_MODEL), BF16)],
        compiler_params=_cparams(("arbitrary", "arbitrary")),
        name="inproj",
    )(x_ctx, x_lat, mod, w_main, w_dt)


def _shift_rows(x, off, pos, row_len):
    if off == 0:
        return x
    n = x.shape[0]
    rolled = pltpu.roll(x, (-off) % n, 0)
    ok = (pos + off >= 0) & (pos + off < row_len)
    return jnp.where(ok, rolled, 0.0)


def _row_pos(n, row_len):
    t = lax.broadcasted_iota(jnp.int32, (n, 1), 0)
    return jnp.bitwise_and(t, row_len - 1)


def _conv_kernel(x_ref, w_ref, b_ref, o_ref, *, n_ctx, ctx_len):
    i = pl.program_id(0)
    row_len = jnp.where(i * TB_CONV < n_ctx, ctx_len, GRID_W)
    pos = _row_pos(TB_CONV, row_len)
    x = x_ref[...].astype(F32)
    half = SSD_CONV // 2
    acc = x * w_ref[half:half + 1, :]
    for k in range(SSD_CONV):
        if k != half:
            acc = acc + _shift_rows(x, k - half, pos, row_len) * w_ref[k:k + 1, :]
    o_ref[...] = _silu(acc + b_ref[...]).astype(BF16)


def _conv_call(proj, conv_w, conv_b, n_ctx, ctx_len):
    t = proj.shape[0]
    tc = 1024
    kern = functools.partial(_conv_kernel, n_ctx=n_ctx, ctx_len=ctx_len)
    return pl.pallas_call(
        kern,
        out_shape=jax.ShapeDtypeStruct((t, XBC_DIM), BF16),
        grid=(t // TB_CONV, XBC_DIM // tc),
        in_specs=[
            pl.BlockSpec((TB_CONV, tc), lambda i, j: (i, COL_XBC // tc + j)),
            pl.BlockSpec((SSD_CONV, tc), lambda i, j: (0, j)),
            pl.BlockSpec((1, tc), lambda i, j: (0, j)),
        ],
        out_specs=pl.BlockSpec((TB_CONV, tc), lambda i, j: (i, j)),
        compiler_params=_cparams(("arbitrary", "arbitrary")),
        name="conv",
    )(proj, conv_w, conv_b.reshape(1, XBC_DIM))


def _ssd_kernel(*refs, has_h0, want_final):
    xs_ref, b_ref, c_ref, dt_ref = refs[:4]
    rest = list(refs[4:])
    h0_ref = rest.pop(0) if has_h0 else None
    alog_ref, bias_ref, dvec_ref, y_ref = rest[:4]
    hfin_ref = rest[4] if want_final else None
    ht_scr = rest[-1]
    d = pl.program_id(1)
    st = pl.program_id(2)
    q = SSD_CHUNK
    n_chunks = TS_SSD // q
    n_blk = D_INNER // LANES
    is_fwd = d == 0

    @pl.when(st == 0)
    def _():
        if has_h0:
            for j in range(n_blk):
                ht_scr[:, j * LANES:(j + 1) * LANES] = h0_ref[0, 0, j * LANES:(j + 1) * LANES, :].T
        else:
            ht_scr[...] = jnp.zeros_like(ht_scr)

    a2_neg = -jnp.exp(alog_ref[0]) * LOG2E
    bias = bias_ref[0]
    row = lax.broadcasted_iota(jnp.int32, (q, q), 0)
    col = lax.broadcasted_iota(jnp.int32, (q, q), 1)
    keep = (row - col) * jnp.where(is_fwd, 1, -1) >= 0
    tri = keep.astype(BF16)
    lane_lo = lax.broadcasted_iota(jnp.int32, (q, LANES), 1) < SSD_HEADDIM
    lane_lo1 = lane_lo[0:1, :]
    d_on = jnp.where(is_fwd, 1.0, 0.0)
    neg_inf = jnp.float32(-jnp.inf)

    def chunk_body(ci, carry):
        c = jnp.where(is_fwd, ci, n_chunks - 1 - ci)
        c0 = pl.multiple_of(c * q, q)
        dt = _softplus(dt_ref[pl.ds(c0, q), :] + bias)
        acs = _dot_exact_lhs(tri, dt * a2_neg)
        acs_t = acs.T
        dt_t = dt.T
        tot_row = jnp.where(is_fwd, acs[q - 1:q, :], acs[0:1, :])
        tot_col = jnp.where(is_fwd, acs_t[:, q - 1:q], acs_t[:, 0:1])
        e_tot = jnp.exp2(tot_row)
        src_t = acs_t - jnp.log2(dt_t)
        dte_t = jnp.exp2(tot_col - acs_t) * dt_t

        for g in range(SSD_GROUPS):
            bg = b_ref[pl.ds(c0, q), g * SSD_STATE:(g + 1) * SSD_STATE]
            cg = c_ref[pl.ds(c0, q), g * SSD_STATE:(g + 1) * SSD_STATE]
            cb = lax.dot_general(cg, bg, (((1,), (1,)), ((), ())),
                                 preferred_element_type=F32)
            cgf = cg.astype(F32)
            bg_t = bg.astype(F32).T
            for p in range(SSD_HPG // 2):
                h0 = g * SSD_HPG + 2 * p
                lo = h0 * SSD_HEADDIM
                xp = xs_ref[pl.ds(c0, q), lo:lo + LANES]
                htp = ht_scr[:, lo:lo + LANES]
                rhs = jnp.concatenate([xp, htp.astype(BF16)], axis=0)
                ys, sts = [], []
                for hh in (h0, h0 + 1):
                    tgt = jnp.broadcast_to(acs[:, hh:hh + 1], (q, q))
                    seg = jnp.where(keep, tgt - src_t[hh:hh + 1, :], neg_inf)
                    w = (cb * jnp.exp2(seg)).astype(BF16)
                    ce = (cgf * jnp.exp2(tgt)).astype(BF16)
                    ys.append(_dot(jnp.concatenate([w, ce], axis=1), rhs))
                    sts.append(_dot((bg_t * dte_t[hh:hh + 1, :]).astype(BF16), xp))
                y = (jnp.where(lane_lo, ys[0], ys[1])
                     + (d_on * dvec_ref[:, lo:lo + LANES]) * xp.astype(F32))
                y_ref[0, pl.ds(c0, q), lo:lo + LANES] = y.astype(BF16)
                dec = jnp.where(lane_lo1, e_tot[:, h0:h0 + 1], e_tot[:, h0 + 1:h0 + 2])
                ht_scr[:, lo:lo + LANES] = htp * dec + jnp.where(lane_lo, sts[0], sts[1])
        return carry

    lax.fori_loop(0, n_chunks, chunk_body, 0)

    if want_final:
        @pl.when(st == pl.num_programs(2) - 1)
        def _():
            for j in range(n_blk):
                hfin_ref[0, 0, j * LANES:(j + 1) * LANES, :] = ht_scr[:, j * LANES:(j + 1) * LANES].T


def _ssd_call(xbc, dt_raw, h0, a_log, dt_bias, dvec, tok0, n_seq, seq_len, want_final):
    n_steps = seq_len // TS_SSD
    blk0 = tok0 // TS_SSD

    def tok_blk(b, d, s):
        return blk0 + b * n_steps + jnp.where(d == 0, s, n_steps - 1 - s)

    state_spec = pl.BlockSpec((1, 1, D_INNER, SSD_STATE), lambda b, d, s: (b, d, 0, 0))
    in_specs = [
        pl.BlockSpec((TS_SSD, D_INNER), lambda b, d, s: (tok_blk(b, d, s), 0)),
        pl.BlockSpec((TS_SSD, 512), lambda b, d, s: (tok_blk(b, d, s), 4)),
        pl.BlockSpec((TS_SSD, 512), lambda b, d, s: (tok_blk(b, d, s), 5)),
        pl.BlockSpec((TS_SSD, LANES), lambda b, d, s: (tok_blk(b, d, s), 0)),
    ]
    args = [xbc, xbc, xbc, dt_raw]
    if h0 is not None:
        in_specs.append(state_spec)
        args.append(h0)
    in_specs += [
        pl.BlockSpec((1, 1, LANES), lambda b, d, s: (d, 0, 0)),
        pl.BlockSpec((1, 1, LANES), lambda b, d, s: (d, 0, 0)),
        pl.BlockSpec((1, D_INNER), lambda b, d, s: (0, 0)),
    ]
    args += [a_log, dt_bias, dvec]
    out_shape = [jax.ShapeDtypeStruct((2, n_seq * seq_len, D_INNER), BF16)]
    out_specs = [pl.BlockSpec((1, TS_SSD, D_INNER),
                              lambda b, d, s: (d, tok_blk(b, d, s) - blk0, 0))]
    if want_final:
        out_shape.append(jax.ShapeDtypeStruct((2, n_seq, D_INNER, SSD_STATE), F32))
        out_specs.append(pl.BlockSpec((1, 1, D_INNER, SSD_STATE), lambda b, d, s: (d, b, 0, 0)))
    kern = functools.partial(_ssd_kernel, has_h0=h0 is not None, want_final=want_final)
    return pl.pallas_call(
        kern,
        out_shape=tuple(out_shape),
        grid=(n_seq, 2, n_steps),
        in_specs=in_specs,
        out_specs=tuple(out_specs),
        scratch_shapes=[pltpu.VMEM((SSD_STATE, D_INNER), F32)],
        compiler_params=_cparams(("arbitrary", "arbitrary", "arbitrary")),
        name="ssd_final" if want_final else "ssd",
    )(*args)


def _post_kernel(xc_ref, xl_ref, mod_ref, z_ref, scb_ref, scc_ref, scv_ref, gssd_ref, gsc_ref,
                 yc_ref, yl_ref, normw_ref, wssd_ref, scw_ref, wsc_ref, wo_ref, g1_ref, b1_ref,
                 wr_ref, br_ref, x1_ref, h2l_ref, ti_ref, tp_ref, cnt_ref,
                 *, n_ctx, ctx_len, lat_len):
    i = pl.program_id(0)
    tm = TM_POST
    r = _mod_row(i, tm, n_ctx, lat_len)
    is_ctx = i * tm < n_ctx

    def mod_vec(k):
        return mod_ref[pl.ds(r, 1), k * D_MODEL:(k + 1) * D_MODEL]

    gate1, shift2, scale2 = mod_vec(2), mod_vec(3), mod_vec(4)

    y_f = jnp.where(is_ctx, yc_ref[0], yl_ref[0]).astype(F32)
    y_b = jnp.where(is_ctx, yc_ref[1], yl_ref[1]).astype(F32)
    y = (y_f + y_b) * _silu(z_ref[...].astype(F32))
    gw = D_INNER // SSD_GROUPS
    parts = []
    for g in range(SSD_GROUPS):
        yg = y[:, g * gw:(g + 1) * gw]
        ms = jnp.mean(yg * yg, axis=-1, keepdims=True)
        parts.append(yg * lax.rsqrt(ms + RMS_EPS))
    yn = (jnp.concatenate(parts, axis=-1) * normw_ref[...]).astype(BF16)
    y_ssd = _dot(yn, wssd_ref[...])

    row_len = jnp.where(i * tm < n_ctx, ctx_len, GRID_W)
    pos = _row_pos(tm, row_len)
    u = scc_ref[...].astype(F32) * scv_ref[...].astype(F32)
    half = SC_CONV // 2
    cv = u * scw_ref[half:half + 1, :]
    for k in range(SC_CONV):
        if k != half:
            cv = cv + _shift_rows(u, k - half, pos, row_len) * scw_ref[k:k + 1, :]
    y_sc = _dot((scb_ref[...].astype(F32) * cv).astype(BF16), wsc_ref[...])

    mixed = (jax.nn.sigmoid(gssd_ref[...].astype(F32)) * y_ssd
             + jax.nn.sigmoid(gsc_ref[...].astype(F32)) * y_sc)
    o = _dot(mixed.astype(BF16), wo_ref[...])

    x = jnp.where(is_ctx, xc_ref[...], xl_ref[...])
    res = DEEPNORM_ALPHA * x + gate1 * o
    mu = jnp.mean(res, axis=-1, keepdims=True)
    cen = res - mu
    var = jnp.mean(cen * cen, axis=-1, keepdims=True)
    x1 = cen * lax.rsqrt(var + LN_EPS) * g1_ref[...] + b1_ref[...]
    x1_ref[...] = x1
    h2 = x1 * (1.0 + scale2) + shift2

    logits = _dot_x3(h2, wr_ref[...]) + br_ref[...]
    lane = lax.broadcasted_iota(jnp.int32, (tm, LANES), 1)
    lane_f = lane.astype(F32)
    neg = jnp.float32(-jnp.inf)
    work = jnp.where(lane < N_EXPERTS, logits, neg)
    vals, idxs, hits = [], [], []
    for _ in range(TOP_K):
        m = jnp.max(work, axis=-1, keepdims=True)
        idx = jnp.min(jnp.where(work == m, lane_f, float(LANES)), axis=-1, keepdims=True)
        hit = lane_f == idx
        vals.append(m)
        idxs.append(idx)
        hits.append(hit)
        work = jnp.where(hit, neg, work)
    es = [jnp.exp(v - vals[0]) for v in vals]
    denom = es[0] + es[1] + es[2] + es[3]

    chosen = jnp.where(hits[0] | hits[1] | hits[2] | hits[3], 1.0, 0.0)
    cnt = jnp.sum(chosen, axis=0, keepdims=True)
    cnt_ref[0] = cnt.astype(jnp.int32)
    cpad = jnp.floor((cnt + (RUN_ALIGN - 1)) * (1.0 / RUN_ALIGN)) * RUN_ALIGN
    er = lax.broadcasted_iota(jnp.int32, (LANES, LANES), 0)
    ec = lax.broadcasted_iota(jnp.int32, (LANES, LANES), 1)
    run_off = _dot(jnp.broadcast_to(cpad, (8, LANES)).astype(BF16), (er < ec).astype(BF16))[0:1]
    rr = lax.broadcasted_iota(jnp.int32, (tm, tm), 0)
    cc = lax.broadcasted_iota(jnp.int32, (tm, tm), 1)
    slot = _dot((rr > cc).astype(BF16), chosen.astype(BF16)) + run_off
    lrows = [jnp.sum(jnp.where(hits[k], slot, 0.0), axis=-1, keepdims=True) for k in range(TOP_K)]

    ti = jnp.zeros((tm, LANES), F32)
    tp = jnp.zeros((tm, LANES), F32)
    lmat = jnp.full((tm, LANES), -1.0, F32)
    for k in range(TOP_K):
        ti = jnp.where(lane == k, idxs[k], ti)
        ti = jnp.where(lane == TOP_K + k, lrows[k], ti)
        lmat = jnp.where(lane == k, lrows[k], lmat)
        tp = jnp.where(lane == k, es[k] / denom, tp)
    ti_ref[...] = ti.astype(jnp.int32)
    tp_ref[...] = tp

    lrow_t = jnp.concatenate([lmat[j * LANES:(j + 1) * LANES, :].T for j in range(tm // LANES)],
                             axis=1)
    jrow = lax.broadcasted_iota(jnp.int32, (LOCAL_ROWS, tm), 0).astype(F32)
    sel_t = jnp.where(jrow == lrow_t[0:1, :], 1.0, 0.0)
    for k in range(1, TOP_K):
        sel_t = sel_t + jnp.where(jrow == lrow_t[k:k + 1, :], 1.0, 0.0)
    h2l_ref[...] = _dot(sel_t.astype(BF16), h2.astype(BF16))


def _post_call(x_ctx, x_lat, mod, proj, y_ctx, y_lat, norm_w, w_ssd, sc_w, w_sc, w_o, ln_g, ln_b,
               w_r, b_r, ctx_len, lat_len):
    n_ctx = x_ctx.shape[0]
    t = n_ctx + x_lat.shape[0]
    tm = TM_POST
    kern = functools.partial(_post_kernel, n_ctx=n_ctx, ctx_len=ctx_len, lat_len=lat_len)

    def colblk(off, width):
        return pl.BlockSpec((tm, width), lambda i: (i, off // width))

    def whole(a):
        return pl.BlockSpec(a.shape, lambda i: (0,) * a.ndim, pipeline_mode=pl.Buffered(1))

    return pl.pallas_call(
        kern,
        out_shape=(jax.ShapeDtypeStruct((t, D_MODEL), F32),
                   jax.ShapeDtypeStruct((t // tm * LOCAL_ROWS, D_MODEL), F32),
                   jax.ShapeDtypeStruct((t, LANES), jnp.int32),
                   jax.ShapeDtypeStruct((t, LANES), F32),
                   jax.ShapeDtypeStruct((t // tm, 1, LANES), jnp.int32)),
        grid=(t // tm,),
        in_specs=[
            *_ctx_lat_specs((tm, D_MODEL), n_ctx // tm),
            whole(mod),
            colblk(COL_Z, D_INNER),
            colblk(COL_SCB, D_MODEL), colblk(COL_SCC, D_MODEL), colblk(COL_SCV, D_MODEL),
            colblk(COL_GSSD, D_MODEL), colblk(COL_GSC, D_MODEL),
            *_ctx_lat_specs((2, tm, D_INNER), n_ctx // tm, lead=(0,)),
            whole(norm_w), whole(w_ssd), whole(sc_w), whole(w_sc), whole(w_o),
            whole(ln_g), whole(ln_b), whole(w_r), whole(b_r),
        ],
        out_specs=(pl.BlockSpec((tm, D_MODEL), lambda i: (i, 0)),
                   pl.BlockSpec((LOCAL_ROWS, D_MODEL), lambda i: (i, 0)),
                   pl.BlockSpec((tm, LANES), lambda i: (i, 0)),
                   pl.BlockSpec((tm, LANES), lambda i: (i, 0)),
                   pl.BlockSpec((1, 1, LANES), lambda i: (i, 0, 0))),
        compiler_params=_cparams(("arbitrary",)),
        name="post",
    )(x_ctx, x_lat, mod, proj, proj, proj, proj, proj, proj, y_ctx, y_lat,
      norm_w, w_ssd, sc_w, w_sc, w_o, ln_g, ln_b, w_r, b_r)


F_FIRST, F_LAST, F_PAR, F_WAIT_OUT, F_FINAL, F_FINAL_OTHER, F_HAS_NEXT = (1 << b for b in range(7))


def _ffn_kernel(tile_ref, exp_ref, lo_ref, hi_ref, flag_ref,
                src_cur_ref, src_nxt_ref, dst_cur_ref,
                h2l_hbm, wgu_ref, bgu_ref, wd_ref, bd_ref,
                yl_hbm, wgu_scr, wd_scr, x_buf, y_buf, sem_x, sem_y):
    s = pl.program_id(0)
    flags = flag_ref[s]
    par = jnp.where((flags & F_PAR) != 0, 1, 0)
    lo = lo_ref[s]
    hi = hi_ref[s]
    new_expert = (s == 0) | (exp_ref[s] != exp_ref[jnp.maximum(s - 1, 0)])

    def on(bit):
        return (flags & bit) != 0

    def gather(tbl_ref, slot):
        for c in range(CHUNKS_PER_TILE):
            src = pl.multiple_of(tbl_ref[0, 0, c] * RUN_ALIGN, RUN_ALIGN)
            pltpu.make_async_copy(h2l_hbm.at[pl.ds(src, RUN_ALIGN)],
                                  x_buf.at[slot, pl.ds(c * RUN_ALIGN, RUN_ALIGN)],
                                  sem_x.at[slot]).start()

    def wait_out(slot):
        pltpu.make_async_copy(y_buf.at[slot], yl_hbm.at[pl.ds(0, TM_FFN)], sem_y.at[slot]).wait()

    @pl.when(s == 0)
    def _():
        gather(src_cur_ref, 0)

    @pl.when(on(F_FIRST))
    def _():
        pltpu.make_async_copy(h2l_hbm.at[pl.ds(0, TM_FFN)], x_buf.at[par], sem_x.at[par]).wait()

        @pl.when(on(F_WAIT_OUT))
        def _():
            wait_out(par)

        y_buf[par] = jnp.zeros((TM_FFN, D_MODEL), F32)

        @pl.when(on(F_HAS_NEXT))
        def _():
            gather(src_nxt_ref, 1 - par)

    @pl.when(hi > lo)
    def _():
        @pl.when(new_expert)
        def _():
            wgu_scr[...] = wgu_ref[0].astype(BF16)
            wd_scr[...] = wd_ref[0].astype(BF16)

        gu = _dot(x_buf[par].astype(BF16), wgu_scr[...]) + bgu_ref[0]
        g = jnp.minimum(gu[:, :D_EXPERT], SWIGLU_LIMIT)
        u = jnp.clip(gu[:, D_EXPERT:], -SWIGLU_LIMIT, SWIGLU_LIMIT)
        act = (u + 1.0) * g * jax.nn.sigmoid(SWIGLU_ALPHA * g)
        y = _dot(act.astype(BF16), wd_scr[...]) + bd_ref[0]
        rowi = lax.broadcasted_iota(jnp.int32, (TM_FFN, 1), 0)
        mine = (rowi >= lo) & (rowi < hi)
        y_buf[par] = jnp.where(mine, y, y_buf[par])

    @pl.when(on(F_LAST))
    def _():
        for c in range(CHUNKS_PER_TILE):
            dst = pl.multiple_of(dst_cur_ref[0, 0, c] * RUN_ALIGN, RUN_ALIGN)
            pltpu.make_async_copy(y_buf.at[par, pl.ds(c * RUN_ALIGN, RUN_ALIGN)],
                                  yl_hbm.at[pl.ds(dst, RUN_ALIGN)], sem_y.at[par]).start()

    @pl.when(on(F_FINAL))
    def _():
        wait_out(par)

        @pl.when(on(F_FINAL_OTHER))
        def _():
            wait_out(1 - par)


def _ffn_call(tables, h2l, w_gu, b_gu, w_d, b_d):
    tile_id, exp_id, lo, hi, flags, src8, dst8 = tables
    n_steps = tile_id.shape[0]
    n_tiles = src8.shape[0]

    def tbl_spec(index_fn):
        return pl.BlockSpec((1, 1, CHUNKS_PER_TILE), index_fn, memory_space=pltpu.SMEM)

    def per_expert(shape):
        return pl.BlockSpec((1,) + shape, lambda s, ti, ei, *_: (ei[s], 0, 0))

    grid_spec = pltpu.PrefetchScalarGridSpec(
        num_scalar_prefetch=5,
        grid=(n_steps,),
        in_specs=[
            tbl_spec(lambda s, ti, *_: (ti[s], 0, 0)),
            tbl_spec(lambda s, ti, *_: (jnp.minimum(ti[s] + 1, n_tiles - 1), 0, 0)),
            tbl_spec(lambda s, ti, *_: (ti[s], 0, 0)),
            pl.BlockSpec(memory_space=pl.ANY),
            per_expert((D_MODEL, 2 * D_EXPERT)), per_expert((1, 2 * D_EXPERT)),
            per_expert((D_EXPERT, D_MODEL)), per_expert((1, D_MODEL)),
        ],
        out_specs=pl.BlockSpec(memory_space=pl.ANY),
        scratch_shapes=[pltpu.VMEM((D_MODEL, 2 * D_EXPERT), BF16),
                        pltpu.VMEM((D_EXPERT, D_MODEL), BF16),
                        pltpu.VMEM((2, TM_FFN, D_MODEL), F32),
                        pltpu.VMEM((2, TM_FFN, D_MODEL), F32),
                        pltpu.SemaphoreType.DMA((2,)),
                        pltpu.SemaphoreType.DMA((2,))],
    )
    tables_and_chunks = (tile_id, exp_id, lo, hi, flags, src8, src8, dst8)
    return pl.pallas_call(
        _ffn_kernel,
        out_shape=jax.ShapeDtypeStruct(h2l.shape, F32),
        grid_spec=grid_spec,
        input_output_aliases={len(tables_and_chunks): 0},
        compiler_params=pltpu.CompilerParams(dimension_semantics=("arbitrary",),
                                             vmem_limit_bytes=VMEM_LIMIT,
                                             has_side_effects=True),
        name="ffn",
    )(*tables_and_chunks, h2l, w_gu, b_gu, w_d, b_d)


def _combine_kernel(x1_ref, ti_ref, tp_ref, cnt_ref, mod_ref, g2_ref, b2_ref, yl_ref,
                    oc_ref, ol_ref, *, n_ctx, lat_len):
    i = pl.program_id(0)
    tb = TM_POST

    jl = lax.broadcasted_iota(jnp.int32, (tb, LOCAL_ROWS), 1)
    ti = ti_ref[...]
    tp = tp_ref[...]
    sel = jnp.where(jl == ti[:, TOP_K:TOP_K + 1], tp[:, 0:1], 0.0)
    for k in range(1, TOP_K):
        sel = sel + jnp.where(jl == ti[:, TOP_K + k:TOP_K + k + 1], tp[:, k:k + 1], 0.0)
    cnt = cnt_ref[0].astype(F32)
    cpad = jnp.floor((cnt + (RUN_ALIGN - 1)) * (1.0 / RUN_ALIGN)) * RUN_ALIGN
    used = jnp.sum(cpad, axis=-1, keepdims=True)
    rowi = lax.broadcasted_iota(jnp.int32, (LOCAL_ROWS, 1), 0).astype(F32)
    yl = jnp.where(rowi < used, yl_ref[...], 0.0)
    moe = _dot(sel.astype(BF16), yl.astype(BF16))

    r = _mod_row(i, tb, n_ctx, lat_len)
    gate2 = mod_ref[pl.ds(r, 1), 5 * D_MODEL:6 * D_MODEL]
    res = DEEPNORM_ALPHA * x1_ref[...] + gate2 * moe
    mu = jnp.mean(res, axis=-1, keepdims=True)
    cen = res - mu
    var = jnp.mean(cen * cen, axis=-1, keepdims=True)
    out = cen * lax.rsqrt(var + LN_EPS) * g2_ref[...] + b2_ref[...]

    @pl.when(i * tb < n_ctx)
    def _():
        oc_ref[...] = out

    @pl.when(i * tb >= n_ctx)
    def _():
        ol_ref[...] = out


def _combine_call(x1, route, tp, cnt, mod, ln_g, ln_b, y_local, n_ctx, lat_len):
    t = x1.shape[0]
    tb = TM_POST
    kern = functools.partial(_combine_kernel, n_ctx=n_ctx, lat_len=lat_len)
    return pl.pallas_call(
        kern,
        out_shape=(jax.ShapeDtypeStruct((n_ctx, D_MODEL), F32),
                   jax.ShapeDtypeStruct((t - n_ctx, D_MODEL), F32)),
        grid=(t // tb,),
        in_specs=[
            pl.BlockSpec((tb, D_MODEL), lambda i: (i, 0)),
            pl.BlockSpec((tb, LANES), lambda i: (i, 0)),
            pl.BlockSpec((tb, LANES), lambda i: (i, 0)),
            pl.BlockSpec((1, 1, LANES), lambda i: (i, 0, 0)),
            pl.BlockSpec(mod.shape, lambda i: (0, 0)),
            pl.BlockSpec((1, D_MODEL), lambda i: (0, 0)),
            pl.BlockSpec((1, D_MODEL), lambda i: (0, 0)),
            pl.BlockSpec((LOCAL_ROWS, D_MODEL), lambda i: (i, 0)),
        ],
        out_specs=_ctx_lat_specs((tb, D_MODEL), n_ctx // tb),
        compiler_params=_cparams(("arbitrary",)),
        name="combine",
    )(x1, route, tp, cnt, mod, ln_g, ln_b, y_local)


def _routing_tables(cnt):
    i32 = jnp.int32
    n_blocks = cnt.shape[0]
    n_tiles = n_blocks * LOCAL_ROWS // TM_FFN
    n_steps = n_tiles + N_EXPERTS - 1
    experts = jnp.arange(N_EXPERTS, dtype=i32)

    cpad = (cnt + (RUN_ALIGN - 1)) // RUN_ALIGN * RUN_ALIGN
    run_loc = jnp.cumsum(cpad, axis=1) - cpad
    region = cpad.sum(axis=0)
    incl = jnp.cumsum(region)
    offs = incl - region
    total_rows = incl[-1]
    run_glob = offs[None, :] + jnp.cumsum(cpad, axis=0) - cpad

    g_row = jnp.arange(n_tiles * CHUNKS_PER_TILE, dtype=i32) * RUN_ALIGN
    rs = run_glob.reshape(-1)
    re = rs + cpad.reshape(-1)
    ls = (jnp.arange(n_blocks, dtype=i32)[:, None] * LOCAL_ROWS + run_loc).reshape(-1)
    inside = (g_row[:, None] >= rs[None, :]) & (g_row[:, None] < re[None, :])
    local_row = g_row + jnp.sum(jnp.where(inside, (ls - rs)[None, :], 0), axis=1)
    valid = g_row < total_rows
    extra = jnp.clip((g_row - total_rows) // RUN_ALIGN, 0, CHUNKS_PER_TILE - 1)
    spare = (extra // 3) * LOCAL_CHUNKS + (LOCAL_CHUNKS - 4) + extra % 3
    src8 = jnp.where(valid, local_row // RUN_ALIGN, LOCAL_CHUNKS - 1).astype(i32)
    dst8 = jnp.where(valid, local_row // RUN_ALIGN, spare).astype(i32)

    first = offs // TM_FFN
    last = jnp.where(region > 0, (incl - 1) // TM_FFN, first - 1)
    items = last - first + 1
    item_end = jnp.cumsum(items)
    item_start = item_end - items
    step = jnp.arange(n_steps, dtype=i32)
    total = item_end[-1]
    live = step < total
    e_of = jnp.sum(item_end[None, :] <= jnp.minimum(step, total - 1)[:, None], axis=1)
    e_of = jnp.minimum(e_of, N_EXPERTS - 1).astype(i32)
    sel = e_of[:, None] == experts

    def pick(v):
        return jnp.sum(jnp.where(sel, v[None, :], 0), axis=1)

    n_live_tiles = (total_rows + TM_FFN - 1) // TM_FFN
    tile = jnp.where(live, pick(first) + step - pick(item_start), n_live_tiles - 1).astype(i32)
    lo = jnp.clip(pick(offs) - tile * TM_FFN, 0, TM_FFN)
    hi = jnp.clip(pick(incl) - tile * TM_FFN, 0, TM_FFN)
    lo = jnp.where(live, lo, 0).astype(i32)
    hi = jnp.where(live, hi, 0).astype(i32)

    prev_tile = jnp.concatenate([jnp.full((1,), -1, i32), tile[:-1]])
    next_tile = jnp.concatenate([tile[1:], jnp.full((1,), -1, i32)])
    is_first = live & (tile != prev_tile)
    is_last = live & ((tile != next_tile) | (step == total - 1))
    visit = tile
    is_final = step == total - 1
    flags = (is_first * F_FIRST + is_last * F_LAST + (visit % 2) * F_PAR
             + (is_first & (visit >= 2)) * F_WAIT_OUT + is_final * F_FINAL
             + (is_final & (visit >= 1)) * F_FINAL_OTHER
             + (is_first & (tile + 1 < n_live_tiles)) * F_HAS_NEXT).astype(i32)
    shape = (n_tiles, 1, CHUNKS_PER_TILE)
    return tile, e_of, lo, hi, flags, src8.reshape(shape), dst8.reshape(shape)


def kernel(x_prompt, x_sample, c, state_ssd_fwd, state_ssd_bwd, c_ctx, w_ada, b_ada, w_in,
           ssd_conv_w, ssd_conv_b, ssd_a_log, ssd_dt_bias, ssd_d, ssd_norm_w, ssd_w_out,
           sc_conv_w, sc_w_out, w_o, ln1_g, ln1_b, w_router, b_router, w_gate_up, b_gate_up,
           w_down, b_down, ln2_g, ln2_b):
    n_ctx_req, ctx_len, _ = x_prompt.shape
    n_lat_req, lat_len, _ = x_sample.shape
    n_ctx = n_ctx_req * ctx_len
    n_lat = n_lat_req * lat_len
    t = n_ctx + n_lat
    assert w_ada.shape[0] == 1, "single trunk layer"
    assert ctx_len % TS_SSD == 0 and lat_len % TS_SSD == 0 and TS_SSD % GRID_W == 0
    assert n_ctx % TM_PROJ == 0 and lat_len % TM_PROJ == 0 and TM_POST % ctx_len == 0
    assert TB_CONV % ctx_len == 0 and ctx_len & (ctx_len - 1) == 0

    x_ctx = x_prompt.reshape(n_ctx, D_MODEL)
    x_lat = x_sample.reshape(n_lat, D_MODEL)

    cvec = jnp.concatenate([c_ctx[None, :], c, jnp.zeros((7 - n_lat_req, D_MODEL), F32)], 0)
    mod = _mod_call(cvec, w_ada[0], b_ada[0])

    w = w_in[0]
    o_dt = D_INNER + XBC_DIM
    w_main = jnp.concatenate([w[:, :o_dt], w[:, o_dt + SSD_HEADS:]], axis=1).astype(BF16)
    w_dt = jnp.pad(w[:, o_dt:o_dt + SSD_HEADS], ((0, 0), (0, LANES - SSD_HEADS))).astype(BF16)
    proj, dt_raw = _inproj_call(x_ctx, x_lat, mod, w_main, w_dt, lat_len)

    xbc = _conv_call(proj, ssd_conv_w[0], ssd_conv_b[0], n_ctx, ctx_len)

    pad_h = ((0, 0), (0, LANES - SSD_HEADS))
    a_log = jnp.pad(ssd_a_log[0], pad_h).reshape(2, 1, LANES)
    dt_bias = jnp.pad(ssd_dt_bias[0], pad_h).reshape(2, 1, LANES)
    dvec = jnp.repeat(ssd_d[0], SSD_HEADDIM).reshape(1, D_INNER)
    h0_lat = jnp.stack([state_ssd_fwd[:, 0].reshape(n_lat_req, D_INNER, SSD_STATE),
                        state_ssd_bwd[:, 0].reshape(n_lat_req, D_INNER, SSD_STATE)], axis=1)
    y_ctx, h_ctx = _ssd_call(xbc, dt_raw, None, a_log, dt_bias, dvec, 0, n_ctx_req, ctx_len, True)
    (y_lat,) = _ssd_call(xbc, dt_raw, h0_lat, a_log, dt_bias, dvec, n_ctx, n_lat_req, lat_len, False)

    w_r = jnp.pad(w_router[0], ((0, 0), (0, LANES - N_EXPERTS)))
    b_r = jnp.pad(b_router[0], (0, LANES - N_EXPERTS)).reshape(1, LANES)
    x1, h2_local, route, top_p, cnt = _post_call(
        x_ctx, x_lat, mod, proj, y_ctx, y_lat, ssd_norm_w[0].reshape(1, D_INNER),
        ssd_w_out[0].astype(BF16), sc_conv_w[0], sc_w_out[0].astype(BF16), w_o[0].astype(BF16),
        ln1_g[0].reshape(1, D_MODEL), ln1_b[0].reshape(1, D_MODEL), w_r, b_r, ctx_len, lat_len)

    tables = _routing_tables(cnt[:, 0, :N_EXPERTS])
    y_local = _ffn_call(tables, h2_local,
                        w_gate_up[0], b_gate_up[0].reshape(N_EXPERTS, 1, 2 * D_EXPERT),
                        w_down[0], b_down[0].reshape(N_EXPERTS, 1, D_MODEL))
    out_ctx, out_lat = _combine_call(x1, route, top_p, cnt, mod, ln2_g[0].reshape(1, D_MODEL),
                                     ln2_b[0].reshape(1, D_MODEL), y_local, n_ctx, lat_len)

    state_shape = (n_ctx_req, 1, SSD_HEADS, SSD_HEADDIM, SSD_STATE)
    return (out_ctx.reshape(n_ctx_req, ctx_len, D_MODEL), out_lat.reshape(n_lat_req, lat_len, D_MODEL),
            h_ctx[0].reshape(state_shape), h_ctx[1].reshape(state_shape))
```

```python
import functools

import jax
import jax.numpy as jnp
from jax import lax
from jax.experimental import pallas as pl
from jax.experimental.pallas import tpu as pltpu

F32 = jnp.float32
BF16 = jnp.bfloat16

D_MODEL = 1024
GRID_W = 64
D_INNER = 2048
SSD_HEADDIM = 64
SSD_HEADS = 32
SSD_GROUPS = 4
SSD_HPG = 8
SSD_STATE = 128
SSD_CONV = 5
SSD_CHUNK = 128
XBC_DIM = D_INNER + 2 * SSD_GROUPS * SSD_STATE
SC_CONV = 3
N_EXPERTS = 32
TOP_K = 4
D_EXPERT = 1024
SWIGLU_LIMIT = 7.0
SWIGLU_ALPHA = 1.702
LN_EPS = 1e-5
RMS_EPS = 1e-5
DEEPNORM_ALPHA = 2.0 ** 0.25
LOG2E = 1.4426950408889634

LANES = 128
MAIN_COLS = 10240
COL_Z = 0
COL_XBC = 2048
COL_SCB = 5120
COL_SCC = 6144
COL_SCV = 7168
COL_GSSD = 8192
COL_GSC = 9216

VMEM_LIMIT = 56 * 1024 * 1024

TM_PROJ = 1024
TN_PROJ = 2048
TB_CONV = 256
TS_SSD = 256
TM_POST = 256
TM_FFN = 512
RUN_ALIGN = 8
CHUNKS_PER_TILE = TM_FFN // RUN_ALIGN
LOCAL_ROWS = TM_POST * TOP_K + N_EXPERTS * RUN_ALIGN
LOCAL_CHUNKS = LOCAL_ROWS // RUN_ALIGN


def _cparams(sem):
    return pltpu.CompilerParams(dimension_semantics=sem, vmem_limit_bytes=VMEM_LIMIT)


def _split3(v):
    hi = v.astype(BF16)
    r1 = v - hi.astype(F32)
    mid = r1.astype(BF16)
    lo = (r1 - mid.astype(F32)).astype(BF16)
    return hi, mid, lo


def _dot(a, b):
    return jnp.dot(a, b, preferred_element_type=F32)


def _dot_exact_lhs(m_bf16, v_f32):
    hi, mid, lo = _split3(v_f32)
    return _dot(m_bf16, hi) + _dot(m_bf16, mid) + _dot(m_bf16, lo)


def _dot_x3(a_f32, b_f32):
    a_hi = a_f32.astype(BF16)
    a_lo = (a_f32 - a_hi.astype(F32)).astype(BF16)
    b_hi = b_f32.astype(BF16)
    b_lo = (b_f32 - b_hi.astype(F32)).astype(BF16)
    return _dot(a_hi, b_hi) + _dot(a_lo, b_hi) + _dot(a_hi, b_lo)


def _silu(v):
    return v * jax.nn.sigmoid(v)


def _softplus(v):
    return jnp.maximum(v, 0.0) + jnp.log(1.0 + jnp.exp(-jnp.abs(v)))


def _mod_kernel(c_ref, w_ref, b_ref, o_ref):
    o_ref[...] = _dot_x3(_silu(c_ref[...]), w_ref[...]) + b_ref[...]


def _mod_call(cvec, w_ada, b_ada):
    n = w_ada.shape[1]
    tn = 1536
    return pl.pallas_call(
        _mod_kernel,
        out_shape=jax.ShapeDtypeStruct((cvec.shape[0], n), F32),
        grid=(n // tn,),
        in_specs=[
            pl.BlockSpec(cvec.shape, lambda j: (0, 0)),
            pl.BlockSpec((D_MODEL, tn), lambda j: (0, j)),
            pl.BlockSpec((1, tn), lambda j: (0, j)),
        ],
        out_specs=pl.BlockSpec((cvec.shape[0], tn), lambda j: (0, j)),
        compiler_params=_cparams(("arbitrary",)),
        name="mod",
    )(cvec, w_ada, b_ada.reshape(1, n))


def _mod_row(block, rows_per_block, n_ctx_tokens, lat_len):
    tok = block * rows_per_block
    return jnp.where(tok < n_ctx_tokens, 0, 1 + (tok - n_ctx_tokens) // lat_len)


def _ctx_lat_specs(block_shape, n_ctx_blocks, lead=()):
    tail = (0,) * (len(block_shape) - len(lead) - 1)
    ctx = pl.BlockSpec(block_shape, lambda i, *_: lead + (jnp.minimum(i, n_ctx_blocks - 1),) + tail)
    lat = pl.BlockSpec(block_shape, lambda i, *_: lead + (jnp.maximum(i - n_ctx_blocks, 0),) + tail)
    return ctx, lat


def _inproj_kernel(xc_ref, xl_ref, mod_ref, w_ref, wdt_ref, o_ref, dt_ref, h_scr, *, n_ctx, lat_len):
    i = pl.program_id(0)
    j = pl.program_id(1)

    @pl.when(j == 0)
    def _():
        r = _mod_row(i, TM_PROJ, n_ctx, lat_len)
        shift = mod_ref[pl.ds(r, 1), 0:D_MODEL]
        scale = mod_ref[pl.ds(r, 1), D_MODEL:2 * D_MODEL]
        x = jnp.where(i * TM_PROJ < n_ctx, xc_ref[...], xl_ref[...])
        h = (x * (1.0 + scale) + shift).astype(BF16)
        h_scr[...] = h
        dt_ref[...] = _dot(h, wdt_ref[...])

    o_ref[...] = _dot(h_scr[...], w_ref[...]).astype(BF16)


def _inproj_call(x_ctx, x_lat, mod, w_main, w_dt, lat_len):
    n_ctx = x_ctx.shape[0]
    t = n_ctx + x_lat.shape[0]
    kern = functools.partial(_inproj_kernel, n_ctx=n_ctx, lat_len=lat_len)
    return pl.pallas_call(
        kern,
        out_shape=(jax.ShapeDtypeStruct((t, MAIN_COLS), BF16),
                   jax.ShapeDtypeStruct((t, LANES), F32)),
        grid=(t // TM_PROJ, MAIN_COLS // TN_PROJ),
        in_specs=[
            *_ctx_lat_specs((TM_PROJ, D_MODEL), n_ctx // TM_PROJ),
            pl.BlockSpec(mod.shape, lambda i, j: (0, 0)),
            pl.BlockSpec((D_MODEL, TN_PROJ), lambda i, j: (0, j)),
            pl.BlockSpec((D_MODEL, LANES), lambda i, j: (0, 0)),
        ],
        out_specs=(pl.BlockSpec((TM_PROJ, TN_PROJ), lambda i, j: (i, j)),
                   pl.BlockSpec((TM_PROJ, LANES), lambda i, j: (i, 0))),
        scratch_shapes=[pltpu.VMEM((TM_PROJ, D_MODEL), BF16)],
        compiler_params=_cparams(("arbitrary", "arbitrary")),
        name="inproj",
    )(x_ctx, x_lat, mod, w_main, w_dt)


def _shift_rows(x, off, pos, row_len):
    if off == 0:
        return x
    n = x.shape[0]
    rolled = pltpu.roll(x, (-off) % n, 0)
    ok = (pos + off >= 0) & (pos + off < row_len)
    return jnp.where(ok, rolled, 0.0)


def _row_pos(n, row_len):
    t = lax.broadcasted_iota(jnp.int32, (n, 1), 0)
    return jnp.bitwise_and(t, row_len - 1)


def _conv_kernel(x_ref, w_ref, b_ref, o_ref, *, n_ctx, ctx_len):
    i = pl.program_id(0)
    row_len = jnp.where(i * TB_CONV < n_ctx, ctx_len, GRID_W)
    pos = _row_pos(TB_CONV, row_len)
    x = x_ref[...].astype(F32)
    half = SSD_CONV // 2
    acc = x * w_ref[half:half + 1, :]
    for k in range(SSD_CONV):
        if k != half:
            acc = acc + _shift_rows(x, k - half, pos, row_len) * w_ref[k:k + 1, :]
    o_ref[...] = _silu(acc + b_ref[...]).astype(BF16)


def _conv_call(proj, conv_w, conv_b, n_ctx, ctx_len):
    t = proj.shape[0]
    tc = 1024
    kern = functools.partial(_conv_kernel, n_ctx=n_ctx, ctx_len=ctx_len)
    return pl.pallas_call(
        kern,
        out_shape=jax.ShapeDtypeStruct((t, XBC_DIM), BF16),
        grid=(t // TB_CONV, XBC_DIM // tc),
        in_specs=[
            pl.BlockSpec((TB_CONV, tc), lambda i, j: (i, COL_XBC // tc + j)),
            pl.BlockSpec((SSD_CONV, tc), lambda i, j: (0, j)),
            pl.BlockSpec((1, tc), lambda i, j: (0, j)),
        ],
        out_specs=pl.BlockSpec((TB_CONV, tc), lambda i, j: (i, j)),
        compiler_params=_cparams(("arbitrary", "arbitrary")),
        name="conv",
    )(proj, conv_w, conv_b.reshape(1, XBC_DIM))


def _ssd_kernel(*refs, has_h0, want_final):
    xs_ref, b_ref, c_ref, dt_ref = refs[:4]
    rest = list(refs[4:])
    h0_ref = rest.pop(0) if has_h0 else None
    alog_ref, bias_ref, dvec_ref, y_ref = rest[:4]
    hfin_ref = rest[4] if want_final else None
    ht_scr = rest[-1]
    d = pl.program_id(1)
    st = pl.program_id(2)
    q = SSD_CHUNK
    n_chunks = TS_SSD // q
    n_blk = D_INNER // LANES
    is_fwd = d == 0

    @pl.when(st == 0)
    def _():
        if has_h0:
            for j in range(n_blk):
                ht_scr[:, j * LANES:(j + 1) * LANES] = h0_ref[0, 0, j * LANES:(j + 1) * LANES, :].T
        else:
            ht_scr[...] = jnp.zeros_like(ht_scr)

    a2_neg = -jnp.exp(alog_ref[0]) * LOG2E
    bias = bias_ref[0]
    row = lax.broadcasted_iota(jnp.int32, (q, q), 0)
    col = lax.broadcasted_iota(jnp.int32, (q, q), 1)
    keep = (row - col) * jnp.where(is_fwd, 1, -1) >= 0
    tri = keep.astype(BF16)
    lane_lo = lax.broadcasted_iota(jnp.int32, (q, LANES), 1) < SSD_HEADDIM
    lane_lo1 = lane_lo[0:1, :]
    d_on = jnp.where(is_fwd, 1.0, 0.0)
    neg_inf = jnp.float32(-jnp.inf)

    def chunk_body(ci, carry):
        c = jnp.where(is_fwd, ci, n_chunks - 1 - ci)
        c0 = pl.multiple_of(c * q, q)
        dt = _softplus(dt_ref[pl.ds(c0, q), :] + bias)
        acs = _dot_exact_lhs(tri, dt * a2_neg)
        acs_t = acs.T
        dt_t = dt.T
        tot_row = jnp.where(is_fwd, acs[q - 1:q, :], acs[0:1, :])
        tot_col = jnp.where(is_fwd, acs_t[:, q - 1:q], acs_t[:, 0:1])
        e_tot = jnp.exp2(tot_row)
        src_t = acs_t - jnp.log2(dt_t)
        dte_t = jnp.exp2(tot_col - acs_t) * dt_t

        for g in range(SSD_GROUPS):
            bg = b_ref[pl.ds(c0, q), g * SSD_STATE:(g + 1) * SSD_STATE]
            cg = c_ref[pl.ds(c0, q), g * SSD_STATE:(g + 1) * SSD_STATE]
            cb = lax.dot_general(cg, bg, (((1,), (1,)), ((), ())),
                                 preferred_element_type=F32)
            cgf = cg.astype(F32)
            bg_t = bg.astype(F32).T
            for p in range(SSD_HPG // 2):
                h0 = g * SSD_HPG + 2 * p
                lo = h0 * SSD_HEADDIM
                xp = xs_ref[pl.ds(c0, q), lo:lo + LANES]
                htp = ht_scr[:, lo:lo + LANES]
                rhs = jnp.concatenate([xp, htp.astype(BF16)], axis=0)
                ys, sts = [], []
                for hh in (h0, h0 + 1):
                    tgt = jnp.broadcast_to(acs[:, hh:hh + 1], (q, q))
                    seg = jnp.where(keep, tgt - src_t[hh:hh + 1, :], neg_inf)
                    w = (cb * jnp.exp2(seg)).astype(BF16)
                    ce = (cgf * jnp.exp2(tgt)).astype(BF16)
                    ys.append(_dot(jnp.concatenate([w, ce], axis=1), rhs))
                    sts.append(_dot((bg_t * dte_t[hh:hh + 1, :]).astype(BF16), xp))
                y = (jnp.where(lane_lo, ys[0], ys[1])
                     + (d_on * dvec_ref[:, lo:lo + LANES]) * xp.astype(F32))
                y_ref[0, pl.ds(c0, q), lo:lo + LANES] = y.astype(BF16)
                dec = jnp.where(lane_lo1, e_tot[:, h0:h0 + 1], e_tot[:, h0 + 1:h0 + 2])
                ht_scr[:, lo:lo + LANES] = htp * dec + jnp.where(lane_lo, sts[0], sts[1])
        return carry

    lax.fori_loop(0, n_chunks, chunk_body, 0)

    if want_final:
        @pl.when(st == pl.num_programs(2) - 1)
        def _():
            for j in range(n_blk):
                hfin_ref[0, 0, j * LANES:(j + 1) * LANES, :] = ht_scr[:, j * LANES:(j + 1) * LANES].T


def _ssd_call(xbc, dt_raw, h0, a_log, dt_bias, dvec, tok0, n_seq, seq_len, want_final):
    n_steps = seq_len // TS_SSD
    blk0 = tok0 // TS_SSD

    def tok_blk(b, d, s):
        return blk0 + b * n_steps + jnp.where(d == 0, s, n_steps - 1 - s)

    state_spec = pl.BlockSpec((1, 1, D_INNER, SSD_STATE), lambda b, d, s: (b, d, 0, 0))
    in_specs = [
        pl.BlockSpec((TS_SSD, D_INNER), lambda b, d, s: (tok_blk(b, d, s), 0)),
        pl.BlockSpec((TS_SSD, 512), lambda b, d, s: (tok_blk(b, d, s), 4)),
        pl.BlockSpec((TS_SSD, 512), lambda b, d, s: (tok_blk(b, d, s), 5)),
        pl.BlockSpec((TS_SSD, LANES), lambda b, d, s: (tok_blk(b, d, s), 0)),
    ]
    args = [xbc, xbc, xbc, dt_raw]
    if h0 is not None:
        in_specs.append(state_spec)
        args.append(h0)
    in_specs += [
        pl.BlockSpec((1, 1, LANES), lambda b, d, s: (d, 0, 0)),
        pl.BlockSpec((1, 1, LANES), lambda b, d, s: (d, 0, 0)),
        pl.BlockSpec((1, D_INNER), lambda b, d, s: (0, 0)),
    ]
    args += [a_log, dt_bias, dvec]
    out_shape = [jax.ShapeDtypeStruct((2, n_seq * seq_len, D_INNER), BF16)]
    out_specs = [pl.BlockSpec((1, TS_SSD, D_INNER),
                              lambda b, d, s: (d, tok_blk(b, d, s) - blk0, 0))]
    if want_final:
        out_shape.append(jax.ShapeDtypeStruct((2, n_seq, D_INNER, SSD_STATE), F32))
        out_specs.append(pl.BlockSpec((1, 1, D_INNER, SSD_STATE), lambda b, d, s: (d, b, 0, 0)))
    kern = functools.partial(_ssd_kernel, has_h0=h0 is not None, want_final=want_final)
    return pl.pallas_call(
        kern,
        out_shape=tuple(out_shape),
        grid=(n_seq, 2, n_steps),
        in_specs=in_specs,
        out_specs=tuple(out_specs),
        scratch_shapes=[pltpu.VMEM((SSD_STATE, D_INNER), F32)],
        compiler_params=_cparams(("arbitrary", "arbitrary", "arbitrary")),
        name="ssd_final" if want_final else "ssd",
    )(*args)


def _post_kernel(xc_ref, xl_ref, mod_ref, z_ref, scb_ref, scc_ref, scv_ref, gssd_ref, gsc_ref,
                 yc_ref, yl_ref, normw_ref, wssd_ref, scw_ref, wsc_ref, wo_ref, g1_ref, b1_ref,
                 wr_ref, br_ref, x1_ref, h2l_ref, ti_ref, tp_ref, cnt_ref,
                 *, n_ctx, ctx_len, lat_len):
    i = pl.program_id(0)
    tm = TM_POST
    r = _mod_row(i, tm, n_ctx, lat_len)
    is_ctx = i * tm < n_ctx

    def mod_vec(k):
        return mod_ref[pl.ds(r, 1), k * D_MODEL:(k + 1) * D_MODEL]

    gate1, shift2, scale2 = mod_vec(2), mod_vec(3), mod_vec(4)

    y_f = jnp.where(is_ctx, yc_ref[0], yl_ref[0]).astype(F32)
    y_b = jnp.where(is_ctx, yc_ref[1], yl_ref[1]).astype(F32)
    y = (y_f + y_b) * _silu(z_ref[...].astype(F32))
    gw = D_INNER // SSD_GROUPS
    parts = []
    for g in range(SSD_GROUPS):
        yg = y[:, g * gw:(g + 1) * gw]
        ms = jnp.mean(yg * yg, axis=-1, keepdims=True)
        parts.append(yg * lax.rsqrt(ms + RMS_EPS))
    yn = (jnp.concatenate(parts, axis=-1) * normw_ref[...]).astype(BF16)
    y_ssd = _dot(yn, wssd_ref[...])

    row_len = jnp.where(i * tm < n_ctx, ctx_len, GRID_W)
    pos = _row_pos(tm, row_len)
    u = scc_ref[...].astype(F32) * scv_ref[...].astype(F32)
    half = SC_CONV // 2
    cv = u * scw_ref[half:half + 1, :]
    for k in range(SC_CONV):
        if k != half:
            cv = cv + _shift_rows(u, k - half, pos, row_len) * scw_ref[k:k + 1, :]
    y_sc = _dot((scb_ref[...].astype(F32) * cv).astype(BF16), wsc_ref[...])

    mixed = (jax.nn.sigmoid(gssd_ref[...].astype(F32)) * y_ssd
             + jax.nn.sigmoid(gsc_ref[...].astype(F32)) * y_sc)
    o = _dot(mixed.astype(BF16), wo_ref[...])

    x = jnp.where(is_ctx, xc_ref[...], xl_ref[...])
    res = DEEPNORM_ALPHA * x + gate1 * o
    mu = jnp.mean(res, axis=-1, keepdims=True)
    cen = res - mu
    var = jnp.mean(cen * cen, axis=-1, keepdims=True)
    x1 = cen * lax.rsqrt(var + LN_EPS) * g1_ref[...] + b1_ref[...]
    x1_ref[...] = x1
    h2 = x1 * (1.0 + scale2) + shift2

    logits = _dot_x3(h2, wr_ref[...]) + br_ref[...]
    lane = lax.broadcasted_iota(jnp.int32, (tm, LANES), 1)
    lane_f = lane.astype(F32)
    neg = jnp.float32(-jnp.inf)
    work = jnp.where(lane < N_EXPERTS, logits, neg)
    vals, idxs, hits = [], [], []
    for _ in range(TOP_K):
        m = jnp.max(work, axis=-1, keepdims=True)
        idx = jnp.min(jnp.where(work == m, lane_f, float(LANES)), axis=-1, keepdims=True)
        hit = lane_f == idx
        vals.append(m)
        idxs.append(idx)
        hits.append(hit)
        work = jnp.where(hit, neg, work)
    es = [jnp.exp(v - vals[0]) for v in vals]
    denom = es[0] + es[1] + es[2] + es[3]

    chosen = jnp.where(hits[0] | hits[1] | hits[2] | hits[3], 1.0, 0.0)
    cnt = jnp.sum(chosen, axis=0, keepdims=True)
    cnt_ref[0] = cnt.astype(jnp.int32)
    cpad = jnp.floor((cnt + (RUN_ALIGN - 1)) * (1.0 / RUN_ALIGN)) * RUN_ALIGN
    er = lax.broadcasted_iota(jnp.int32, (LANES, LANES), 0)
    ec = lax.broadcasted_iota(jnp.int32, (LANES, LANES), 1)
    run_off = _dot(jnp.broadcast_to(cpad, (8, LANES)).astype(BF16), (er < ec).astype(BF16))[0:1]
    rr = lax.broadcasted_iota(jnp.int32, (tm, tm), 0)
    cc = lax.broadcasted_iota(jnp.int32, (tm, tm), 1)
    slot = _dot((rr > cc).astype(BF16), chosen.astype(BF16)) + run_off
    lrows = [jnp.sum(jnp.where(hits[k], slot, 0.0), axis=-1, keepdims=True) for k in range(TOP_K)]

    ti = jnp.zeros((tm, LANES), F32)
    tp = jnp.zeros((tm, LANES), F32)
    lmat = jnp.full((tm, LANES), -1.0, F32)
    for k in range(TOP_K):
        ti = jnp.where(lane == k, idxs[k], ti)
        ti = jnp.where(lane == TOP_K + k, lrows[k], ti)
        lmat = jnp.where(lane == k, lrows[k], lmat)
        tp = jnp.where(lane == k, es[k] / denom, tp)
    ti_ref[...] = ti.astype(jnp.int32)
    tp_ref[...] = tp

    lrow_t = jnp.concatenate([lmat[j * LANES:(j + 1) * LANES, :].T for j in range(tm // LANES)],
                             axis=1)
    jrow = lax.broadcasted_iota(jnp.int32, (LOCAL_ROWS, tm), 0).astype(F32)
    sel_t = jnp.where(jrow == lrow_t[0:1, :], 1.0, 0.0)
    for k in range(1, TOP_K):
        sel_t = sel_t + jnp.where(jrow == lrow_t[k:k + 1, :], 1.0, 0.0)
    h2l_ref[...] = _dot(sel_t.astype(BF16), h2.astype(BF16))


def _post_call(x_ctx, x_lat, mod, proj, y_ctx, y_lat, norm_w, w_ssd, sc_w, w_sc, w_o, ln_g, ln_b,
               w_r, b_r, ctx_len, lat_len):
    n_ctx = x_ctx.shape[0]
    t = n_ctx + x_lat.shape[0]
    tm = TM_POST
    kern = functools.partial(_post_kernel, n_ctx=n_ctx, ctx_len=ctx_len, lat_len=lat_len)

    def colblk(off, width):
        return pl.BlockSpec((tm, width), lambda i: (i, off // width))

    def whole(a):
        return pl.BlockSpec(a.shape, lambda i: (0,) * a.ndim, pipeline_mode=pl.Buffered(1))

    return pl.pallas_call(
        kern,
        out_shape=(jax.ShapeDtypeStruct((t, D_MODEL), F32),
                   jax.ShapeDtypeStruct((t // tm * LOCAL_ROWS, D_MODEL), F32),
                   jax.ShapeDtypeStruct((t, LANES), jnp.int32),
                   jax.ShapeDtypeStruct((t, LANES), F32),
                   jax.ShapeDtypeStruct((t // tm, 1, LANES), jnp.int32)),
        grid=(t // tm,),
        in_specs=[
            *_ctx_lat_specs((tm, D_MODEL), n_ctx // tm),
            whole(mod),
            colblk(COL_Z, D_INNER),
            colblk(COL_SCB, D_MODEL), colblk(COL_SCC, D_MODEL), colblk(COL_SCV, D_MODEL),
            colblk(COL_GSSD, D_MODEL), colblk(COL_GSC, D_MODEL),
            *_ctx_lat_specs((2, tm, D_INNER), n_ctx // tm, lead=(0,)),
            whole(norm_w), whole(w_ssd), whole(sc_w), whole(w_sc), whole(w_o),
            whole(ln_g), whole(ln_b), whole(w_r), whole(b_r),
        ],
        out_specs=(pl.BlockSpec((tm, D_MODEL), lambda i: (i, 0)),
                   pl.BlockSpec((LOCAL_ROWS, D_MODEL), lambda i: (i, 0)),
                   pl.BlockSpec((tm, LANES), lambda i: (i, 0)),
                   pl.BlockSpec((tm, LANES), lambda i: (i, 0)),
                   pl.BlockSpec((1, 1, LANES), lambda i: (i, 0, 0))),
        compiler_params=_cparams(("arbitrary",)),
        name="post",
    )(x_ctx, x_lat, mod, proj, proj, proj, proj, proj, proj, y_ctx, y_lat,
      norm_w, w_ssd, sc_w, w_sc, w_o, ln_g, ln_b, w_r, b_r)


def _ffn_kernel(nlive_ref, exp_ref, nv_ref, src_cur_ref, src_nxt_ref, dst_cur_ref,
                h2l_hbm, wgu_ref, bgu_ref, wd_ref, bd_ref,
                yl_hbm, wgu_scr, wd_scr, x_buf, y_buf, sem_x, sem_y):
    s = pl.program_id(0)
    par = s & 1
    n_live = nlive_ref[0]
    nv = nv_ref[s]
    new_expert = (s == 0) | (exp_ref[s] != exp_ref[jnp.maximum(s - 1, 0)])

    def gather(tbl_ref, slot):
        for c in range(CHUNKS_PER_TILE):
            src = pl.multiple_of(tbl_ref[0, 0, c] * RUN_ALIGN, RUN_ALIGN)
            pltpu.make_async_copy(h2l_hbm.at[pl.ds(src, RUN_ALIGN)],
                                  x_buf.at[slot, pl.ds(c * RUN_ALIGN, RUN_ALIGN)],
                                  sem_x.at[slot]).start()

    def out_copy(slot, c, dst):
        return pltpu.make_async_copy(y_buf.at[slot, pl.ds(c * RUN_ALIGN, RUN_ALIGN)],
                                     yl_hbm.at[pl.ds(dst, RUN_ALIGN)], sem_y.at[slot])

    def wait_out(slot, count):
        def one(c, carry):
            out_copy(slot, 0, 0).wait()
            return carry
        lax.fori_loop(0, count, one, 0)

    @pl.when(s == 0)
    def _():
        gather(src_cur_ref, 0)

    @pl.when(s < n_live)
    def _():
        pltpu.make_async_copy(h2l_hbm.at[pl.ds(0, TM_FFN)], x_buf.at[par], sem_x.at[par]).wait()

        @pl.when(s + 1 < n_live)
        def _():
            gather(src_nxt_ref, 1 - par)

        @pl.when(new_expert)
        def _():
            wgu_scr[...] = wgu_ref[0].astype(BF16)
            wd_scr[...] = wd_ref[0].astype(BF16)

        gu = _dot(x_buf[par].astype(BF16), wgu_scr[...]) + bgu_ref[0]
        g = jnp.minimum(gu[:, :D_EXPERT], SWIGLU_LIMIT)
        u = jnp.clip(gu[:, D_EXPERT:], -SWIGLU_LIMIT, SWIGLU_LIMIT)
        act = (u + 1.0) * g * jax.nn.sigmoid(SWIGLU_ALPHA * g)
        y = _dot(act.astype(BF16), wd_scr[...]) + bd_ref[0]

        @pl.when(s >= 2)
        def _():
            wait_out(par, nv_ref[jnp.maximum(s - 2, 0)])

        y_buf[par] = y

        def scatter(c, carry):
            dst = pl.multiple_of(dst_cur_ref[0, 0, c] * RUN_ALIGN, RUN_ALIGN)
            out_copy(par, c, dst).start()
            return carry
        lax.fori_loop(0, nv, scatter, 0)

        @pl.when(s == n_live - 1)
        def _():
            wait_out(par, nv)

            @pl.when(s >= 1)
            def _():
                wait_out(1 - par, nv_ref[jnp.maximum(s - 1, 0)])


def _ffn_call(tables, h2l, w_gu, b_gu, w_d, b_d):
    n_live, exp_id, n_valid, src8, dst8 = tables
    n_tiles = src8.shape[0]

    def tbl_spec(index_fn):
        return pl.BlockSpec((1, 1, CHUNKS_PER_TILE), index_fn, memory_space=pltpu.SMEM)

    def per_expert(shape):
        return pl.BlockSpec((1,) + shape, lambda s, nl, ei, *_: (ei[s], 0, 0))

    grid_spec = pltpu.PrefetchScalarGridSpec(
        num_scalar_prefetch=3,
        grid=(n_tiles,),
        in_specs=[
            tbl_spec(lambda s, *_: (s, 0, 0)),
            tbl_spec(lambda s, *_: (jnp.minimum(s + 1, n_tiles - 1), 0, 0)),
            tbl_spec(lambda s, *_: (s, 0, 0)),
            pl.BlockSpec(memory_space=pl.ANY),
            per_expert((D_MODEL, 2 * D_EXPERT)), per_expert((1, 2 * D_EXPERT)),
            per_expert((D_EXPERT, D_MODEL)), per_expert((1, D_MODEL)),
        ],
        out_specs=pl.BlockSpec(memory_space=pl.ANY),
        scratch_shapes=[pltpu.VMEM((D_MODEL, 2 * D_EXPERT), BF16),
                        pltpu.VMEM((D_EXPERT, D_MODEL), BF16),
                        pltpu.VMEM((2, TM_FFN, D_MODEL), F32),
                        pltpu.VMEM((2, TM_FFN, D_MODEL), F32),
                        pltpu.SemaphoreType.DMA((2,)),
                        pltpu.SemaphoreType.DMA((2,))],
    )
    tables_and_chunks = (n_live, exp_id, n_valid, src8, src8, dst8)
    return pl.pallas_call(
        _ffn_kernel,
        out_shape=jax.ShapeDtypeStruct(h2l.shape, F32),
        grid_spec=grid_spec,
        input_output_aliases={len(tables_and_chunks): 0},
        compiler_params=pltpu.CompilerParams(dimension_semantics=("arbitrary",),
                                             vmem_limit_bytes=VMEM_LIMIT,
                                             has_side_effects=True),
        name="ffn",
    )(*tables_and_chunks, h2l, w_gu, b_gu, w_d, b_d)


def _combine_kernel(x1_ref, ti_ref, tp_ref, cnt_ref, mod_ref, g2_ref, b2_ref, yl_ref,
                    oc_ref, ol_ref, *, n_ctx, lat_len):
    i = pl.program_id(0)
    tb = TM_POST

    jl = lax.broadcasted_iota(jnp.int32, (tb, LOCAL_ROWS), 1)
    ti = ti_ref[...]
    tp = tp_ref[...]
    sel = jnp.where(jl == ti[:, TOP_K:TOP_K + 1], tp[:, 0:1], 0.0)
    for k in range(1, TOP_K):
        sel = sel + jnp.where(jl == ti[:, TOP_K + k:TOP_K + k + 1], tp[:, k:k + 1], 0.0)
    cnt = cnt_ref[0].astype(F32)
    cpad = jnp.floor((cnt + (RUN_ALIGN - 1)) * (1.0 / RUN_ALIGN)) * RUN_ALIGN
    used = jnp.sum(cpad, axis=-1, keepdims=True)
    rowi = lax.broadcasted_iota(jnp.int32, (LOCAL_ROWS, 1), 0).astype(F32)
    yl = jnp.where(rowi < used, yl_ref[...], 0.0)
    moe = _dot(sel.astype(BF16), yl.astype(BF16))

    r = _mod_row(i, tb, n_ctx, lat_len)
    gate2 = mod_ref[pl.ds(r, 1), 5 * D_MODEL:6 * D_MODEL]
    res = DEEPNORM_ALPHA * x1_ref[...] + gate2 * moe
    mu = jnp.mean(res, axis=-1, keepdims=True)
    cen = res - mu
    var = jnp.mean(cen * cen, axis=-1, keepdims=True)
    out = cen * lax.rsqrt(var + LN_EPS) * g2_ref[...] + b2_ref[...]

    @pl.when(i * tb < n_ctx)
    def _():
        oc_ref[...] = out

    @pl.when(i * tb >= n_ctx)
    def _():
        ol_ref[...] = out


def _combine_call(x1, route, tp, cnt, mod, ln_g, ln_b, y_local, n_ctx, lat_len):
    t = x1.shape[0]
    tb = TM_POST
    kern = functools.partial(_combine_kernel, n_ctx=n_ctx, lat_len=lat_len)
    return pl.pallas_call(
        kern,
        out_shape=(jax.ShapeDtypeStruct((n_ctx, D_MODEL), F32),
                   jax.ShapeDtypeStruct((t - n_ctx, D_MODEL), F32)),
        grid=(t // tb,),
        in_specs=[
            pl.BlockSpec((tb, D_MODEL), lambda i: (i, 0)),
            pl.BlockSpec((tb, LANES), lambda i: (i, 0)),
            pl.BlockSpec((tb, LANES), lambda i: (i, 0)),
            pl.BlockSpec((1, 1, LANES), lambda i: (i, 0, 0)),
            pl.BlockSpec(mod.shape, lambda i: (0, 0)),
            pl.BlockSpec((1, D_MODEL), lambda i: (0, 0)),
            pl.BlockSpec((1, D_MODEL), lambda i: (0, 0)),
            pl.BlockSpec((LOCAL_ROWS, D_MODEL), lambda i: (i, 0)),
        ],
        out_specs=_ctx_lat_specs((tb, D_MODEL), n_ctx // tb),
        compiler_params=_cparams(("arbitrary",)),
        name="combine",
    )(x1, route, tp, cnt, mod, ln_g, ln_b, y_local)


def _routing_tables(cnt):
    i32 = jnp.int32
    n_blocks = cnt.shape[0]
    max_rows = n_blocks * (TM_POST * TOP_K + N_EXPERTS * (RUN_ALIGN - 1)) + N_EXPERTS * (TM_FFN - RUN_ALIGN)
    n_tiles = -(-max_rows // TM_FFN)

    cpad = (cnt + (RUN_ALIGN - 1)) // RUN_ALIGN * RUN_ALIGN
    run_loc = jnp.cumsum(cpad, axis=1) - cpad
    used = cpad.sum(axis=0)
    region = (used + (TM_FFN - 1)) // TM_FFN * TM_FFN
    incl = jnp.cumsum(region)
    offs = incl - region
    n_live = incl[-1] // TM_FFN
    run_glob = offs[None, :] + jnp.cumsum(cpad, axis=0) - cpad

    g_row = jnp.arange(n_tiles * CHUNKS_PER_TILE, dtype=i32) * RUN_ALIGN
    rs = run_glob.reshape(-1)
    re = rs + cpad.reshape(-1)
    ls = (jnp.arange(n_blocks, dtype=i32)[:, None] * LOCAL_ROWS + run_loc).reshape(-1)
    inside = (g_row[:, None] >= rs[None, :]) & (g_row[:, None] < re[None, :])
    valid = jnp.any(inside, axis=1)
    local_row = g_row + jnp.sum(jnp.where(inside, (ls - rs)[None, :], 0), axis=1)
    src8 = jnp.where(valid, local_row // RUN_ALIGN, LOCAL_CHUNKS - 1).astype(i32)
    dst8 = jnp.where(valid, local_row // RUN_ALIGN, 0).astype(i32)

    row0 = jnp.minimum(jnp.arange(n_tiles, dtype=i32), n_live - 1) * TM_FFN
    e_of = jnp.sum(incl[None, :] <= row0[:, None], axis=1)
    e_of = jnp.minimum(e_of, N_EXPERTS - 1).astype(i32)
    sel = e_of[:, None] == jnp.arange(N_EXPERTS, dtype=i32)
    run_end = jnp.sum(jnp.where(sel, (offs + used)[None, :], 0), axis=1)
    n_valid = jnp.clip((run_end - row0) // RUN_ALIGN, 0, CHUNKS_PER_TILE)
    n_valid = jnp.where(jnp.arange(n_tiles, dtype=i32) < n_live, n_valid, 0).astype(i32)
    shape = (n_tiles, 1, CHUNKS_PER_TILE)
    return (n_live.reshape(1).astype(i32), e_of, n_valid, src8.reshape(shape), dst8.reshape(shape))


def kernel(x_prompt, x_sample, c, state_ssd_fwd, state_ssd_bwd, c_ctx, w_ada, b_ada, w_in,
           ssd_conv_w, ssd_conv_b, ssd_a_log, ssd_dt_bias, ssd_d, ssd_norm_w, ssd_w_out,
           sc_conv_w, sc_w_out, w_o, ln1_g, ln1_b, w_router, b_router, w_gate_up, b_gate_up,
           w_down, b_down, ln2_g, ln2_b):
    n_ctx_req, ctx_len, _ = x_prompt.shape
    n_lat_req, lat_len, _ = x_sample.shape
    n_ctx = n_ctx_req * ctx_len
    n_lat = n_lat_req * lat_len
    t = n_ctx + n_lat
    assert w_ada.shape[0] == 1, "single trunk layer"
    assert ctx_len % TS_SSD == 0 and lat_len % TS_SSD == 0 and TS_SSD % GRID_W == 0
    assert n_ctx % TM_PROJ == 0 and lat_len % TM_PROJ == 0 and TM_POST % ctx_len == 0
    assert TB_CONV % ctx_len == 0 and ctx_len & (ctx_len - 1) == 0

    x_ctx = x_prompt.reshape(n_ctx, D_MODEL)
    x_lat = x_sample.reshape(n_lat, D_MODEL)

    cvec = jnp.concatenate([c_ctx[None, :], c, jnp.zeros((7 - n_lat_req, D_MODEL), F32)], 0)
    mod = _mod_call(cvec, w_ada[0], b_ada[0])

    w = w_in[0]
    o_dt = D_INNER + XBC_DIM
    w_main = jnp.concatenate([w[:, :o_dt], w[:, o_dt + SSD_HEADS:]], axis=1).astype(BF16)
    w_dt = jnp.pad(w[:, o_dt:o_dt + SSD_HEADS], ((0, 0), (0, LANES - SSD_HEADS))).astype(BF16)
    proj, dt_raw = _inproj_call(x_ctx, x_lat, mod, w_main, w_dt, lat_len)

    xbc = _conv_call(proj, ssd_conv_w[0], ssd_conv_b[0], n_ctx, ctx_len)

    pad_h = ((0, 0), (0, LANES - SSD_HEADS))
    a_log = jnp.pad(ssd_a_log[0], pad_h).reshape(2, 1, LANES)
    dt_bias = jnp.pad(ssd_dt_bias[0], pad_h).reshape(2, 1, LANES)
    dvec = jnp.repeat(ssd_d[0], SSD_HEADDIM).reshape(1, D_INNER)
    h0_lat = jnp.stack([state_ssd_fwd[:, 0].reshape(n_lat_req, D_INNER, SSD_STATE),
                        state_ssd_bwd[:, 0].reshape(n_lat_req, D_INNER, SSD_STATE)], axis=1)
    y_ctx, h_ctx = _ssd_call(xbc, dt_raw, None, a_log, dt_bias, dvec, 0, n_ctx_req, ctx_len, True)
    (y_lat,) = _ssd_call(xbc, dt_raw, h0_lat, a_log, dt_bias, dvec, n_ctx, n_lat_req, lat_len, False)

    w_r = jnp.pad(w_router[0], ((0, 0), (0, LANES - N_EXPERTS)))
    b_r = jnp.pad(b_router[0], (0, LANES - N_EXPERTS)).reshape(1, LANES)
    x1, h2_local, route, top_p, cnt = _post_call(
        x_ctx, x_lat, mod, proj, y_ctx, y_lat, ssd_norm_w[0].reshape(1, D_INNER),
        ssd_w_out[0].astype(BF16), sc_conv_w[0], sc_w_out[0].astype(BF16), w_o[0].astype(BF16),
        ln1_g[0].reshape(1, D_MODEL), ln1_b[0].reshape(1, D_MODEL), w_r, b_r, ctx_len, lat_len)

    tables = _routing_tables(cnt[:, 0, :N_EXPERTS])
    y_local = _ffn_call(tables, h2_local,
                        w_gate_up[0], b_gate_up[0].reshape(N_EXPERTS, 1, 2 * D_EXPERT),
                        w_down[0], b_down[0].reshape(N_EXPERTS, 1, D_MODEL))
    out_ctx, out_lat = _combine_call(x1, route, top_p, cnt, mod, ln2_g[0].reshape(1, D_MODEL),
                                     ln2_b[0].reshape(1, D_MODEL), y_local, n_ctx, lat_len)

    state_shape = (n_ctx_req, 1, SSD_HEADS, SSD_HEADDIM, SSD_STATE)
    return (out_ctx.reshape(n_ctx_req, ctx_len, D_MODEL), out_lat.reshape(n_lat_req, lat_len, D_MODEL),
            h_ctx[0].reshape(state_shape), h_ctx[1].reshape(state_shape))
```

```python
import functools

import jax
import jax.numpy as jnp
from jax import lax
from jax.experimental import pallas as pl
from jax.experimental.pallas import tpu as pltpu

F32 = jnp.float32
BF16 = jnp.bfloat16

D_MODEL = 1024
GRID_W = 64
D_INNER = 2048
SSD_HEADDIM = 64
SSD_HEADS = 32
SSD_GROUPS = 4
SSD_HPG = 8
SSD_STATE = 128
SSD_CONV = 5
SSD_CHUNK = 128
XBC_DIM = D_INNER + 2 * SSD_GROUPS * SSD_STATE
SC_CONV = 3
N_EXPERTS = 32
TOP_K = 4
D_EXPERT = 1024
SWIGLU_LIMIT = 7.0
SWIGLU_ALPHA = 1.702
LN_EPS = 1e-5
RMS_EPS = 1e-5
DEEPNORM_ALPHA = 2.0 ** 0.25
LOG2E = 1.4426950408889634

LANES = 128
MAIN_COLS = 10240
COL_Z = 0
COL_XBC = 2048
COL_SCB = 5120
COL_SCC = 6144
COL_SCV = 7168
COL_GSSD = 8192
COL_GSC = 9216

VMEM_LIMIT = 56 * 1024 * 1024

TM_PROJ = 1024
TN_PROJ = 2048
TB_CONV = 256
TS_SSD = 256
TM_POST = 256
TM_FFN = 512
RUN_ALIGN = 8
CHUNKS_PER_TILE = TM_FFN // RUN_ALIGN
LOCAL_ROWS = TM_POST * TOP_K + N_EXPERTS * RUN_ALIGN
LOCAL_CHUNKS = LOCAL_ROWS // RUN_ALIGN


def _cparams(sem):
    return pltpu.CompilerParams(dimension_semantics=sem, vmem_limit_bytes=VMEM_LIMIT)


def _split3(v):
    hi = v.astype(BF16)
    r1 = v - hi.astype(F32)
    mid = r1.astype(BF16)
    lo = (r1 - mid.astype(F32)).astype(BF16)
    return hi, mid, lo


def _dot(a, b):
    return jnp.dot(a, b, preferred_element_type=F32)


def _dot_exact_lhs(m_bf16, v_f32):
    hi, mid, lo = _split3(v_f32)
    return _dot(m_bf16, hi) + _dot(m_bf16, mid) + _dot(m_bf16, lo)


def _dot_x3(a_f32, b_f32):
    a_hi = a_f32.astype(BF16)
    a_lo = (a_f32 - a_hi.astype(F32)).astype(BF16)
    b_hi = b_f32.astype(BF16)
    b_lo = (b_f32 - b_hi.astype(F32)).astype(BF16)
    return _dot(a_hi, b_hi) + _dot(a_lo, b_hi) + _dot(a_hi, b_lo)


def _silu(v):
    return v * jax.nn.sigmoid(v)


def _softplus(v):
    return jnp.maximum(v, 0.0) + jnp.log(1.0 + jnp.exp(-jnp.abs(v)))


def _mod_kernel(c_ref, w_ref, b_ref, o_ref):
    o_ref[...] = _dot_x3(_silu(c_ref[...]), w_ref[...]) + b_ref[...]


def _mod_call(cvec, w_ada, b_ada):
    n = w_ada.shape[1]
    tn = 1536
    return pl.pallas_call(
        _mod_kernel,
        out_shape=jax.ShapeDtypeStruct((cvec.shape[0], n), F32),
        grid=(n // tn,),
        in_specs=[
            pl.BlockSpec(cvec.shape, lambda j: (0, 0)),
            pl.BlockSpec((D_MODEL, tn), lambda j: (0, j)),
            pl.BlockSpec((1, tn), lambda j: (0, j)),
        ],
        out_specs=pl.BlockSpec((cvec.shape[0], tn), lambda j: (0, j)),
        compiler_params=_cparams(("arbitrary",)),
        name="mod",
    )(cvec, w_ada, b_ada.reshape(1, n))


def _mod_row(block, rows_per_block, n_ctx_tokens, lat_len):
    tok = block * rows_per_block
    return jnp.where(tok < n_ctx_tokens, 0, 1 + (tok - n_ctx_tokens) // lat_len)


def _ctx_lat_specs(block_shape, n_ctx_blocks, lead=()):
    tail = (0,) * (len(block_shape) - len(lead) - 1)
    ctx = pl.BlockSpec(block_shape, lambda i, *_: lead + (jnp.minimum(i, n_ctx_blocks - 1),) + tail)
    lat = pl.BlockSpec(block_shape, lambda i, *_: lead + (jnp.maximum(i - n_ctx_blocks, 0),) + tail)
    return ctx, lat


def _shift_rows(x, off, pos, row_len):
    if off == 0:
        return x
    n = x.shape[0]
    rolled = pltpu.roll(x, (-off) % n, 0)
    ok = (pos + off >= 0) & (pos + off < row_len)
    return jnp.where(ok, rolled, 0.0)


def _row_pos(n, row_len):
    t = lax.broadcasted_iota(jnp.int32, (n, 1), 0)
    return jnp.bitwise_and(t, row_len - 1)


def _inproj_kernel(xc_ref, xl_ref, mod_ref, w_ref, wdt_ref, o_ref, dt_ref, h_scr, *, n_ctx, lat_len):
    i = pl.program_id(0)
    j = pl.program_id(1)

    @pl.when(j == 0)
    def _():
        r = _mod_row(i, TM_PROJ, n_ctx, lat_len)
        shift = mod_ref[pl.ds(r, 1), 0:D_MODEL]
        scale = mod_ref[pl.ds(r, 1), D_MODEL:2 * D_MODEL]
        x = jnp.where(i * TM_PROJ < n_ctx, xc_ref[...], xl_ref[...])
        h = (x * (1.0 + scale) + shift).astype(BF16)
        h_scr[...] = h
        dt_ref[...] = _dot(h, wdt_ref[...])

    o_ref[...] = _dot(h_scr[...], w_ref[...]).astype(BF16)


def _inproj_call(x_ctx, x_lat, mod, w_main, w_dt, lat_len):
    n_ctx = x_ctx.shape[0]
    t = n_ctx + x_lat.shape[0]
    kern = functools.partial(_inproj_kernel, n_ctx=n_ctx, lat_len=lat_len)
    return pl.pallas_call(
        kern,
        out_shape=(jax.ShapeDtypeStruct((t, MAIN_COLS), BF16),
                   jax.ShapeDtypeStruct((t, LANES), F32)),
        grid=(t // TM_PROJ, MAIN_COLS // TN_PROJ),
        in_specs=[
            *_ctx_lat_specs((TM_PROJ, D_MODEL), n_ctx // TM_PROJ),
            pl.BlockSpec(mod.shape, lambda i, j: (0, 0)),
            pl.BlockSpec((D_MODEL, TN_PROJ), lambda i, j: (0, j)),
            pl.BlockSpec((D_MODEL, LANES), lambda i, j: (0, 0)),
        ],
        out_specs=(pl.BlockSpec((TM_PROJ, TN_PROJ), lambda i, j: (i, j)),
                   pl.BlockSpec((TM_PROJ, LANES), lambda i, j: (i, 0))),
        scratch_shapes=[pltpu.VMEM((TM_PROJ, D_MODEL), BF16)],
        compiler_params=_cparams(("arbitrary", "arbitrary")),
        name="inproj",
    )(x_ctx, x_lat, mod, w_main, w_dt)


def _conv_kernel(x_ref, s_ref, w_ref, b_ref, o_ref):
    x = x_ref[...]
    half = SSD_CONV // 2
    acc = x.astype(F32) * w_ref[half:half + 1, :]
    for k in range(SSD_CONV):
        if k != half:
            tap = k if k < half else k - 1
            acc = acc + _dot(s_ref[0, tap], x) * w_ref[k:k + 1, :]
    o_ref[...] = _silu(acc + b_ref[...]).astype(BF16)


def _shift_matrices(n, row_lens, width):
    t = jnp.arange(n, dtype=jnp.int32)
    half = width // 2
    mats = []
    for row_len in row_lens:
        taps = []
        for off in [o for o in range(-half, half + 1) if o != 0]:
            pos = t % row_len + off
            hit = (t[None, :] == t[:, None] + off) & ((pos >= 0) & (pos < row_len))[:, None]
            taps.append(hit)
        mats.append(jnp.stack(taps))
    return jnp.stack(mats).astype(BF16)


def _conv_call(proj, conv_w, conv_b, n_ctx, ctx_len):
    t = proj.shape[0]
    tc = 1024
    n_ctx_blocks = n_ctx // TB_CONV
    shifts = _shift_matrices(TB_CONV, (ctx_len, GRID_W), SSD_CONV)
    return pl.pallas_call(
        _conv_kernel,
        out_shape=jax.ShapeDtypeStruct((t, XBC_DIM), BF16),
        grid=(t // TB_CONV, XBC_DIM // tc),
        in_specs=[
            pl.BlockSpec((TB_CONV, tc), lambda i, j: (i, COL_XBC // tc + j)),
            pl.BlockSpec((1, SSD_CONV - 1, TB_CONV, TB_CONV),
                         lambda i, j: (jnp.where(i < n_ctx_blocks, 0, 1), 0, 0, 0)),
            pl.BlockSpec((SSD_CONV, tc), lambda i, j: (0, j)),
            pl.BlockSpec((1, tc), lambda i, j: (0, j)),
        ],
        out_specs=pl.BlockSpec((TB_CONV, tc), lambda i, j: (i, j)),
        compiler_params=_cparams(("arbitrary", "arbitrary")),
        name="conv",
    )(proj, shifts, conv_w, conv_b.reshape(1, XBC_DIM))


def _ssd_kernel(*refs, has_h0, want_final):
    xs_ref, b_ref, c_ref, dt_ref = refs[:4]
    rest = list(refs[4:])
    h0_ref = rest.pop(0) if has_h0 else None
    alog_ref, bias_ref, dvec_ref, y_ref = rest[:4]
    hfin_ref = rest[4] if want_final else None
    ht_scr = rest[-1]
    d = pl.program_id(1)
    st = pl.program_id(2)
    q = SSD_CHUNK
    n_chunks = TS_SSD // q
    n_blk = D_INNER // LANES
    is_fwd = d == 0

    @pl.when(st == 0)
    def _():
        if has_h0:
            for j in range(n_blk):
                ht_scr[:, j * LANES:(j + 1) * LANES] = h0_ref[0, 0, j * LANES:(j + 1) * LANES, :].T
        else:
            ht_scr[...] = jnp.zeros_like(ht_scr)

    a2_neg = -jnp.exp(alog_ref[0]) * LOG2E
    bias = bias_ref[0]
    row = lax.broadcasted_iota(jnp.int32, (q, q), 0)
    col = lax.broadcasted_iota(jnp.int32, (q, q), 1)
    keep = (row - col) * jnp.where(is_fwd, 1, -1) >= 0
    tri = keep.astype(BF16)
    lane_lo = lax.broadcasted_iota(jnp.int32, (q, LANES), 1) < SSD_HEADDIM
    lane_lo1 = lane_lo[0:1, :]
    d_on = jnp.where(is_fwd, 1.0, 0.0)
    neg_inf = jnp.float32(-jnp.inf)

    def chunk_body(ci, carry):
        c = jnp.where(is_fwd, ci, n_chunks - 1 - ci)
        c0 = pl.multiple_of(c * q, q)
        dt = _softplus(dt_ref[pl.ds(c0, q), :] + bias)
        acs = _dot_exact_lhs(tri, dt * a2_neg)
        acs_t = acs.T
        dt_t = dt.T
        tot_row = jnp.where(is_fwd, acs[q - 1:q, :], acs[0:1, :])
        tot_col = jnp.where(is_fwd, acs_t[:, q - 1:q], acs_t[:, 0:1])
        e_tot = jnp.exp2(tot_row)
        src_t = acs_t - jnp.log2(dt_t)
        dte_t = jnp.exp2(tot_col - acs_t) * dt_t

        for g in range(SSD_GROUPS):
            bg = b_ref[pl.ds(c0, q), g * SSD_STATE:(g + 1) * SSD_STATE]
            cg = c_ref[pl.ds(c0, q), g * SSD_STATE:(g + 1) * SSD_STATE]
            cb = lax.dot_general(cg, bg, (((1,), (1,)), ((), ())),
                                 preferred_element_type=F32)
            cgf = cg.astype(F32)
            bg_t = bg.astype(F32).T
            for p in range(SSD_HPG // 2):
                h0 = g * SSD_HPG + 2 * p
                lo = h0 * SSD_HEADDIM
                xp = xs_ref[pl.ds(c0, q), lo:lo + LANES]
                htp = ht_scr[:, lo:lo + LANES]
                rhs = jnp.concatenate([xp, htp.astype(BF16)], axis=0)
                ys, sts = [], []
                for hh in (h0, h0 + 1):
                    tgt = jnp.broadcast_to(acs[:, hh:hh + 1], (q, q))
                    seg = jnp.where(keep, tgt - src_t[hh:hh + 1, :], neg_inf)
                    w = (cb * jnp.exp2(seg)).astype(BF16)
                    ce = (cgf * jnp.exp2(tgt)).astype(BF16)
                    ys.append(_dot(jnp.concatenate([w, ce], axis=1), rhs))
                    sts.append(_dot((bg_t * dte_t[hh:hh + 1, :]).astype(BF16), xp))
                y = (jnp.where(lane_lo, ys[0], ys[1])
                     + (d_on * dvec_ref[:, lo:lo + LANES]) * xp.astype(F32))
                y_ref[0, pl.ds(c0, q), lo:lo + LANES] = y.astype(BF16)
                dec = jnp.where(lane_lo1, e_tot[:, h0:h0 + 1], e_tot[:, h0 + 1:h0 + 2])
                ht_scr[:, lo:lo + LANES] = htp * dec + jnp.where(lane_lo, sts[0], sts[1])
        return carry

    lax.fori_loop(0, n_chunks, chunk_body, 0)

    if want_final:
        @pl.when(st == pl.num_programs(2) - 1)
        def _():
            for j in range(n_blk):
                hfin_ref[0, 0, j * LANES:(j + 1) * LANES, :] = ht_scr[:, j * LANES:(j + 1) * LANES].T


def _ssd_call(xbc, dt_raw, h0, a_log, dt_bias, dvec, tok0, n_seq, seq_len, want_final):
    n_steps = seq_len // TS_SSD
    blk0 = tok0 // TS_SSD
    bc_w = SSD_GROUPS * SSD_STATE

    def tok_blk(b, d, s):
        return blk0 + b * n_steps + jnp.where(d == 0, s, n_steps - 1 - s)

    state_spec = pl.BlockSpec((1, 1, D_INNER, SSD_STATE), lambda b, d, s: (b, d, 0, 0))
    in_specs = [
        pl.BlockSpec((TS_SSD, D_INNER), lambda b, d, s: (tok_blk(b, d, s), 0)),
        pl.BlockSpec((TS_SSD, bc_w), lambda b, d, s: (tok_blk(b, d, s), D_INNER // bc_w)),
        pl.BlockSpec((TS_SSD, bc_w), lambda b, d, s: (tok_blk(b, d, s), D_INNER // bc_w + 1)),
        pl.BlockSpec((TS_SSD, LANES), lambda b, d, s: (tok_blk(b, d, s), 0)),
    ]
    args = [xbc, xbc, xbc, dt_raw]
    if h0 is not None:
        in_specs.append(state_spec)
        args.append(h0)
    in_specs += [
        pl.BlockSpec((1, 1, LANES), lambda b, d, s: (d, 0, 0)),
        pl.BlockSpec((1, 1, LANES), lambda b, d, s: (d, 0, 0)),
        pl.BlockSpec((1, D_INNER), lambda b, d, s: (0, 0)),
    ]
    args += [a_log, dt_bias, dvec]
    out_shape = [jax.ShapeDtypeStruct((2, n_seq * seq_len, D_INNER), BF16)]
    out_specs = [pl.BlockSpec((1, TS_SSD, D_INNER),
                              lambda b, d, s: (d, tok_blk(b, d, s) - blk0, 0))]
    if want_final:
        out_shape.append(jax.ShapeDtypeStruct((2, n_seq, D_INNER, SSD_STATE), F32))
        out_specs.append(pl.BlockSpec((1, 1, D_INNER, SSD_STATE), lambda b, d, s: (d, b, 0, 0)))
    kern = functools.partial(_ssd_kernel, has_h0=h0 is not None, want_final=want_final)
    return pl.pallas_call(
        kern,
        out_shape=tuple(out_shape),
        grid=(n_seq, 2, n_steps),
        in_specs=in_specs,
        out_specs=tuple(out_specs),
        scratch_shapes=[pltpu.VMEM((SSD_STATE, D_INNER), F32)],
        compiler_params=_cparams(("arbitrary", "arbitrary", "arbitrary")),
        name="ssd_final" if want_final else "ssd",
    )(*args)


def _post_kernel(xc_ref, xl_ref, mod_ref, z_ref, scb_ref, scc_ref, scv_ref, gssd_ref, gsc_ref,
                 yc_ref, yl_ref, normw_ref, wssd_ref, scw_ref, wsc_ref, wo_ref, g1_ref, b1_ref,
                 wr_ref, br_ref, x1_ref, h2l_ref, ti_ref, tp_ref, cnt_ref,
                 *, n_ctx, ctx_len, lat_len):
    i = pl.program_id(0)
    tm = TM_POST
    r = _mod_row(i, tm, n_ctx, lat_len)
    is_ctx = i * tm < n_ctx

    def mod_vec(k):
        return mod_ref[pl.ds(r, 1), k * D_MODEL:(k + 1) * D_MODEL]

    gate1, shift2, scale2 = mod_vec(2), mod_vec(3), mod_vec(4)

    y_f = jnp.where(is_ctx, yc_ref[0], yl_ref[0]).astype(F32)
    y_b = jnp.where(is_ctx, yc_ref[1], yl_ref[1]).astype(F32)
    y = (y_f + y_b) * _silu(z_ref[...].astype(F32))
    gw = D_INNER // SSD_GROUPS
    parts = []
    for g in range(SSD_GROUPS):
        yg = y[:, g * gw:(g + 1) * gw]
        ms = jnp.mean(yg * yg, axis=-1, keepdims=True)
        parts.append(yg * lax.rsqrt(ms + RMS_EPS))
    yn = (jnp.concatenate(parts, axis=-1) * normw_ref[...]).astype(BF16)
    y_ssd = _dot(yn, wssd_ref[...])

    row_len = jnp.where(i * tm < n_ctx, ctx_len, GRID_W)
    pos = _row_pos(tm, row_len)
    u = scc_ref[...].astype(F32) * scv_ref[...].astype(F32)
    half = SC_CONV // 2
    cv = u * scw_ref[half:half + 1, :]
    for k in range(SC_CONV):
        if k != half:
            cv = cv + _shift_rows(u, k - half, pos, row_len) * scw_ref[k:k + 1, :]
    y_sc = _dot((scb_ref[...].astype(F32) * cv).astype(BF16), wsc_ref[...])

    mixed = (jax.nn.sigmoid(gssd_ref[...].astype(F32)) * y_ssd
             + jax.nn.sigmoid(gsc_ref[...].astype(F32)) * y_sc)
    o = _dot(mixed.astype(BF16), wo_ref[...])

    x = jnp.where(is_ctx, xc_ref[...], xl_ref[...])
    res = DEEPNORM_ALPHA * x + gate1 * o
    mu = jnp.mean(res, axis=-1, keepdims=True)
    cen = res - mu
    var = jnp.mean(cen * cen, axis=-1, keepdims=True)
    x1 = cen * lax.rsqrt(var + LN_EPS) * g1_ref[...] + b1_ref[...]
    x1_ref[...] = x1
    h2 = x1 * (1.0 + scale2) + shift2

    logits = _dot_x3(h2, wr_ref[...]) + br_ref[...]
    lane = lax.broadcasted_iota(jnp.int32, (tm, LANES), 1)
    lane_f = lane.astype(F32)
    neg = jnp.float32(-jnp.inf)
    work = jnp.where(lane < N_EXPERTS, logits, neg)
    vals, idxs, hits = [], [], []
    for _ in range(TOP_K):
        m = jnp.max(work, axis=-1, keepdims=True)
        idx = jnp.min(jnp.where(work == m, lane_f, float(LANES)), axis=-1, keepdims=True)
        hit = lane_f == idx
        vals.append(m)
        idxs.append(idx)
        hits.append(hit)
        work = jnp.where(hit, neg, work)
    es = [jnp.exp(v - vals[0]) for v in vals]
    denom = es[0] + es[1] + es[2] + es[3]

    chosen = jnp.where(hits[0] | hits[1] | hits[2] | hits[3], 1.0, 0.0)
    cnt = jnp.sum(chosen, axis=0, keepdims=True)
    cnt_ref[0] = cnt.astype(jnp.int32)
    cpad = jnp.floor((cnt + (RUN_ALIGN - 1)) * (1.0 / RUN_ALIGN)) * RUN_ALIGN
    er = lax.broadcasted_iota(jnp.int32, (LANES, LANES), 0)
    ec = lax.broadcasted_iota(jnp.int32, (LANES, LANES), 1)
    run_off = _dot(jnp.broadcast_to(cpad, (8, LANES)).astype(BF16), (er < ec).astype(BF16))[0:1]
    rr = lax.broadcasted_iota(jnp.int32, (tm, tm), 0)
    cc = lax.broadcasted_iota(jnp.int32, (tm, tm), 1)
    slot = _dot((rr > cc).astype(BF16), chosen.astype(BF16)) + run_off
    lrows = [jnp.sum(jnp.where(hits[k], slot, 0.0), axis=-1, keepdims=True) for k in range(TOP_K)]

    ti = jnp.zeros((tm, LANES), F32)
    tp = jnp.zeros((tm, LANES), F32)
    lmat = jnp.full((tm, LANES), -1.0, F32)
    for k in range(TOP_K):
        ti = jnp.where(lane == k, idxs[k], ti)
        ti = jnp.where(lane == TOP_K + k, lrows[k], ti)
        lmat = jnp.where(lane == k, lrows[k], lmat)
        tp = jnp.where(lane == k, es[k] / denom, tp)
    ti_ref[...] = ti.astype(jnp.int32)
    tp_ref[...] = tp

    lrow_t = jnp.concatenate([lmat[j * LANES:(j + 1) * LANES, :].T for j in range(tm // LANES)],
                             axis=1)
    jrow = lax.broadcasted_iota(jnp.int32, (LOCAL_ROWS, tm), 0).astype(F32)
    sel_t = jnp.where(jrow == lrow_t[0:1, :], 1.0, 0.0)
    for k in range(1, TOP_K):
        sel_t = sel_t + jnp.where(jrow == lrow_t[k:k + 1, :], 1.0, 0.0)
    h2l_ref[...] = _dot(sel_t.astype(BF16), h2.astype(BF16))


def _post_call(x_ctx, x_lat, mod, proj, y_ctx, y_lat, norm_w, w_ssd, sc_w, w_sc, w_o, ln_g, ln_b,
               w_r, b_r, ctx_len, lat_len):
    n_ctx = x_ctx.shape[0]
    t = n_ctx + x_lat.shape[0]
    tm = TM_POST
    kern = functools.partial(_post_kernel, n_ctx=n_ctx, ctx_len=ctx_len, lat_len=lat_len)

    def colblk(off, width):
        return pl.BlockSpec((tm, width), lambda i: (i, off // width))

    def whole(a):
        return pl.BlockSpec(a.shape, lambda i: (0,) * a.ndim, pipeline_mode=pl.Buffered(1))

    return pl.pallas_call(
        kern,
        out_shape=(jax.ShapeDtypeStruct((t, D_MODEL), F32),
                   jax.ShapeDtypeStruct((t // tm * LOCAL_ROWS, D_MODEL), F32),
                   jax.ShapeDtypeStruct((t, LANES), jnp.int32),
                   jax.ShapeDtypeStruct((t, LANES), F32),
                   jax.ShapeDtypeStruct((t // tm, 1, LANES), jnp.int32)),
        grid=(t // tm,),
        in_specs=[
            *_ctx_lat_specs((tm, D_MODEL), n_ctx // tm),
            whole(mod),
            colblk(COL_Z, D_INNER),
            colblk(COL_SCB, D_MODEL), colblk(COL_SCC, D_MODEL), colblk(COL_SCV, D_MODEL),
            colblk(COL_GSSD, D_MODEL), colblk(COL_GSC, D_MODEL),
            *_ctx_lat_specs((2, tm, D_INNER), n_ctx // tm, lead=(0,)),
            whole(norm_w), whole(w_ssd), whole(sc_w), whole(w_sc), whole(w_o),
            whole(ln_g), whole(ln_b), whole(w_r), whole(b_r),
        ],
        out_specs=(pl.BlockSpec((tm, D_MODEL), lambda i: (i, 0)),
                   pl.BlockSpec((LOCAL_ROWS, D_MODEL), lambda i: (i, 0)),
                   pl.BlockSpec((tm, LANES), lambda i: (i, 0)),
                   pl.BlockSpec((tm, LANES), lambda i: (i, 0)),
                   pl.BlockSpec((1, 1, LANES), lambda i: (i, 0, 0))),
        compiler_params=_cparams(("arbitrary",)),
        name="post",
    )(x_ctx, x_lat, mod, proj, proj, proj, proj, proj, proj, y_ctx, y_lat,
      norm_w, w_ssd, sc_w, w_sc, w_o, ln_g, ln_b, w_r, b_r)


def _ffn_kernel(nlive_ref, exp_ref, nv_ref, src_cur_ref, src_nxt_ref, dst_cur_ref,
                h2l_hbm, wgu_ref, bgu_ref, wd_ref, bd_ref,
                yl_hbm, wgu_scr, wd_scr, x_buf, y_buf, sem_x, sem_y):
    s = pl.program_id(0)
    par = s & 1
    n_live = nlive_ref[0]
    nv = nv_ref[s]
    new_expert = (s == 0) | (exp_ref[s] != exp_ref[jnp.maximum(s - 1, 0)])

    def gather(tbl_ref, slot):
        for c in range(CHUNKS_PER_TILE):
            src = pl.multiple_of(tbl_ref[0, 0, c] * RUN_ALIGN, RUN_ALIGN)
            pltpu.make_async_copy(h2l_hbm.at[pl.ds(src, RUN_ALIGN)],
                                  x_buf.at[slot, pl.ds(c * RUN_ALIGN, RUN_ALIGN)],
                                  sem_x.at[slot]).start()

    def out_copy(slot, c, dst):
        return pltpu.make_async_copy(y_buf.at[slot, pl.ds(c * RUN_ALIGN, RUN_ALIGN)],
                                     yl_hbm.at[pl.ds(dst, RUN_ALIGN)], sem_y.at[slot])

    def wait_out(slot, count):
        rows = pl.multiple_of(count * RUN_ALIGN, RUN_ALIGN)
        pltpu.make_async_copy(y_buf.at[slot, pl.ds(0, rows)], yl_hbm.at[pl.ds(0, rows)],
                              sem_y.at[slot]).wait()

    @pl.when(s == 0)
    def _():
        gather(src_cur_ref, 0)

    @pl.when(s < n_live)
    def _():
        pltpu.make_async_copy(h2l_hbm.at[pl.ds(0, TM_FFN)], x_buf.at[par], sem_x.at[par]).wait()

        @pl.when(s + 1 < n_live)
        def _():
            gather(src_nxt_ref, 1 - par)

        @pl.when(new_expert)
        def _():
            wgu_scr[...] = wgu_ref[0].astype(BF16)
            wd_scr[...] = wd_ref[0].astype(BF16)

        gu = _dot(x_buf[par].astype(BF16), wgu_scr[...]) + bgu_ref[0]
        g = jnp.minimum(gu[:, :D_EXPERT], SWIGLU_LIMIT)
        u = jnp.clip(gu[:, D_EXPERT:], -SWIGLU_LIMIT, SWIGLU_LIMIT)
        act = (u + 1.0) * g * jax.nn.sigmoid(SWIGLU_ALPHA * g)
        y = _dot(act.astype(BF16), wd_scr[...]) + bd_ref[0]

        @pl.when(s >= 2)
        def _():
            wait_out(par, nv_ref[jnp.maximum(s - 2, 0)])

        y_buf[par] = y

        def scatter(c, carry):
            dst = pl.multiple_of(dst_cur_ref[0, 0, c] * RUN_ALIGN, RUN_ALIGN)
            out_copy(par, c, dst).start()
            return carry
        lax.fori_loop(0, nv, scatter, 0)

        @pl.when(s == n_live - 1)
        def _():
            wait_out(par, nv)

            @pl.when(s >= 1)
            def _():
                wait_out(1 - par, nv_ref[jnp.maximum(s - 1, 0)])


def _ffn_call(tables, h2l, w_gu, b_gu, w_d, b_d):
    n_live, exp_id, n_valid, src8, dst8 = tables
    n_tiles = src8.shape[0]

    def tbl_spec(index_fn):
        return pl.BlockSpec((1, 1, CHUNKS_PER_TILE), index_fn, memory_space=pltpu.SMEM)

    def per_expert(shape):
        return pl.BlockSpec((1,) + shape, lambda s, nl, ei, *_: (ei[s], 0, 0))

    grid_spec = pltpu.PrefetchScalarGridSpec(
        num_scalar_prefetch=3,
        grid=(n_tiles,),
        in_specs=[
            tbl_spec(lambda s, *_: (s, 0, 0)),
            tbl_spec(lambda s, *_: (jnp.minimum(s + 1, n_tiles - 1), 0, 0)),
            tbl_spec(lambda s, *_: (s, 0, 0)),
            pl.BlockSpec(memory_space=pl.ANY),
            per_expert((D_MODEL, 2 * D_EXPERT)), per_expert((1, 2 * D_EXPERT)),
            per_expert((D_EXPERT, D_MODEL)), per_expert((1, D_MODEL)),
        ],
        out_specs=pl.BlockSpec(memory_space=pl.ANY),
        scratch_shapes=[pltpu.VMEM((D_MODEL, 2 * D_EXPERT), BF16),
                        pltpu.VMEM((D_EXPERT, D_MODEL), BF16),
                        pltpu.VMEM((2, TM_FFN, D_MODEL), F32),
                        pltpu.VMEM((2, TM_FFN, D_MODEL), F32),
                        pltpu.SemaphoreType.DMA((2,)),
                        pltpu.SemaphoreType.DMA((2,))],
    )
    tables_and_chunks = (n_live, exp_id, n_valid, src8, src8, dst8)
    return pl.pallas_call(
        _ffn_kernel,
        out_shape=jax.ShapeDtypeStruct(h2l.shape, F32),
        grid_spec=grid_spec,
        input_output_aliases={len(tables_and_chunks): 0},
        compiler_params=pltpu.CompilerParams(dimension_semantics=("arbitrary",),
                                             vmem_limit_bytes=VMEM_LIMIT,
                                             has_side_effects=True),
        name="ffn",
    )(*tables_and_chunks, h2l, w_gu, b_gu, w_d, b_d)


def _combine_kernel(x1_ref, ti_ref, tp_ref, cnt_ref, mod_ref, g2_ref, b2_ref, yl_ref,
                    oc_ref, ol_ref, *, n_ctx, lat_len):
    i = pl.program_id(0)
    tb = TM_POST

    jl = lax.broadcasted_iota(jnp.int32, (tb, LOCAL_ROWS), 1)
    ti = ti_ref[...]
    tp = tp_ref[...]
    sel = jnp.where(jl == ti[:, TOP_K:TOP_K + 1], tp[:, 0:1], 0.0)
    for k in range(1, TOP_K):
        sel = sel + jnp.where(jl == ti[:, TOP_K + k:TOP_K + k + 1], tp[:, k:k + 1], 0.0)
    cnt = cnt_ref[0].astype(F32)
    cpad = jnp.floor((cnt + (RUN_ALIGN - 1)) * (1.0 / RUN_ALIGN)) * RUN_ALIGN
    used = jnp.sum(cpad, axis=-1, keepdims=True)
    rowi = lax.broadcasted_iota(jnp.int32, (LOCAL_ROWS, 1), 0).astype(F32)
    yl = jnp.where(rowi < used, yl_ref[...], 0.0)
    moe = _dot(sel.astype(BF16), yl.astype(BF16))

    r = _mod_row(i, tb, n_ctx, lat_len)
    gate2 = mod_ref[pl.ds(r, 1), 5 * D_MODEL:6 * D_MODEL]
    res = DEEPNORM_ALPHA * x1_ref[...] + gate2 * moe
    mu = jnp.mean(res, axis=-1, keepdims=True)
    cen = res - mu
    var = jnp.mean(cen * cen, axis=-1, keepdims=True)
    out = cen * lax.rsqrt(var + LN_EPS) * g2_ref[...] + b2_ref[...]

    @pl.when(i * tb < n_ctx)
    def _():
        oc_ref[...] = out

    @pl.when(i * tb >= n_ctx)
    def _():
        ol_ref[...] = out


def _combine_call(x1, route, tp, cnt, mod, ln_g, ln_b, y_local, n_ctx, lat_len):
    t = x1.shape[0]
    tb = TM_POST
    kern = functools.partial(_combine_kernel, n_ctx=n_ctx, lat_len=lat_len)
    return pl.pallas_call(
        kern,
        out_shape=(jax.ShapeDtypeStruct((n_ctx, D_MODEL), F32),
                   jax.ShapeDtypeStruct((t - n_ctx, D_MODEL), F32)),
        grid=(t // tb,),
        in_specs=[
            pl.BlockSpec((tb, D_MODEL), lambda i: (i, 0)),
            pl.BlockSpec((tb, LANES), lambda i: (i, 0)),
            pl.BlockSpec((tb, LANES), lambda i: (i, 0)),
            pl.BlockSpec((1, 1, LANES), lambda i: (i, 0, 0)),
            pl.BlockSpec(mod.shape, lambda i: (0, 0)),
            pl.BlockSpec((1, D_MODEL), lambda i: (0, 0)),
            pl.BlockSpec((1, D_MODEL), lambda i: (0, 0)),
            pl.BlockSpec((LOCAL_ROWS, D_MODEL), lambda i: (i, 0)),
        ],
        out_specs=_ctx_lat_specs((tb, D_MODEL), n_ctx // tb),
        compiler_params=_cparams(("arbitrary",)),
        name="combine",
    )(x1, route, tp, cnt, mod, ln_g, ln_b, y_local)


def _routing_tables(cnt):
    i32 = jnp.int32
    n_blocks = cnt.shape[0]
    max_rows = n_blocks * (TM_POST * TOP_K + N_EXPERTS * (RUN_ALIGN - 1)) + N_EXPERTS * (TM_FFN - RUN_ALIGN)
    n_tiles = -(-max_rows // TM_FFN)

    cpad = (cnt + (RUN_ALIGN - 1)) // RUN_ALIGN * RUN_ALIGN
    run_loc = jnp.cumsum(cpad, axis=1) - cpad
    used = cpad.sum(axis=0)
    region = (used + (TM_FFN - 1)) // TM_FFN * TM_FFN
    incl = jnp.cumsum(region)
    offs = incl - region
    n_live = incl[-1] // TM_FFN
    run_glob = offs[None, :] + jnp.cumsum(cpad, axis=0) - cpad

    g_row = jnp.arange(n_tiles * CHUNKS_PER_TILE, dtype=i32) * RUN_ALIGN
    rs = run_glob.reshape(-1)
    re = rs + cpad.reshape(-1)
    ls = (jnp.arange(n_blocks, dtype=i32)[:, None] * LOCAL_ROWS + run_loc).reshape(-1)
    inside = (g_row[:, None] >= rs[None, :]) & (g_row[:, None] < re[None, :])
    valid = jnp.any(inside, axis=1)
    local_row = g_row + jnp.sum(jnp.where(inside, (ls - rs)[None, :], 0), axis=1)
    src8 = jnp.where(valid, local_row // RUN_ALIGN, LOCAL_CHUNKS - 1).astype(i32)
    dst8 = jnp.where(valid, local_row // RUN_ALIGN, 0).astype(i32)

    row0 = jnp.minimum(jnp.arange(n_tiles, dtype=i32), n_live - 1) * TM_FFN
    e_of = jnp.sum(incl[None, :] <= row0[:, None], axis=1)
    e_of = jnp.minimum(e_of, N_EXPERTS - 1).astype(i32)
    sel = e_of[:, None] == jnp.arange(N_EXPERTS, dtype=i32)
    run_end = jnp.sum(jnp.where(sel, (offs + used)[None, :], 0), axis=1)
    n_valid = jnp.clip((run_end - row0) // RUN_ALIGN, 0, CHUNKS_PER_TILE)
    n_valid = jnp.where(jnp.arange(n_tiles, dtype=i32) < n_live, n_valid, 0).astype(i32)
    shape = (n_tiles, 1, CHUNKS_PER_TILE)
    return (n_live.reshape(1).astype(i32), e_of, n_valid, src8.reshape(shape), dst8.reshape(shape))


def kernel(x_prompt, x_sample, c, state_ssd_fwd, state_ssd_bwd, c_ctx, w_ada, b_ada, w_in,
           ssd_conv_w, ssd_conv_b, ssd_a_log, ssd_dt_bias, ssd_d, ssd_norm_w, ssd_w_out,
           sc_conv_w, sc_w_out, w_o, ln1_g, ln1_b, w_router, b_router, w_gate_up, b_gate_up,
           w_down, b_down, ln2_g, ln2_b):
    n_ctx_req, ctx_len, _ = x_prompt.shape
    n_lat_req, lat_len, _ = x_sample.shape
    n_ctx = n_ctx_req * ctx_len
    n_lat = n_lat_req * lat_len
    t = n_ctx + n_lat
    assert w_ada.shape[0] == 1, "single trunk layer"
    assert ctx_len % TS_SSD == 0 and lat_len % TS_SSD == 0 and TS_SSD % GRID_W == 0
    assert n_ctx % TM_PROJ == 0 and lat_len % TM_PROJ == 0 and TM_POST % ctx_len == 0
    assert TB_CONV % ctx_len == 0 and ctx_len & (ctx_len - 1) == 0

    x_ctx = x_prompt.reshape(n_ctx, D_MODEL)
    x_lat = x_sample.reshape(n_lat, D_MODEL)

    cvec = jnp.concatenate([c_ctx[None, :], c, jnp.zeros((7 - n_lat_req, D_MODEL), F32)], 0)
    mod = _mod_call(cvec, w_ada[0], b_ada[0])

    w = w_in[0]
    o_dt = D_INNER + XBC_DIM
    w_main = jnp.concatenate([w[:, :o_dt], w[:, o_dt + SSD_HEADS:]], axis=1).astype(BF16)
    w_dt = jnp.pad(w[:, o_dt:o_dt + SSD_HEADS], ((0, 0), (0, LANES - SSD_HEADS))).astype(BF16)
    proj, dt_raw = _inproj_call(x_ctx, x_lat, mod, w_main, w_dt, lat_len)

    xbc = _conv_call(proj, ssd_conv_w[0], ssd_conv_b[0], n_ctx, ctx_len)

    pad_h = ((0, 0), (0, LANES - SSD_HEADS))
    a_log = jnp.pad(ssd_a_log[0], pad_h).reshape(2, 1, LANES)
    dt_bias = jnp.pad(ssd_dt_bias[0], pad_h).reshape(2, 1, LANES)
    dvec = jnp.repeat(ssd_d[0], SSD_HEADDIM).reshape(1, D_INNER)
    h0_lat = jnp.stack([state_ssd_fwd[:, 0].reshape(n_lat_req, D_INNER, SSD_STATE),
                        state_ssd_bwd[:, 0].reshape(n_lat_req, D_INNER, SSD_STATE)], axis=1)
    y_ctx, h_ctx = _ssd_call(xbc, dt_raw, None, a_log, dt_bias, dvec, 0, n_ctx_req, ctx_len, True)
    (y_lat,) = _ssd_call(xbc, dt_raw, h0_lat, a_log, dt_bias, dvec, n_ctx, n_lat_req, lat_len, False)

    w_r = jnp.pad(w_router[0], ((0, 0), (0, LANES - N_EXPERTS)))
    b_r = jnp.pad(b_router[0], (0, LANES - N_EXPERTS)).reshape(1, LANES)
    x1, h2_local, route, top_p, cnt = _post_call(
        x_ctx, x_lat, mod, proj, y_ctx, y_lat, ssd_norm_w[0].reshape(1, D_INNER),
        ssd_w_out[0].astype(BF16), sc_conv_w[0], sc_w_out[0].astype(BF16), w_o[0].astype(BF16),
        ln1_g[0].reshape(1, D_MODEL), ln1_b[0].reshape(1, D_MODEL), w_r, b_r, ctx_len, lat_len)

    tables = _routing_tables(cnt[:, 0, :N_EXPERTS])
    y_local = _ffn_call(tables, h2_local,
                        w_gate_up[0], b_gate_up[0].reshape(N_EXPERTS, 1, 2 * D_EXPERT),
                        w_down[0], b_down[0].reshape(N_EXPERTS, 1, D_MODEL))
    out_ctx, out_lat = _combine_call(x1, route, top_p, cnt, mod, ln2_g[0].reshape(1, D_MODEL),
                                     ln2_b[0].reshape(1, D_MODEL), y_local, n_ctx, lat_len)

    state_shape = (n_ctx_req, 1, SSD_HEADS, SSD_HEADDIM, SSD_STATE)
    return (out_ctx.reshape(n_ctx_req, ctx_len, D_MODEL), out_lat.reshape(n_lat_req, lat_len, D_MODEL),
            h_ctx[0].reshape(state_shape), h_ctx[1].reshape(state_shape))
```

```python
import functools

import jax
import jax.numpy as jnp
from jax import lax
from jax.experimental import pallas as pl
from jax.experimental.pallas import tpu as pltpu

F32 = jnp.float32
BF16 = jnp.bfloat16

D_MODEL = 1024
GRID_W = 64
D_INNER = 2048
SSD_HEADDIM = 64
SSD_HEADS = 32
SSD_GROUPS = 4
SSD_HPG = 8
SSD_STATE = 128
SSD_CONV = 5
SSD_CHUNK = 128
XBC_DIM = D_INNER + 2 * SSD_GROUPS * SSD_STATE
SC_CONV = 3
N_EXPERTS = 32
TOP_K = 4
D_EXPERT = 1024
SWIGLU_LIMIT = 7.0
SWIGLU_ALPHA = 1.702
LN_EPS = 1e-5
RMS_EPS = 1e-5
DEEPNORM_ALPHA = 2.0 ** 0.25
LOG2E = 1.4426950408889634

LANES = 128
MAIN_COLS = 10240
COL_Z = 0
COL_XBC = 2048
COL_SCB = 5120
COL_SCC = 6144
COL_SCV = 7168
COL_GSSD = 8192
COL_GSC = 9216

VMEM_LIMIT = 56 * 1024 * 1024

TM_PROJ = 1024
TN_PROJ = 2048
TB_CONV = 256
TS_SSD = 256
TM_POST = 256
TM_FFN = 512
RUN_ALIGN = 8
CHUNKS_PER_TILE = TM_FFN // RUN_ALIGN
SCATTER_GROUP = 8
LOCAL_ROWS = TM_POST * TOP_K + N_EXPERTS * RUN_ALIGN
LOCAL_CHUNKS = LOCAL_ROWS // RUN_ALIGN


def _cparams(sem):
    return pltpu.CompilerParams(dimension_semantics=sem, vmem_limit_bytes=VMEM_LIMIT)


def _split3(v):
    hi = v.astype(BF16)
    r1 = v - hi.astype(F32)
    mid = r1.astype(BF16)
    lo = (r1 - mid.astype(F32)).astype(BF16)
    return hi, mid, lo


def _dot(a, b):
    return jnp.dot(a, b, preferred_element_type=F32)


def _dot_exact_lhs(m_bf16, v_f32):
    hi, mid, lo = _split3(v_f32)
    return _dot(m_bf16, hi) + _dot(m_bf16, mid) + _dot(m_bf16, lo)


def _dot_x3(a_f32, b_f32):
    a_hi = a_f32.astype(BF16)
    a_lo = (a_f32 - a_hi.astype(F32)).astype(BF16)
    b_hi = b_f32.astype(BF16)
    b_lo = (b_f32 - b_hi.astype(F32)).astype(BF16)
    return _dot(a_hi, b_hi) + _dot(a_lo, b_hi) + _dot(a_hi, b_lo)


def _silu(v):
    return v * jax.nn.sigmoid(v)


def _softplus(v):
    return jnp.maximum(v, 0.0) + jnp.log(1.0 + jnp.exp(-jnp.abs(v)))


def _mod_kernel(c_ref, w_ref, b_ref, o_ref):
    o_ref[...] = _dot_x3(_silu(c_ref[...]), w_ref[...]) + b_ref[...]


def _mod_call(cvec, w_ada, b_ada):
    n = w_ada.shape[1]
    tn = 1536
    return pl.pallas_call(
        _mod_kernel,
        out_shape=jax.ShapeDtypeStruct((cvec.shape[0], n), F32),
        grid=(n // tn,),
        in_specs=[
            pl.BlockSpec(cvec.shape, lambda j: (0, 0)),
            pl.BlockSpec((D_MODEL, tn), lambda j: (0, j)),
            pl.BlockSpec((1, tn), lambda j: (0, j)),
        ],
        out_specs=pl.BlockSpec((cvec.shape[0], tn), lambda j: (0, j)),
        compiler_params=_cparams(("arbitrary",)),
        name="mod",
    )(cvec, w_ada, b_ada.reshape(1, n))


def _mod_row(block, rows_per_block, n_ctx_tokens, lat_len):
    tok = block * rows_per_block
    return jnp.where(tok < n_ctx_tokens, 0, 1 + (tok - n_ctx_tokens) // lat_len)


def _ctx_lat_specs(block_shape, n_ctx_blocks, lead=()):
    tail = (0,) * (len(block_shape) - len(lead) - 1)
    ctx = pl.BlockSpec(block_shape, lambda i, *_: lead + (jnp.minimum(i, n_ctx_blocks - 1),) + tail)
    lat = pl.BlockSpec(block_shape, lambda i, *_: lead + (jnp.maximum(i - n_ctx_blocks, 0),) + tail)
    return ctx, lat


def _shift_rows(x, off, pos, row_len):
    if off == 0:
        return x
    n = x.shape[0]
    rolled = pltpu.roll(x, (-off) % n, 0)
    ok = (pos + off >= 0) & (pos + off < row_len)
    return jnp.where(ok, rolled, 0.0)


def _row_pos(n, row_len):
    t = lax.broadcasted_iota(jnp.int32, (n, 1), 0)
    return jnp.bitwise_and(t, row_len - 1)


def _inproj_kernel(xc_ref, xl_ref, mod_ref, w_ref, wdt_ref, o_ref, dt_ref, h_scr, *, n_ctx, lat_len):
    i = pl.program_id(0)
    j = pl.program_id(1)

    @pl.when(j == 0)
    def _():
        r = _mod_row(i, TM_PROJ, n_ctx, lat_len)
        shift = mod_ref[pl.ds(r, 1), 0:D_MODEL]
        scale = mod_ref[pl.ds(r, 1), D_MODEL:2 * D_MODEL]
        x = jnp.where(i * TM_PROJ < n_ctx, xc_ref[...], xl_ref[...])
        h = (x * (1.0 + scale) + shift).astype(BF16)
        h_scr[...] = h
        dt_ref[...] = _dot(h, wdt_ref[...])

    o_ref[...] = _dot(h_scr[...], w_ref[...]).astype(BF16)


def _inproj_call(x_ctx, x_lat, mod, w_main, w_dt, lat_len):
    n_ctx = x_ctx.shape[0]
    t = n_ctx + x_lat.shape[0]
    kern = functools.partial(_inproj_kernel, n_ctx=n_ctx, lat_len=lat_len)
    return pl.pallas_call(
        kern,
        out_shape=(jax.ShapeDtypeStruct((t, MAIN_COLS), BF16),
                   jax.ShapeDtypeStruct((t, LANES), F32)),
        grid=(t // TM_PROJ, MAIN_COLS // TN_PROJ),
        in_specs=[
            *_ctx_lat_specs((TM_PROJ, D_MODEL), n_ctx // TM_PROJ),
            pl.BlockSpec(mod.shape, lambda i, j: (0, 0)),
            pl.BlockSpec((D_MODEL, TN_PROJ), lambda i, j: (0, j)),
            pl.BlockSpec((D_MODEL, LANES), lambda i, j: (0, 0)),
        ],
        out_specs=(pl.BlockSpec((TM_PROJ, TN_PROJ), lambda i, j: (i, j)),
                   pl.BlockSpec((TM_PROJ, LANES), lambda i, j: (i, 0))),
        scratch_shapes=[pltpu.VMEM((TM_PROJ, D_MODEL), BF16)],
        compiler_params=_cparams(("arbitrary", "arbitrary")),
        name="inproj",
    )(x_ctx, x_lat, mod, w_main, w_dt)


def _conv_kernel(x_ref, s_ref, w_ref, b_ref, o_ref):
    x = x_ref[...]
    half = SSD_CONV // 2
    acc = x.astype(F32) * w_ref[half:half + 1, :]
    for k in range(SSD_CONV):
        if k != half:
            tap = k if k < half else k - 1
            acc = acc + _dot(s_ref[0, tap], x) * w_ref[k:k + 1, :]
    o_ref[...] = _silu(acc + b_ref[...]).astype(BF16)


def _shift_matrices(n, row_lens, width):
    t = jnp.arange(n, dtype=jnp.int32)
    half = width // 2
    mats = []
    for row_len in row_lens:
        taps = []
        for off in [o for o in range(-half, half + 1) if o != 0]:
            pos = t % row_len + off
            hit = (t[None, :] == t[:, None] + off) & ((pos >= 0) & (pos < row_len))[:, None]
            taps.append(hit)
        mats.append(jnp.stack(taps))
    return jnp.stack(mats).astype(BF16)


def _conv_call(proj, conv_w, conv_b, n_ctx, ctx_len):
    t = proj.shape[0]
    tc = 1024
    n_ctx_blocks = n_ctx // TB_CONV
    shifts = _shift_matrices(TB_CONV, (ctx_len, GRID_W), SSD_CONV)
    return pl.pallas_call(
        _conv_kernel,
        out_shape=jax.ShapeDtypeStruct((t, XBC_DIM), BF16),
        grid=(t // TB_CONV, XBC_DIM // tc),
        in_specs=[
            pl.BlockSpec((TB_CONV, tc), lambda i, j: (i, COL_XBC // tc + j)),
            pl.BlockSpec((1, SSD_CONV - 1, TB_CONV, TB_CONV),
                         lambda i, j: (jnp.where(i < n_ctx_blocks, 0, 1), 0, 0, 0)),
            pl.BlockSpec((SSD_CONV, tc), lambda i, j: (0, j)),
            pl.BlockSpec((1, tc), lambda i, j: (0, j)),
        ],
        out_specs=pl.BlockSpec((TB_CONV, tc), lambda i, j: (i, j)),
        compiler_params=_cparams(("arbitrary", "arbitrary")),
        name="conv",
    )(proj, shifts, conv_w, conv_b.reshape(1, XBC_DIM))


def _ssd_kernel(*refs, has_h0, want_final):
    xs_ref, b_ref, c_ref, dt_ref = refs[:4]
    rest = list(refs[4:])
    h0_ref = rest.pop(0) if has_h0 else None
    alog_ref, bias_ref, dvec_ref, y_ref = rest[:4]
    hfin_ref = rest[4] if want_final else None
    ht_scr = rest[-1]
    d = pl.program_id(1)
    st = pl.program_id(2)
    q = SSD_CHUNK
    n_chunks = TS_SSD // q
    n_blk = D_INNER // LANES
    is_fwd = d == 0

    @pl.when(st == 0)
    def _():
        if has_h0:
            for j in range(n_blk):
                ht_scr[:, j * LANES:(j + 1) * LANES] = h0_ref[0, 0, j * LANES:(j + 1) * LANES, :].T
        else:
            ht_scr[...] = jnp.zeros_like(ht_scr)

    a2_neg = -jnp.exp(alog_ref[0]) * LOG2E
    bias = bias_ref[0]
    row = lax.broadcasted_iota(jnp.int32, (q, q), 0)
    col = lax.broadcasted_iota(jnp.int32, (q, q), 1)
    keep = (row - col) * jnp.where(is_fwd, 1, -1) >= 0
    tri = keep.astype(BF16)
    lane_lo = lax.broadcasted_iota(jnp.int32, (q, LANES), 1) < SSD_HEADDIM
    lane_lo1 = lane_lo[0:1, :]
    d_on = jnp.where(is_fwd, 1.0, 0.0)
    neg_inf = jnp.float32(-jnp.inf)

    def chunk_body(ci, carry):
        c = jnp.where(is_fwd, ci, n_chunks - 1 - ci)
        c0 = pl.multiple_of(c * q, q)
        dt = _softplus(dt_ref[pl.ds(c0, q), :] + bias)
        acs = _dot_exact_lhs(tri, dt * a2_neg)
        acs_t = acs.T
        dt_t = dt.T
        tot_row = jnp.where(is_fwd, acs[q - 1:q, :], acs[0:1, :])
        tot_col = jnp.where(is_fwd, acs_t[:, q - 1:q], acs_t[:, 0:1])
        e_tot = jnp.exp2(tot_row)
        src_t = acs_t - jnp.log2(dt_t)
        dte_t = jnp.exp2(tot_col - acs_t) * dt_t

        for g in range(SSD_GROUPS):
            bg = b_ref[pl.ds(c0, q), g * SSD_STATE:(g + 1) * SSD_STATE]
            cg = c_ref[pl.ds(c0, q), g * SSD_STATE:(g + 1) * SSD_STATE]
            cb = lax.dot_general(cg, bg, (((1,), (1,)), ((), ())),
                                 preferred_element_type=F32)
            cgf = cg.astype(F32)
            bg_t = bg.astype(F32).T
            for p in range(SSD_HPG // 2):
                h0 = g * SSD_HPG + 2 * p
                lo = h0 * SSD_HEADDIM
                xp = xs_ref[pl.ds(c0, q), lo:lo + LANES]
                htp = ht_scr[:, lo:lo + LANES]
                rhs = jnp.concatenate([xp, htp.astype(BF16)], axis=0)
                ys, sts = [], []
                for hh in (h0, h0 + 1):
                    tgt = jnp.broadcast_to(acs[:, hh:hh + 1], (q, q))
                    seg = jnp.where(keep, tgt - src_t[hh:hh + 1, :], neg_inf)
                    w = (cb * jnp.exp2(seg)).astype(BF16)
                    ce = (cgf * jnp.exp2(tgt)).astype(BF16)
                    ys.append(_dot(jnp.concatenate([w, ce], axis=1), rhs))
                    sts.append(_dot((bg_t * dte_t[hh:hh + 1, :]).astype(BF16), xp))
                y = (jnp.where(lane_lo, ys[0], ys[1])
                     + (d_on * dvec_ref[:, lo:lo + LANES]) * xp.astype(F32))
                y_ref[0, pl.ds(c0, q), lo:lo + LANES] = y.astype(BF16)
                dec = jnp.where(lane_lo1, e_tot[:, h0:h0 + 1], e_tot[:, h0 + 1:h0 + 2])
                ht_scr[:, lo:lo + LANES] = htp * dec + jnp.where(lane_lo, sts[0], sts[1])
        return carry

    lax.fori_loop(0, n_chunks, chunk_body, 0)

    if want_final:
        @pl.when(st == pl.num_programs(2) - 1)
        def _():
            for j in range(n_blk):
                hfin_ref[0, 0, j * LANES:(j + 1) * LANES, :] = ht_scr[:, j * LANES:(j + 1) * LANES].T


def _ssd_call(xbc, dt_raw, h0, a_log, dt_bias, dvec, tok0, n_seq, seq_len, want_final):
    n_steps = seq_len // TS_SSD
    blk0 = tok0 // TS_SSD
    bc_w = SSD_GROUPS * SSD_STATE

    def tok_blk(b, d, s):
        return blk0 + b * n_steps + jnp.where(d == 0, s, n_steps - 1 - s)

    state_spec = pl.BlockSpec((1, 1, D_INNER, SSD_STATE), lambda b, d, s: (b, d, 0, 0))
    in_specs = [
        pl.BlockSpec((TS_SSD, D_INNER), lambda b, d, s: (tok_blk(b, d, s), 0)),
        pl.BlockSpec((TS_SSD, bc_w), lambda b, d, s: (tok_blk(b, d, s), D_INNER // bc_w)),
        pl.BlockSpec((TS_SSD, bc_w), lambda b, d, s: (tok_blk(b, d, s), D_INNER // bc_w + 1)),
        pl.BlockSpec((TS_SSD, LANES), lambda b, d, s: (tok_blk(b, d, s), 0)),
    ]
    args = [xbc, xbc, xbc, dt_raw]
    if h0 is not None:
        in_specs.append(state_spec)
        args.append(h0)
    in_specs += [
        pl.BlockSpec((1, 1, LANES), lambda b, d, s: (d, 0, 0)),
        pl.BlockSpec((1, 1, LANES), lambda b, d, s: (d, 0, 0)),
        pl.BlockSpec((1, D_INNER), lambda b, d, s: (0, 0)),
    ]
    args += [a_log, dt_bias, dvec]
    out_shape = [jax.ShapeDtypeStruct((2, n_seq * seq_len, D_INNER), BF16)]
    out_specs = [pl.BlockSpec((1, TS_SSD, D_INNER),
                              lambda b, d, s: (d, tok_blk(b, d, s) - blk0, 0))]
    if want_final:
        out_shape.append(jax.ShapeDtypeStruct((2, n_seq, D_INNER, SSD_STATE), F32))
        out_specs.append(pl.BlockSpec((1, 1, D_INNER, SSD_STATE), lambda b, d, s: (d, b, 0, 0)))
    kern = functools.partial(_ssd_kernel, has_h0=h0 is not None, want_final=want_final)
    return pl.pallas_call(
        kern,
        out_shape=tuple(out_shape),
        grid=(n_seq, 2, n_steps),
        in_specs=in_specs,
        out_specs=tuple(out_specs),
        scratch_shapes=[pltpu.VMEM((SSD_STATE, D_INNER), F32)],
        compiler_params=_cparams(("arbitrary", "arbitrary", "arbitrary")),
        name="ssd_final" if want_final else "ssd",
    )(*args)


def _post_kernel(xc_ref, xl_ref, mod_ref, z_ref, scb_ref, scc_ref, scv_ref, gssd_ref, gsc_ref,
                 yc_ref, yl_ref, normw_ref, wssd_ref, scw_ref, wsc_ref, wo_ref, g1_ref, b1_ref,
                 wr_ref, br_ref, x1_ref, h2l_ref, ti_ref, tp_ref, cnt_ref,
                 *, n_ctx, ctx_len, lat_len):
    i = pl.program_id(0)
    tm = TM_POST
    r = _mod_row(i, tm, n_ctx, lat_len)
    is_ctx = i * tm < n_ctx

    def mod_vec(k):
        return mod_ref[pl.ds(r, 1), k * D_MODEL:(k + 1) * D_MODEL]

    gate1, shift2, scale2 = mod_vec(2), mod_vec(3), mod_vec(4)

    y_f = jnp.where(is_ctx, yc_ref[0], yl_ref[0]).astype(F32)
    y_b = jnp.where(is_ctx, yc_ref[1], yl_ref[1]).astype(F32)
    y = (y_f + y_b) * _silu(z_ref[...].astype(F32))
    gw = D_INNER // SSD_GROUPS
    parts = []
    for g in range(SSD_GROUPS):
        yg = y[:, g * gw:(g + 1) * gw]
        ms = jnp.mean(yg * yg, axis=-1, keepdims=True)
        parts.append(yg * lax.rsqrt(ms + RMS_EPS))
    yn = (jnp.concatenate(parts, axis=-1) * normw_ref[...]).astype(BF16)
    y_ssd = _dot(yn, wssd_ref[...])

    row_len = jnp.where(i * tm < n_ctx, ctx_len, GRID_W)
    pos = _row_pos(tm, row_len)
    u = scc_ref[...].astype(F32) * scv_ref[...].astype(F32)
    half = SC_CONV // 2
    cv = u * scw_ref[half:half + 1, :]
    for k in range(SC_CONV):
        if k != half:
            cv = cv + _shift_rows(u, k - half, pos, row_len) * scw_ref[k:k + 1, :]
    y_sc = _dot((scb_ref[...].astype(F32) * cv).astype(BF16), wsc_ref[...])

    mixed = (jax.nn.sigmoid(gssd_ref[...].astype(F32)) * y_ssd
             + jax.nn.sigmoid(gsc_ref[...].astype(F32)) * y_sc)
    o = _dot(mixed.astype(BF16), wo_ref[...])

    x = jnp.where(is_ctx, xc_ref[...], xl_ref[...])
    res = DEEPNORM_ALPHA * x + gate1 * o
    mu = jnp.mean(res, axis=-1, keepdims=True)
    cen = res - mu
    var = jnp.mean(cen * cen, axis=-1, keepdims=True)
    x1 = cen * lax.rsqrt(var + LN_EPS) * g1_ref[...] + b1_ref[...]
    x1_ref[...] = x1
    h2 = x1 * (1.0 + scale2) + shift2

    logits = _dot_x3(h2, wr_ref[...]) + br_ref[...]
    lane = lax.broadcasted_iota(jnp.int32, (tm, LANES), 1)
    lane_f = lane.astype(F32)
    neg = jnp.float32(-jnp.inf)
    work = jnp.where(lane < N_EXPERTS, logits, neg)
    vals, idxs, hits = [], [], []
    for _ in range(TOP_K):
        m = jnp.max(work, axis=-1, keepdims=True)
        idx = jnp.min(jnp.where(work == m, lane_f, float(LANES)), axis=-1, keepdims=True)
        hit = lane_f == idx
        vals.append(m)
        idxs.append(idx)
        hits.append(hit)
        work = jnp.where(hit, neg, work)
    es = [jnp.exp(v - vals[0]) for v in vals]
    denom = es[0] + es[1] + es[2] + es[3]

    chosen = jnp.where(hits[0] | hits[1] | hits[2] | hits[3], 1.0, 0.0)
    cnt = jnp.sum(chosen, axis=0, keepdims=True)
    cnt_ref[0] = cnt.astype(jnp.int32)
    cpad = jnp.floor((cnt + (RUN_ALIGN - 1)) * (1.0 / RUN_ALIGN)) * RUN_ALIGN
    er = lax.broadcasted_iota(jnp.int32, (LANES, LANES), 0)
    ec = lax.broadcasted_iota(jnp.int32, (LANES, LANES), 1)
    run_off = _dot(jnp.broadcast_to(cpad, (8, LANES)).astype(BF16), (er < ec).astype(BF16))[0:1]
    rr = lax.broadcasted_iota(jnp.int32, (tm, tm), 0)
    cc = lax.broadcasted_iota(jnp.int32, (tm, tm), 1)
    slot = _dot((rr > cc).astype(BF16), chosen.astype(BF16)) + run_off
    lrows = [jnp.sum(jnp.where(hits[k], slot, 0.0), axis=-1, keepdims=True) for k in range(TOP_K)]

    ti = jnp.zeros((tm, LANES), F32)
    tp = jnp.zeros((tm, LANES), F32)
    lmat = jnp.full((tm, LANES), -1.0, F32)
    for k in range(TOP_K):
        ti = jnp.where(lane == k, idxs[k], ti)
        ti = jnp.where(lane == TOP_K + k, lrows[k], ti)
        lmat = jnp.where(lane == k, lrows[k], lmat)
        tp = jnp.where(lane == k, es[k] / denom, tp)
    ti_ref[...] = ti.astype(jnp.int32)
    tp_ref[...] = tp

    lrow_t = jnp.concatenate([lmat[j * LANES:(j + 1) * LANES, :].T for j in range(tm // LANES)],
                             axis=1)
    jrow = lax.broadcasted_iota(jnp.int32, (LOCAL_ROWS, tm), 0).astype(F32)
    sel_t = jnp.where(jrow == lrow_t[0:1, :], 1.0, 0.0)
    for k in range(1, TOP_K):
        sel_t = sel_t + jnp.where(jrow == lrow_t[k:k + 1, :], 1.0, 0.0)
    h2l_ref[...] = _dot(sel_t.astype(BF16), h2.astype(BF16))


def _post_call(x_ctx, x_lat, mod, proj, y_ctx, y_lat, norm_w, w_ssd, sc_w, w_sc, w_o, ln_g, ln_b,
               w_r, b_r, ctx_len, lat_len):
    n_ctx = x_ctx.shape[0]
    t = n_ctx + x_lat.shape[0]
    tm = TM_POST
    kern = functools.partial(_post_kernel, n_ctx=n_ctx, ctx_len=ctx_len, lat_len=lat_len)

    def colblk(off, width):
        return pl.BlockSpec((tm, width), lambda i: (i, off // width))

    def whole(a):
        return pl.BlockSpec(a.shape, lambda i: (0,) * a.ndim, pipeline_mode=pl.Buffered(1))

    return pl.pallas_call(
        kern,
        out_shape=(jax.ShapeDtypeStruct((t, D_MODEL), F32),
                   jax.ShapeDtypeStruct((t // tm * LOCAL_ROWS, D_MODEL), F32),
                   jax.ShapeDtypeStruct((t, LANES), jnp.int32),
                   jax.ShapeDtypeStruct((t, LANES), F32),
                   jax.ShapeDtypeStruct((t // tm, 1, LANES), jnp.int32)),
        grid=(t // tm,),
        in_specs=[
            *_ctx_lat_specs((tm, D_MODEL), n_ctx // tm),
            whole(mod),
            colblk(COL_Z, D_INNER),
            colblk(COL_SCB, D_MODEL), colblk(COL_SCC, D_MODEL), colblk(COL_SCV, D_MODEL),
            colblk(COL_GSSD, D_MODEL), colblk(COL_GSC, D_MODEL),
            *_ctx_lat_specs((2, tm, D_INNER), n_ctx // tm, lead=(0,)),
            whole(norm_w), whole(w_ssd), whole(sc_w), whole(w_sc), whole(w_o),
            whole(ln_g), whole(ln_b), whole(w_r), whole(b_r),
        ],
        out_specs=(pl.BlockSpec((tm, D_MODEL), lambda i: (i, 0)),
                   pl.BlockSpec((LOCAL_ROWS, D_MODEL), lambda i: (i, 0)),
                   pl.BlockSpec((tm, LANES), lambda i: (i, 0)),
                   pl.BlockSpec((tm, LANES), lambda i: (i, 0)),
                   pl.BlockSpec((1, 1, LANES), lambda i: (i, 0, 0))),
        compiler_params=_cparams(("arbitrary",)),
        name="post",
    )(x_ctx, x_lat, mod, proj, proj, proj, proj, proj, proj, y_ctx, y_lat,
      norm_w, w_ssd, sc_w, w_sc, w_o, ln_g, ln_b, w_r, b_r)


def _ffn_kernel(nlive_ref, exp_ref, nv_ref, src_cur_ref, src_nxt_ref, dst_cur_ref,
                h2l_hbm, wgu_ref, bgu_ref, wd_ref, bd_ref,
                yl_hbm, wgu_scr, wd_scr, x_buf, y_buf, sem_x, sem_y):
    s = pl.program_id(0)
    par = s & 1
    n_live = nlive_ref[0]
    nv = nv_ref[s]
    new_expert = (s == 0) | (exp_ref[s] != exp_ref[jnp.maximum(s - 1, 0)])

    def gather(tbl_ref, slot):
        for c in range(CHUNKS_PER_TILE):
            src = pl.multiple_of(tbl_ref[0, 0, c] * RUN_ALIGN, RUN_ALIGN)
            pltpu.make_async_copy(h2l_hbm.at[pl.ds(src, RUN_ALIGN)],
                                  x_buf.at[slot, pl.ds(c * RUN_ALIGN, RUN_ALIGN)],
                                  sem_x.at[slot]).start()

    def out_copy(slot, c, dst):
        return pltpu.make_async_copy(y_buf.at[slot, pl.ds(c * RUN_ALIGN, RUN_ALIGN)],
                                     yl_hbm.at[pl.ds(dst, RUN_ALIGN)], sem_y.at[slot])

    def wait_out(slot, count):
        rows = pl.multiple_of(count * RUN_ALIGN, RUN_ALIGN)
        pltpu.make_async_copy(y_buf.at[slot, pl.ds(0, rows)], yl_hbm.at[pl.ds(0, rows)],
                              sem_y.at[slot]).wait()

    @pl.when(s == 0)
    def _():
        gather(src_cur_ref, 0)

    @pl.when(s < n_live)
    def _():
        pltpu.make_async_copy(h2l_hbm.at[pl.ds(0, TM_FFN)], x_buf.at[par], sem_x.at[par]).wait()

        @pl.when(s + 1 < n_live)
        def _():
            gather(src_nxt_ref, 1 - par)

        @pl.when(new_expert)
        def _():
            wgu_scr[...] = wgu_ref[0].astype(BF16)
            wd_scr[...] = wd_ref[0].astype(BF16)

        gu = _dot(x_buf[par].astype(BF16), wgu_scr[...]) + bgu_ref[0]
        g = jnp.minimum(gu[:, :D_EXPERT], SWIGLU_LIMIT)
        u = jnp.clip(gu[:, D_EXPERT:], -SWIGLU_LIMIT, SWIGLU_LIMIT)
        act = (u + 1.0) * g * jax.nn.sigmoid(SWIGLU_ALPHA * g)
        y = _dot(act.astype(BF16), wd_scr[...]) + bd_ref[0]

        @pl.when(s >= 2)
        def _():
            wait_out(par, nv_ref[jnp.maximum(s - 2, 0)])

        y_buf[par] = y

        for grp in range(CHUNKS_PER_TILE // SCATTER_GROUP):
            @pl.when(grp * SCATTER_GROUP < nv)
            def _():
                for c in range(grp * SCATTER_GROUP, (grp + 1) * SCATTER_GROUP):
                    dst = pl.multiple_of(dst_cur_ref[0, 0, c] * RUN_ALIGN, RUN_ALIGN)
                    out_copy(par, c, dst).start()

        @pl.when(s == n_live - 1)
        def _():
            wait_out(par, nv)

            @pl.when(s >= 1)
            def _():
                wait_out(1 - par, nv_ref[jnp.maximum(s - 1, 0)])


def _ffn_call(tables, h2l, w_gu, b_gu, w_d, b_d):
    n_live, exp_id, n_valid, src8, dst8 = tables
    n_tiles = src8.shape[0]

    def tbl_spec(index_fn):
        return pl.BlockSpec((1, 1, CHUNKS_PER_TILE), index_fn, memory_space=pltpu.SMEM)

    def per_expert(shape):
        return pl.BlockSpec((1,) + shape, lambda s, nl, ei, *_: (ei[s], 0, 0))

    grid_spec = pltpu.PrefetchScalarGridSpec(
        num_scalar_prefetch=3,
        grid=(n_tiles,),
        in_specs=[
            tbl_spec(lambda s, *_: (s, 0, 0)),
            tbl_spec(lambda s, *_: (jnp.minimum(s + 1, n_tiles - 1), 0, 0)),
            tbl_spec(lambda s, *_: (s, 0, 0)),
            pl.BlockSpec(memory_space=pl.ANY),
            per_expert((D_MODEL, 2 * D_EXPERT)), per_expert((1, 2 * D_EXPERT)),
            per_expert((D_EXPERT, D_MODEL)), per_expert((1, D_MODEL)),
        ],
        out_specs=pl.BlockSpec(memory_space=pl.ANY),
        scratch_shapes=[pltpu.VMEM((D_MODEL, 2 * D_EXPERT), BF16),
                        pltpu.VMEM((D_EXPERT, D_MODEL), BF16),
                        pltpu.VMEM((2, TM_FFN, D_MODEL), F32),
                        pltpu.VMEM((2, TM_FFN, D_MODEL), F32),
                        pltpu.SemaphoreType.DMA((2,)),
                        pltpu.SemaphoreType.DMA((2,))],
    )
    tables_and_chunks = (n_live, exp_id, n_valid, src8, src8, dst8)
    return pl.pallas_call(
        _ffn_kernel,
        out_shape=jax.ShapeDtypeStruct(h2l.shape, F32),
        grid_spec=grid_spec,
        input_output_aliases={len(tables_and_chunks): 0},
        compiler_params=pltpu.CompilerParams(dimension_semantics=("arbitrary",),
                                             vmem_limit_bytes=VMEM_LIMIT,
                                             has_side_effects=True),
        name="ffn",
    )(*tables_and_chunks, h2l, w_gu, b_gu, w_d, b_d)


def _combine_kernel(x1_ref, ti_ref, tp_ref, cnt_ref, mod_ref, g2_ref, b2_ref, yl_ref,
                    oc_ref, ol_ref, *, n_ctx, lat_len):
    i = pl.program_id(0)
    tb = TM_POST

    jl = lax.broadcasted_iota(jnp.int32, (tb, LOCAL_ROWS), 1)
    ti = ti_ref[...]
    tp = tp_ref[...]
    sel = jnp.where(jl == ti[:, TOP_K:TOP_K + 1], tp[:, 0:1], 0.0)
    for k in range(1, TOP_K):
        sel = sel + jnp.where(jl == ti[:, TOP_K + k:TOP_K + k + 1], tp[:, k:k + 1], 0.0)
    cnt = cnt_ref[0].astype(F32)
    cpad = jnp.floor((cnt + (RUN_ALIGN - 1)) * (1.0 / RUN_ALIGN)) * RUN_ALIGN
    used = jnp.sum(cpad, axis=-1, keepdims=True)
    rowi = lax.broadcasted_iota(jnp.int32, (LOCAL_ROWS, 1), 0).astype(F32)
    yl = jnp.where(rowi < used, yl_ref[...], 0.0)
    moe = _dot(sel.astype(BF16), yl.astype(BF16))

    r = _mod_row(i, tb, n_ctx, lat_len)
    gate2 = mod_ref[pl.ds(r, 1), 5 * D_MODEL:6 * D_MODEL]
    res = DEEPNORM_ALPHA * x1_ref[...] + gate2 * moe
    mu = jnp.mean(res, axis=-1, keepdims=True)
    cen = res - mu
    var = jnp.mean(cen * cen, axis=-1, keepdims=True)
    out = cen * lax.rsqrt(var + LN_EPS) * g2_ref[...] + b2_ref[...]

    @pl.when(i * tb < n_ctx)
    def _():
        oc_ref[...] = out

    @pl.when(i * tb >= n_ctx)
    def _():
        ol_ref[...] = out


def _combine_call(x1, route, tp, cnt, mod, ln_g, ln_b, y_local, n_ctx, lat_len):
    t = x1.shape[0]
    tb = TM_POST
    kern = functools.partial(_combine_kernel, n_ctx=n_ctx, lat_len=lat_len)
    return pl.pallas_call(
        kern,
        out_shape=(jax.ShapeDtypeStruct((n_ctx, D_MODEL), F32),
                   jax.ShapeDtypeStruct((t - n_ctx, D_MODEL), F32)),
        grid=(t // tb,),
        in_specs=[
            pl.BlockSpec((tb, D_MODEL), lambda i: (i, 0)),
            pl.BlockSpec((tb, LANES), lambda i: (i, 0)),
            pl.BlockSpec((tb, LANES), lambda i: (i, 0)),
            pl.BlockSpec((1, 1, LANES), lambda i: (i, 0, 0)),
            pl.BlockSpec(mod.shape, lambda i: (0, 0)),
            pl.BlockSpec((1, D_MODEL), lambda i: (0, 0)),
            pl.BlockSpec((1, D_MODEL), lambda i: (0, 0)),
            pl.BlockSpec((LOCAL_ROWS, D_MODEL), lambda i: (i, 0)),
        ],
        out_specs=_ctx_lat_specs((tb, D_MODEL), n_ctx // tb),
        compiler_params=_cparams(("arbitrary",)),
        name="combine",
    )(x1, route, tp, cnt, mod, ln_g, ln_b, y_local)


def _routing_tables(cnt):
    i32 = jnp.int32
    n_blocks = cnt.shape[0]
    max_rows = n_blocks * (TM_POST * TOP_K + N_EXPERTS * (RUN_ALIGN - 1)) + N_EXPERTS * (TM_FFN - RUN_ALIGN)
    n_tiles = -(-max_rows // TM_FFN)

    cpad = (cnt + (RUN_ALIGN - 1)) // RUN_ALIGN * RUN_ALIGN
    run_loc = jnp.cumsum(cpad, axis=1) - cpad
    used = cpad.sum(axis=0)
    region = (used + (TM_FFN - 1)) // TM_FFN * TM_FFN
    incl = jnp.cumsum(region)
    offs = incl - region
    n_live = incl[-1] // TM_FFN
    run_glob = offs[None, :] + jnp.cumsum(cpad, axis=0) - cpad

    experts = jnp.arange(N_EXPERTS, dtype=i32)

    def per_expert_row(e_idx, table):
        onehot = (e_idx[:, None] == experts).astype(F32)
        return jnp.dot(onehot, table.astype(F32), precision=lax.Precision.HIGHEST).astype(i32)

    g_row = jnp.arange(n_tiles * CHUNKS_PER_TILE, dtype=i32) * RUN_ALIGN
    g_exp = jnp.minimum(jnp.sum(incl[None, :] <= g_row[:, None], axis=1), N_EXPERTS - 1).astype(i32)
    rs = per_expert_row(g_exp, run_glob.T)
    re = rs + per_expert_row(g_exp, cpad.T)
    ls = per_expert_row(g_exp, (jnp.arange(n_blocks, dtype=i32)[:, None] * LOCAL_ROWS + run_loc).T)
    inside = (g_row[:, None] >= rs) & (g_row[:, None] < re)
    valid = jnp.any(inside, axis=1)
    local_row = g_row + jnp.sum(jnp.where(inside, ls - rs, 0), axis=1)
    src8 = jnp.where(valid, local_row // RUN_ALIGN, LOCAL_CHUNKS - 1).astype(i32)

    tile = jnp.arange(n_tiles, dtype=i32)
    row0 = jnp.minimum(tile, n_live - 1) * TM_FFN
    e_of = jnp.minimum(jnp.sum(incl[None, :] <= row0[:, None], axis=1), N_EXPERTS - 1).astype(i32)
    run_end = jnp.sum(jnp.where(e_of[:, None] == experts, (offs + used)[None, :], 0), axis=1)
    n_run = jnp.clip((run_end - row0) // RUN_ALIGN, 0, CHUNKS_PER_TILE)
    n_out = (n_run + (SCATTER_GROUP - 1)) // SCATTER_GROUP * SCATTER_GROUP
    n_out = jnp.where(tile < n_live, n_out, 0).astype(i32)
    c_in_tile = jnp.arange(n_tiles * CHUNKS_PER_TILE, dtype=i32) % CHUNKS_PER_TILE
    filler = jnp.logical_not(valid) & (c_in_tile < jnp.repeat(n_out, CHUNKS_PER_TILE))
    u = jnp.cumsum(filler.astype(i32)) - 1
    spare = (u // 3) * LOCAL_CHUNKS + (LOCAL_CHUNKS - 4) + u % 3
    dst8 = jnp.where(valid, local_row // RUN_ALIGN, jnp.where(filler, spare, 0)).astype(i32)
    shape = (n_tiles, 1, CHUNKS_PER_TILE)
    return (n_live.reshape(1).astype(i32), e_of, n_out, src8.reshape(shape), dst8.reshape(shape))


def kernel(x_prompt, x_sample, c, state_ssd_fwd, state_ssd_bwd, c_ctx, w_ada, b_ada, w_in,
           ssd_conv_w, ssd_conv_b, ssd_a_log, ssd_dt_bias, ssd_d, ssd_norm_w, ssd_w_out,
           sc_conv_w, sc_w_out, w_o, ln1_g, ln1_b, w_router, b_router, w_gate_up, b_gate_up,
           w_down, b_down, ln2_g, ln2_b):
    n_ctx_req, ctx_len, _ = x_prompt.shape
    n_lat_req, lat_len, _ = x_sample.shape
    n_ctx = n_ctx_req * ctx_len
    n_lat = n_lat_req * lat_len
    t = n_ctx + n_lat
    assert w_ada.shape[0] == 1, "single trunk layer"
    assert ctx_len % TS_SSD == 0 and lat_len % TS_SSD == 0 and TS_SSD % GRID_W == 0
    assert n_ctx % TM_PROJ == 0 and lat_len % TM_PROJ == 0 and TM_POST % ctx_len == 0
    assert TB_CONV % ctx_len == 0 and ctx_len & (ctx_len - 1) == 0

    x_ctx = x_prompt.reshape(n_ctx, D_MODEL)
    x_lat = x_sample.reshape(n_lat, D_MODEL)

    cvec = jnp.concatenate([c_ctx[None, :], c, jnp.zeros((7 - n_lat_req, D_MODEL), F32)], 0)
    mod = _mod_call(cvec, w_ada[0], b_ada[0])

    w = w_in[0]
    o_dt = D_INNER + XBC_DIM
    w_main = jnp.concatenate([w[:, :o_dt], w[:, o_dt + SSD_HEADS:]], axis=1).astype(BF16)
    w_dt = jnp.pad(w[:, o_dt:o_dt + SSD_HEADS], ((0, 0), (0, LANES - SSD_HEADS))).astype(BF16)
    proj, dt_raw = _inproj_call(x_ctx, x_lat, mod, w_main, w_dt, lat_len)

    xbc = _conv_call(proj, ssd_conv_w[0], ssd_conv_b[0], n_ctx, ctx_len)

    pad_h = ((0, 0), (0, LANES - SSD_HEADS))
    a_log = jnp.pad(ssd_a_log[0], pad_h).reshape(2, 1, LANES)
    dt_bias = jnp.pad(ssd_dt_bias[0], pad_h).reshape(2, 1, LANES)
    dvec = jnp.repeat(ssd_d[0], SSD_HEADDIM).reshape(1, D_INNER)
    h0_lat = jnp.stack([state_ssd_fwd[:, 0].reshape(n_lat_req, D_INNER, SSD_STATE),
                        state_ssd_bwd[:, 0].reshape(n_lat_req, D_INNER, SSD_STATE)], axis=1)
    y_ctx, h_ctx = _ssd_call(xbc, dt_raw, None, a_log, dt_bias, dvec, 0, n_ctx_req, ctx_len, True)
    (y_lat,) = _ssd_call(xbc, dt_raw, h0_lat, a_log, dt_bias, dvec, n_ctx, n_lat_req, lat_len, False)

    w_r = jnp.pad(w_router[0], ((0, 0), (0, LANES - N_EXPERTS)))
    b_r = jnp.pad(b_router[0], (0, LANES - N_EXPERTS)).reshape(1, LANES)
    x1, h2_local, route, top_p, cnt = _post_call(
        x_ctx, x_lat, mod, proj, y_ctx, y_lat, ssd_norm_w[0].reshape(1, D_INNER),
        ssd_w_out[0].astype(BF16), sc_conv_w[0], sc_w_out[0].astype(BF16), w_o[0].astype(BF16),
        ln1_g[0].reshape(1, D_MODEL), ln1_b[0].reshape(1, D_MODEL), w_r, b_r, ctx_len, lat_len)

    tables = _routing_tables(cnt[:, 0, :N_EXPERTS])
    y_local = _ffn_call(tables, h2_local,
                        w_gate_up[0], b_gate_up[0].reshape(N_EXPERTS, 1, 2 * D_EXPERT),
                        w_down[0], b_down[0].reshape(N_EXPERTS, 1, D_MODEL))
    out_ctx, out_lat = _combine_call(x1, route, top_p, cnt, mod, ln2_g[0].reshape(1, D_MODEL),
                                     ln2_b[0].reshape(1, D_MODEL), y_local, n_ctx, lat_len)

    state_shape = (n_ctx_req, 1, SSD_HEADS, SSD_HEADDIM, SSD_STATE)
    return (out_ctx.reshape(n_ctx_req, ctx_len, D_MODEL), out_lat.reshape(n_lat_req, lat_len, D_MODEL),
            h_ctx[0].reshape(state_shape), h_ctx[1].reshape(state_shape))
```

```python
import functools

import jax
import jax.numpy as jnp
from jax import lax
from jax.experimental import pallas as pl
from jax.experimental.pallas import tpu as pltpu

F32 = jnp.float32
BF16 = jnp.bfloat16

D_MODEL = 1024
GRID_W = 64
D_INNER = 2048
SSD_HEADDIM = 64
SSD_HEADS = 32
SSD_GROUPS = 4
SSD_HPG = 8
SSD_STATE = 128
SSD_CONV = 5
SSD_CHUNK = 128
XBC_DIM = D_INNER + 2 * SSD_GROUPS * SSD_STATE
SC_CONV = 3
N_EXPERTS = 32
TOP_K = 4
D_EXPERT = 1024
SWIGLU_LIMIT = 7.0
SWIGLU_ALPHA = 1.702
LN_EPS = 1e-5
RMS_EPS = 1e-5
DEEPNORM_ALPHA = 2.0 ** 0.25
LOG2E = 1.4426950408889634

LANES = 128
MAIN_COLS = 10240
COL_Z = 0
COL_XBC = 2048
COL_SCB = 5120
COL_SCC = 6144
COL_SCV = 7168
COL_GSSD = 8192
COL_GSC = 9216

VMEM_LIMIT = 56 * 1024 * 1024

TM_PROJ = 1024
TN_PROJ = 2048
TB_CONV = 1024
CONV_SUB = 256
TS_SSD = 512
TM_POST = 256
TM_FFN = 512
RUN_ALIGN = 8
CHUNKS_PER_TILE = TM_FFN // RUN_ALIGN
SCATTER_GROUP = 8
LOCAL_ROWS = TM_POST * TOP_K + N_EXPERTS * RUN_ALIGN
LOCAL_CHUNKS = LOCAL_ROWS // RUN_ALIGN


def _cparams(sem):
    return pltpu.CompilerParams(dimension_semantics=sem, vmem_limit_bytes=VMEM_LIMIT)


def _split3(v):
    hi = v.astype(BF16)
    r1 = v - hi.astype(F32)
    mid = r1.astype(BF16)
    lo = (r1 - mid.astype(F32)).astype(BF16)
    return hi, mid, lo


def _dot(a, b):
    return jnp.dot(a, b, preferred_element_type=F32)


def _dot_exact_lhs(m_bf16, v_f32):
    hi, mid, lo = _split3(v_f32)
    return _dot(m_bf16, hi) + _dot(m_bf16, mid) + _dot(m_bf16, lo)


def _dot_x3(a_f32, b_f32):
    a_hi = a_f32.astype(BF16)
    a_lo = (a_f32 - a_hi.astype(F32)).astype(BF16)
    b_hi = b_f32.astype(BF16)
    b_lo = (b_f32 - b_hi.astype(F32)).astype(BF16)
    return _dot(a_hi, b_hi) + _dot(a_lo, b_hi) + _dot(a_hi, b_lo)


def _silu(v):
    return v * jax.nn.sigmoid(v)


def _softplus(v):
    return jnp.maximum(v, 0.0) + jnp.log(1.0 + jnp.exp(-jnp.abs(v)))


def _mod_kernel(c_ref, w_ref, b_ref, o_ref):
    o_ref[...] = _dot_x3(_silu(c_ref[...]), w_ref[...]) + b_ref[...]


def _mod_call(cvec, w_ada, b_ada):
    n = w_ada.shape[1]
    tn = 1536
    return pl.pallas_call(
        _mod_kernel,
        out_shape=jax.ShapeDtypeStruct((cvec.shape[0], n), F32),
        grid=(n // tn,),
        in_specs=[
            pl.BlockSpec(cvec.shape, lambda j: (0, 0)),
            pl.BlockSpec((D_MODEL, tn), lambda j: (0, j)),
            pl.BlockSpec((1, tn), lambda j: (0, j)),
        ],
        out_specs=pl.BlockSpec((cvec.shape[0], tn), lambda j: (0, j)),
        compiler_params=_cparams(("arbitrary",)),
        name="mod",
    )(cvec, w_ada, b_ada.reshape(1, n))


def _mod_row(block, rows_per_block, n_ctx_tokens, lat_len):
    tok = block * rows_per_block
    return jnp.where(tok < n_ctx_tokens, 0, 1 + (tok - n_ctx_tokens) // lat_len)


def _ctx_lat_specs(block_shape, n_ctx_blocks, lead=()):
    tail = (0,) * (len(block_shape) - len(lead) - 1)
    ctx = pl.BlockSpec(block_shape, lambda i, *_: lead + (jnp.minimum(i, n_ctx_blocks - 1),) + tail)
    lat = pl.BlockSpec(block_shape, lambda i, *_: lead + (jnp.maximum(i - n_ctx_blocks, 0),) + tail)
    return ctx, lat


def _shift_rows(x, off, pos, row_len):
    if off == 0:
        return x
    n = x.shape[0]
    rolled = pltpu.roll(x, (-off) % n, 0)
    ok = (pos + off >= 0) & (pos + off < row_len)
    return jnp.where(ok, rolled, 0.0)


def _row_pos(n, row_len):
    t = lax.broadcasted_iota(jnp.int32, (n, 1), 0)
    return jnp.bitwise_and(t, row_len - 1)


def _inproj_kernel(xc_ref, xl_ref, mod_ref, w_ref, wdt_ref, o_ref, dt_ref, h_scr, *, n_ctx, lat_len):
    i = pl.program_id(0)
    j = pl.program_id(1)

    @pl.when(j == 0)
    def _():
        r = _mod_row(i, TM_PROJ, n_ctx, lat_len)
        shift = mod_ref[pl.ds(r, 1), 0:D_MODEL]
        scale = mod_ref[pl.ds(r, 1), D_MODEL:2 * D_MODEL]
        x = jnp.where(i * TM_PROJ < n_ctx, xc_ref[...], xl_ref[...])
        h = (x * (1.0 + scale) + shift).astype(BF16)
        h_scr[...] = h
        dt_ref[...] = _dot(h, wdt_ref[...])

    o_ref[...] = _dot(h_scr[...], w_ref[...]).astype(BF16)


def _inproj_call(x_ctx, x_lat, mod, w_main, w_dt, lat_len):
    n_ctx = x_ctx.shape[0]
    t = n_ctx + x_lat.shape[0]
    kern = functools.partial(_inproj_kernel, n_ctx=n_ctx, lat_len=lat_len)
    return pl.pallas_call(
        kern,
        out_shape=(jax.ShapeDtypeStruct((t, MAIN_COLS), BF16),
                   jax.ShapeDtypeStruct((t, LANES), F32)),
        grid=(t // TM_PROJ, MAIN_COLS // TN_PROJ),
        in_specs=[
            *_ctx_lat_specs((TM_PROJ, D_MODEL), n_ctx // TM_PROJ),
            pl.BlockSpec(mod.shape, lambda i, j: (0, 0)),
            pl.BlockSpec((D_MODEL, TN_PROJ), lambda i, j: (0, j)),
            pl.BlockSpec((D_MODEL, LANES), lambda i, j: (0, 0)),
        ],
        out_specs=(pl.BlockSpec((TM_PROJ, TN_PROJ), lambda i, j: (i, j)),
                   pl.BlockSpec((TM_PROJ, LANES), lambda i, j: (i, 0))),
        scratch_shapes=[pltpu.VMEM((TM_PROJ, D_MODEL), BF16)],
        compiler_params=_cparams(("arbitrary", "arbitrary")),
        name="inproj",
    )(x_ctx, x_lat, mod, w_main, w_dt)


def _conv_kernel(x_ref, s_ref, w_ref, b_ref, o_ref):
    half = SSD_CONV // 2
    for r in range(0, TB_CONV, CONV_SUB):
        x = x_ref[r:r + CONV_SUB, :]
        acc = x.astype(F32) * w_ref[half:half + 1, :]
        for k in range(SSD_CONV):
            if k != half:
                tap = k if k < half else k - 1
                acc = acc + _dot(s_ref[0, tap], x) * w_ref[k:k + 1, :]
        o_ref[r:r + CONV_SUB, :] = _silu(acc + b_ref[...]).astype(BF16)


def _shift_matrices(n, row_lens, width):
    t = jnp.arange(n, dtype=jnp.int32)
    half = width // 2
    mats = []
    for row_len in row_lens:
        taps = []
        for off in [o for o in range(-half, half + 1) if o != 0]:
            pos = t % row_len + off
            hit = (t[None, :] == t[:, None] + off) & ((pos >= 0) & (pos < row_len))[:, None]
            taps.append(hit)
        mats.append(jnp.stack(taps))
    return jnp.stack(mats).astype(BF16)


def _conv_call(proj, conv_w, conv_b, n_ctx, ctx_len):
    t = proj.shape[0]
    tc = 1024
    n_ctx_blocks = n_ctx // TB_CONV
    shifts = _shift_matrices(CONV_SUB, (ctx_len, GRID_W), SSD_CONV)
    return pl.pallas_call(
        _conv_kernel,
        out_shape=jax.ShapeDtypeStruct((t, XBC_DIM), BF16),
        grid=(t // TB_CONV, XBC_DIM // tc),
        in_specs=[
            pl.BlockSpec((TB_CONV, tc), lambda i, j: (i, COL_XBC // tc + j)),
            pl.BlockSpec((1, SSD_CONV - 1, CONV_SUB, CONV_SUB),
                         lambda i, j: (jnp.where(i < n_ctx_blocks, 0, 1), 0, 0, 0)),
            pl.BlockSpec((SSD_CONV, tc), lambda i, j: (0, j)),
            pl.BlockSpec((1, tc), lambda i, j: (0, j)),
        ],
        out_specs=pl.BlockSpec((TB_CONV, tc), lambda i, j: (i, j)),
        compiler_params=_cparams(("arbitrary", "arbitrary")),
        name="conv",
    )(proj, shifts, conv_w, conv_b.reshape(1, XBC_DIM))


def _ssd_kernel(*refs, has_h0, want_final):
    xs_ref, b_ref, c_ref, dt_ref = refs[:4]
    rest = list(refs[4:])
    h0_ref = rest.pop(0) if has_h0 else None
    alog_ref, bias_ref, dvec_ref, y_ref = rest[:4]
    hfin_ref = rest[4] if want_final else None
    ht_scr = rest[-1]
    d = pl.program_id(1)
    st = pl.program_id(2)
    q = SSD_CHUNK
    n_chunks = xs_ref.shape[0] // q
    n_blk = D_INNER // LANES
    is_fwd = d == 0

    @pl.when(st == 0)
    def _():
        if has_h0:
            for j in range(n_blk):
                ht_scr[:, j * LANES:(j + 1) * LANES] = h0_ref[0, 0, j * LANES:(j + 1) * LANES, :].T
        else:
            ht_scr[...] = jnp.zeros_like(ht_scr)

    a2_neg = -jnp.exp(alog_ref[0]) * LOG2E
    bias = bias_ref[0]
    row = lax.broadcasted_iota(jnp.int32, (q, q), 0)
    col = lax.broadcasted_iota(jnp.int32, (q, q), 1)
    keep = (row - col) * jnp.where(is_fwd, 1, -1) >= 0
    tri = keep.astype(BF16)
    lane_lo = lax.broadcasted_iota(jnp.int32, (q, LANES), 1) < SSD_HEADDIM
    lane_lo1 = lane_lo[0:1, :]
    d_on = jnp.where(is_fwd, 1.0, 0.0)
    neg_inf = jnp.float32(-jnp.inf)

    def chunk_body(ci, carry):
        c = jnp.where(is_fwd, ci, n_chunks - 1 - ci)
        c0 = pl.multiple_of(c * q, q)
        dt = _softplus(dt_ref[pl.ds(c0, q), :] + bias)
        acs = _dot_exact_lhs(tri, dt * a2_neg)
        acs_t = acs.T
        dt_t = dt.T
        tot_row = jnp.where(is_fwd, acs[q - 1:q, :], acs[0:1, :])
        tot_col = jnp.where(is_fwd, acs_t[:, q - 1:q], acs_t[:, 0:1])
        e_tot = jnp.exp2(tot_row)
        src_t = acs_t - jnp.log2(dt_t)
        dte_t = jnp.exp2(tot_col - acs_t) * dt_t

        for g in range(SSD_GROUPS):
            bg = b_ref[pl.ds(c0, q), g * SSD_STATE:(g + 1) * SSD_STATE]
            cg = c_ref[pl.ds(c0, q), g * SSD_STATE:(g + 1) * SSD_STATE]
            cb = lax.dot_general(cg, bg, (((1,), (1,)), ((), ())),
                                 preferred_element_type=F32)
            cgf = cg.astype(F32)
            bg_t = bg.astype(F32).T
            for p in range(SSD_HPG // 2):
                h0 = g * SSD_HPG + 2 * p
                lo = h0 * SSD_HEADDIM
                xp = xs_ref[pl.ds(c0, q), lo:lo + LANES]
                htp = ht_scr[:, lo:lo + LANES]
                rhs = jnp.concatenate([xp, htp.astype(BF16)], axis=0)
                ys, sts = [], []
                for hh in (h0, h0 + 1):
                    tgt = jnp.broadcast_to(acs[:, hh:hh + 1], (q, q))
                    seg = jnp.where(keep, tgt - src_t[hh:hh + 1, :], neg_inf)
                    w = (cb * jnp.exp2(seg)).astype(BF16)
                    ce = (cgf * jnp.exp2(tgt)).astype(BF16)
                    ys.append(_dot(jnp.concatenate([w, ce], axis=1), rhs))
                    sts.append(_dot((bg_t * dte_t[hh:hh + 1, :]).astype(BF16), xp))
                y = (jnp.where(lane_lo, ys[0], ys[1])
                     + (d_on * dvec_ref[:, lo:lo + LANES]) * xp.astype(F32))
                y_ref[0, pl.ds(c0, q), lo:lo + LANES] = y.astype(BF16)
                dec = jnp.where(lane_lo1, e_tot[:, h0:h0 + 1], e_tot[:, h0 + 1:h0 + 2])
                ht_scr[:, lo:lo + LANES] = htp * dec + jnp.where(lane_lo, sts[0], sts[1])
        return carry

    lax.fori_loop(0, n_chunks, chunk_body, 0)

    if want_final:
        @pl.when(st == pl.num_programs(2) - 1)
        def _():
            for j in range(n_blk):
                hfin_ref[0, 0, j * LANES:(j + 1) * LANES, :] = ht_scr[:, j * LANES:(j + 1) * LANES].T


def _ssd_call(xbc, dt_raw, h0, a_log, dt_bias, dvec, tok0, n_seq, seq_len, want_final):
    ts = min(seq_len, TS_SSD)
    assert seq_len % ts == 0 and tok0 % ts == 0 and ts % SSD_CHUNK == 0
    n_steps = seq_len // ts
    blk0 = tok0 // ts
    bc_w = SSD_GROUPS * SSD_STATE

    def tok_blk(b, d, s):
        return blk0 + b * n_steps + jnp.where(d == 0, s, n_steps - 1 - s)

    state_spec = pl.BlockSpec((1, 1, D_INNER, SSD_STATE), lambda b, d, s: (b, d, 0, 0))
    in_specs = [
        pl.BlockSpec((ts, D_INNER), lambda b, d, s: (tok_blk(b, d, s), 0)),
        pl.BlockSpec((ts, bc_w), lambda b, d, s: (tok_blk(b, d, s), D_INNER // bc_w)),
        pl.BlockSpec((ts, bc_w), lambda b, d, s: (tok_blk(b, d, s), D_INNER // bc_w + 1)),
        pl.BlockSpec((ts, LANES), lambda b, d, s: (tok_blk(b, d, s), 0)),
    ]
    args = [xbc, xbc, xbc, dt_raw]
    if h0 is not None:
        in_specs.append(state_spec)
        args.append(h0)
    in_specs += [
        pl.BlockSpec((1, 1, LANES), lambda b, d, s: (d, 0, 0)),
        pl.BlockSpec((1, 1, LANES), lambda b, d, s: (d, 0, 0)),
        pl.BlockSpec((1, D_INNER), lambda b, d, s: (0, 0)),
    ]
    args += [a_log, dt_bias, dvec]
    out_shape = [jax.ShapeDtypeStruct((2, n_seq * seq_len, D_INNER), BF16)]
    out_specs = [pl.BlockSpec((1, ts, D_INNER),
                              lambda b, d, s: (d, tok_blk(b, d, s) - blk0, 0))]
    if want_final:
        out_shape.append(jax.ShapeDtypeStruct((2, n_seq, D_INNER, SSD_STATE), F32))
        out_specs.append(pl.BlockSpec((1, 1, D_INNER, SSD_STATE), lambda b, d, s: (d, b, 0, 0)))
    kern = functools.partial(_ssd_kernel, has_h0=h0 is not None, want_final=want_final)
    return pl.pallas_call(
        kern,
        out_shape=tuple(out_shape),
        grid=(n_seq, 2, n_steps),
        in_specs=in_specs,
        out_specs=tuple(out_specs),
        scratch_shapes=[pltpu.VMEM((SSD_STATE, D_INNER), F32)],
        compiler_params=_cparams(("arbitrary", "arbitrary", "arbitrary")),
        name="ssd_final" if want_final else "ssd",
    )(*args)


def _post_kernel(xc_ref, xl_ref, mod_ref, z_ref, scb_ref, scc_ref, scv_ref, gssd_ref, gsc_ref,
                 yc_ref, yl_ref, normw_ref, wssd_ref, scw_ref, wsc_ref, wo_ref, g1_ref, b1_ref,
                 wr_ref, br_ref, x1_ref, h2l_ref, ti_ref, tp_ref, cnt_ref,
                 *, n_ctx, ctx_len, lat_len):
    i = pl.program_id(0)
    tm = TM_POST
    r = _mod_row(i, tm, n_ctx, lat_len)
    is_ctx = i * tm < n_ctx

    def mod_vec(k):
        return mod_ref[pl.ds(r, 1), k * D_MODEL:(k + 1) * D_MODEL]

    gate1, shift2, scale2 = mod_vec(2), mod_vec(3), mod_vec(4)

    y_f = jnp.where(is_ctx, yc_ref[0], yl_ref[0]).astype(F32)
    y_b = jnp.where(is_ctx, yc_ref[1], yl_ref[1]).astype(F32)
    y = (y_f + y_b) * _silu(z_ref[...].astype(F32))
    gw = D_INNER // SSD_GROUPS
    parts = []
    for g in range(SSD_GROUPS):
        yg = y[:, g * gw:(g + 1) * gw]
        ms = jnp.mean(yg * yg, axis=-1, keepdims=True)
        parts.append(yg * lax.rsqrt(ms + RMS_EPS))
    yn = (jnp.concatenate(parts, axis=-1) * normw_ref[...]).astype(BF16)
    y_ssd = _dot(yn, wssd_ref[...])

    row_len = jnp.where(i * tm < n_ctx, ctx_len, GRID_W)
    pos = _row_pos(tm, row_len)
    u = scc_ref[...].astype(F32) * scv_ref[...].astype(F32)
    half = SC_CONV // 2
    cv = u * scw_ref[half:half + 1, :]
    for k in range(SC_CONV):
        if k != half:
            cv = cv + _shift_rows(u, k - half, pos, row_len) * scw_ref[k:k + 1, :]
    y_sc = _dot((scb_ref[...].astype(F32) * cv).astype(BF16), wsc_ref[...])

    mixed = (jax.nn.sigmoid(gssd_ref[...].astype(F32)) * y_ssd
             + jax.nn.sigmoid(gsc_ref[...].astype(F32)) * y_sc)
    o = _dot(mixed.astype(BF16), wo_ref[...])

    x = jnp.where(is_ctx, xc_ref[...], xl_ref[...])
    res = DEEPNORM_ALPHA * x + gate1 * o
    mu = jnp.mean(res, axis=-1, keepdims=True)
    cen = res - mu
    var = jnp.mean(cen * cen, axis=-1, keepdims=True)
    x1 = cen * lax.rsqrt(var + LN_EPS) * g1_ref[...] + b1_ref[...]
    x1_ref[...] = x1
    h2 = x1 * (1.0 + scale2) + shift2

    logits = _dot_x3(h2, wr_ref[...]) + br_ref[...]
    lane = lax.broadcasted_iota(jnp.int32, (tm, LANES), 1)
    lane_f = lane.astype(F32)
    neg = jnp.float32(-jnp.inf)
    work = jnp.where(lane < N_EXPERTS, logits, neg)
    vals, idxs, hits = [], [], []
    for _ in range(TOP_K):
        m = jnp.max(work, axis=-1, keepdims=True)
        idx = jnp.min(jnp.where(work == m, lane_f, float(LANES)), axis=-1, keepdims=True)
        hit = lane_f == idx
        vals.append(m)
        idxs.append(idx)
        hits.append(hit)
        work = jnp.where(hit, neg, work)
    es = [jnp.exp(v - vals[0]) for v in vals]
    denom = es[0] + es[1] + es[2] + es[3]

    chosen = jnp.where(hits[0] | hits[1] | hits[2] | hits[3], 1.0, 0.0)
    cnt = jnp.sum(chosen, axis=0, keepdims=True)
    cnt_ref[0] = cnt.astype(jnp.int32)
    cpad = jnp.floor((cnt + (RUN_ALIGN - 1)) * (1.0 / RUN_ALIGN)) * RUN_ALIGN
    er = lax.broadcasted_iota(jnp.int32, (LANES, LANES), 0)
    ec = lax.broadcasted_iota(jnp.int32, (LANES, LANES), 1)
    run_off = _dot(jnp.broadcast_to(cpad, (8, LANES)).astype(BF16), (er < ec).astype(BF16))[0:1]
    rr = lax.broadcasted_iota(jnp.int32, (tm, tm), 0)
    cc = lax.broadcasted_iota(jnp.int32, (tm, tm), 1)
    slot = _dot((rr > cc).astype(BF16), chosen.astype(BF16)) + run_off
    lrows = [jnp.sum(jnp.where(hits[k], slot, 0.0), axis=-1, keepdims=True) for k in range(TOP_K)]

    ti = jnp.zeros((tm, LANES), F32)
    tp = jnp.zeros((tm, LANES), F32)
    lmat = jnp.full((tm, LANES), -1.0, F32)
    for k in range(TOP_K):
        ti = jnp.where(lane == k, idxs[k], ti)
        ti = jnp.where(lane == TOP_K + k, lrows[k], ti)
        lmat = jnp.where(lane == k, lrows[k], lmat)
        tp = jnp.where(lane == k, es[k] / denom, tp)
    ti_ref[...] = ti.astype(jnp.int32)
    tp_ref[...] = tp

    lrow_t = jnp.concatenate([lmat[j * LANES:(j + 1) * LANES, :].T for j in range(tm // LANES)],
                             axis=1)
    jrow = lax.broadcasted_iota(jnp.int32, (LOCAL_ROWS, tm), 0).astype(F32)
    sel_t = jnp.where(jrow == lrow_t[0:1, :], 1.0, 0.0)
    for k in range(1, TOP_K):
        sel_t = sel_t + jnp.where(jrow == lrow_t[k:k + 1, :], 1.0, 0.0)
    h2l_ref[...] = _dot(sel_t.astype(BF16), h2.astype(BF16))


def _post_call(x_ctx, x_lat, mod, proj, y_ctx, y_lat, norm_w, w_ssd, sc_w, w_sc, w_o, ln_g, ln_b,
               w_r, b_r, ctx_len, lat_len):
    n_ctx = x_ctx.shape[0]
    t = n_ctx + x_lat.shape[0]
    tm = TM_POST
    kern = functools.partial(_post_kernel, n_ctx=n_ctx, ctx_len=ctx_len, lat_len=lat_len)

    def colblk(off, width):
        return pl.BlockSpec((tm, width), lambda i: (i, off // width))

    def whole(a):
        return pl.BlockSpec(a.shape, lambda i: (0,) * a.ndim, pipeline_mode=pl.Buffered(1))

    return pl.pallas_call(
        kern,
        out_shape=(jax.ShapeDtypeStruct((t, D_MODEL), F32),
                   jax.ShapeDtypeStruct((t // tm * LOCAL_ROWS, D_MODEL), F32),
                   jax.ShapeDtypeStruct((t, LANES), jnp.int32),
                   jax.ShapeDtypeStruct((t, LANES), F32),
                   jax.ShapeDtypeStruct((t // tm, 1, LANES), jnp.int32)),
        grid=(t // tm,),
        in_specs=[
            *_ctx_lat_specs((tm, D_MODEL), n_ctx // tm),
            whole(mod),
            colblk(COL_Z, D_INNER),
            colblk(COL_SCB, D_MODEL), colblk(COL_SCC, D_MODEL), colblk(COL_SCV, D_MODEL),
            colblk(COL_GSSD, D_MODEL), colblk(COL_GSC, D_MODEL),
            *_ctx_lat_specs((2, tm, D_INNER), n_ctx // tm, lead=(0,)),
            whole(norm_w), whole(w_ssd), whole(sc_w), whole(w_sc), whole(w_o),
            whole(ln_g), whole(ln_b), whole(w_r), whole(b_r),
        ],
        out_specs=(pl.BlockSpec((tm, D_MODEL), lambda i: (i, 0)),
                   pl.BlockSpec((LOCAL_ROWS, D_MODEL), lambda i: (i, 0)),
                   pl.BlockSpec((tm, LANES), lambda i: (i, 0)),
                   pl.BlockSpec((tm, LANES), lambda i: (i, 0)),
                   pl.BlockSpec((1, 1, LANES), lambda i: (i, 0, 0))),
        compiler_params=_cparams(("arbitrary",)),
        name="post",
    )(x_ctx, x_lat, mod, proj, proj, proj, proj, proj, proj, y_ctx, y_lat,
      norm_w, w_ssd, sc_w, w_sc, w_o, ln_g, ln_b, w_r, b_r)


def _ffn_kernel(nlive_ref, exp_ref, nv_ref, src_cur_ref, src_nxt_ref, dst_cur_ref,
                h2l_hbm, wgu_ref, bgu_ref, wd_ref, bd_ref,
                yl_hbm, wgu_scr, wd_scr, x_buf, y_buf, sem_x, sem_y):
    s = pl.program_id(0)
    par = s & 1
    n_live = nlive_ref[0]
    nv = nv_ref[s]
    new_expert = (s == 0) | (exp_ref[s] != exp_ref[jnp.maximum(s - 1, 0)])

    def gather(tbl_ref, slot):
        for c in range(CHUNKS_PER_TILE):
            src = pl.multiple_of(tbl_ref[0, 0, c] * RUN_ALIGN, RUN_ALIGN)
            pltpu.make_async_copy(h2l_hbm.at[pl.ds(src, RUN_ALIGN)],
                                  x_buf.at[slot, pl.ds(c * RUN_ALIGN, RUN_ALIGN)],
                                  sem_x.at[slot]).start()

    def out_copy(slot, c, dst):
        return pltpu.make_async_copy(y_buf.at[slot, pl.ds(c * RUN_ALIGN, RUN_ALIGN)],
                                     yl_hbm.at[pl.ds(dst, RUN_ALIGN)], sem_y.at[slot])

    def wait_out(slot, count):
        rows = pl.multiple_of(count * RUN_ALIGN, RUN_ALIGN)
        pltpu.make_async_copy(y_buf.at[slot, pl.ds(0, rows)], yl_hbm.at[pl.ds(0, rows)],
                              sem_y.at[slot]).wait()

    @pl.when(s == 0)
    def _():
        gather(src_cur_ref, 0)

    @pl.when(s < n_live)
    def _():
        pltpu.make_async_copy(h2l_hbm.at[pl.ds(0, TM_FFN)], x_buf.at[par], sem_x.at[par]).wait()

        @pl.when(s + 1 < n_live)
        def _():
            gather(src_nxt_ref, 1 - par)

        @pl.when(new_expert)
        def _():
            wgu_scr[...] = wgu_ref[0].astype(BF16)
            wd_scr[...] = wd_ref[0].astype(BF16)

        gu = _dot(x_buf[par].astype(BF16), wgu_scr[...]) + bgu_ref[0]
        g = jnp.minimum(gu[:, :D_EXPERT], SWIGLU_LIMIT)
        u = jnp.clip(gu[:, D_EXPERT:], -SWIGLU_LIMIT, SWIGLU_LIMIT)
        act = (u + 1.0) * g * jax.nn.sigmoid(SWIGLU_ALPHA * g)
        y = _dot(act.astype(BF16), wd_scr[...]) + bd_ref[0]

        @pl.when(s >= 2)
        def _():
            wait_out(par, nv_ref[jnp.maximum(s - 2, 0)])

        y_buf[par] = y

        for grp in range(CHUNKS_PER_TILE // SCATTER_GROUP):
            @pl.when(grp * SCATTER_GROUP < nv)
            def _():
                for c in range(grp * SCATTER_GROUP, (grp + 1) * SCATTER_GROUP):
                    dst = pl.multiple_of(dst_cur_ref[0, 0, c] * RUN_ALIGN, RUN_ALIGN)
                    out_copy(par, c, dst).start()

        @pl.when(s == n_live - 1)
        def _():
            wait_out(par, nv)

            @pl.when(s >= 1)
            def _():
                wait_out(1 - par, nv_ref[jnp.maximum(s - 1, 0)])


def _ffn_call(tables, h2l, w_gu, b_gu, w_d, b_d):
    n_live, exp_id, n_valid, src8, dst8 = tables
    n_tiles = src8.shape[0]

    def tbl_spec(index_fn):
        return pl.BlockSpec((1, 1, CHUNKS_PER_TILE), index_fn, memory_space=pltpu.SMEM)

    def per_expert(shape):
        return pl.BlockSpec((1,) + shape, lambda s, nl, ei, *_: (ei[s], 0, 0))

    grid_spec = pltpu.PrefetchScalarGridSpec(
        num_scalar_prefetch=3,
        grid=(n_tiles,),
        in_specs=[
            tbl_spec(lambda s, *_: (s, 0, 0)),
            tbl_spec(lambda s, *_: (jnp.minimum(s + 1, n_tiles - 1), 0, 0)),
            tbl_spec(lambda s, *_: (s, 0, 0)),
            pl.BlockSpec(memory_space=pl.ANY),
            per_expert((D_MODEL, 2 * D_EXPERT)), per_expert((1, 2 * D_EXPERT)),
            per_expert((D_EXPERT, D_MODEL)), per_expert((1, D_MODEL)),
        ],
        out_specs=pl.BlockSpec(memory_space=pl.ANY),
        scratch_shapes=[pltpu.VMEM((D_MODEL, 2 * D_EXPERT), BF16),
                        pltpu.VMEM((D_EXPERT, D_MODEL), BF16),
                        pltpu.VMEM((2, TM_FFN, D_MODEL), F32),
                        pltpu.VMEM((2, TM_FFN, D_MODEL), F32),
                        pltpu.SemaphoreType.DMA((2,)),
                        pltpu.SemaphoreType.DMA((2,))],
    )
    tables_and_chunks = (n_live, exp_id, n_valid, src8, src8, dst8)
    return pl.pallas_call(
        _ffn_kernel,
        out_shape=jax.ShapeDtypeStruct(h2l.shape, F32),
        grid_spec=grid_spec,
        input_output_aliases={len(tables_and_chunks): 0},
        compiler_params=pltpu.CompilerParams(dimension_semantics=("arbitrary",),
                                             vmem_limit_bytes=VMEM_LIMIT,
                                             has_side_effects=True),
        name="ffn",
    )(*tables_and_chunks, h2l, w_gu, b_gu, w_d, b_d)


def _combine_kernel(x1_ref, ti_ref, tp_ref, cnt_ref, mod_ref, g2_ref, b2_ref, yl_ref,
                    oc_ref, ol_ref, *, n_ctx, lat_len):
    i = pl.program_id(0)
    tb = TM_POST

    jl = lax.broadcasted_iota(jnp.int32, (tb, LOCAL_ROWS), 1)
    ti = ti_ref[...]
    tp = tp_ref[...]
    sel = jnp.where(jl == ti[:, TOP_K:TOP_K + 1], tp[:, 0:1], 0.0)
    for k in range(1, TOP_K):
        sel = sel + jnp.where(jl == ti[:, TOP_K + k:TOP_K + k + 1], tp[:, k:k + 1], 0.0)
    cnt = cnt_ref[0].astype(F32)
    cpad = jnp.floor((cnt + (RUN_ALIGN - 1)) * (1.0 / RUN_ALIGN)) * RUN_ALIGN
    used = jnp.sum(cpad, axis=-1, keepdims=True)
    rowi = lax.broadcasted_iota(jnp.int32, (LOCAL_ROWS, 1), 0).astype(F32)
    yl = jnp.where(rowi < used, yl_ref[...], 0.0)
    moe = _dot(sel.astype(BF16), yl.astype(BF16))

    r = _mod_row(i, tb, n_ctx, lat_len)
    gate2 = mod_ref[pl.ds(r, 1), 5 * D_MODEL:6 * D_MODEL]
    res = DEEPNORM_ALPHA * x1_ref[...] + gate2 * moe
    mu = jnp.mean(res, axis=-1, keepdims=True)
    cen = res - mu
    var = jnp.mean(cen * cen, axis=-1, keepdims=True)
    out = cen * lax.rsqrt(var + LN_EPS) * g2_ref[...] + b2_ref[...]

    @pl.when(i * tb < n_ctx)
    def _():
        oc_ref[...] = out

    @pl.when(i * tb >= n_ctx)
    def _():
        ol_ref[...] = out


def _combine_call(x1, route, tp, cnt, mod, ln_g, ln_b, y_local, n_ctx, lat_len):
    t = x1.shape[0]
    tb = TM_POST
    kern = functools.partial(_combine_kernel, n_ctx=n_ctx, lat_len=lat_len)
    return pl.pallas_call(
        kern,
        out_shape=(jax.ShapeDtypeStruct((n_ctx, D_MODEL), F32),
                   jax.ShapeDtypeStruct((t - n_ctx, D_MODEL), F32)),
        grid=(t // tb,),
        in_specs=[
            pl.BlockSpec((tb, D_MODEL), lambda i: (i, 0)),
            pl.BlockSpec((tb, LANES), lambda i: (i, 0)),
            pl.BlockSpec((tb, LANES), lambda i: (i, 0)),
            pl.BlockSpec((1, 1, LANES), lambda i: (i, 0, 0)),
            pl.BlockSpec(mod.shape, lambda i: (0, 0)),
            pl.BlockSpec((1, D_MODEL), lambda i: (0, 0)),
            pl.BlockSpec((1, D_MODEL), lambda i: (0, 0)),
            pl.BlockSpec((LOCAL_ROWS, D_MODEL), lambda i: (i, 0)),
        ],
        out_specs=_ctx_lat_specs((tb, D_MODEL), n_ctx // tb),
        compiler_params=_cparams(("arbitrary",)),
        name="combine",
    )(x1, route, tp, cnt, mod, ln_g, ln_b, y_local)


def _routing_tables(cnt):
    i32 = jnp.int32
    n_blocks = cnt.shape[0]
    max_rows = n_blocks * (TM_POST * TOP_K + N_EXPERTS * (RUN_ALIGN - 1)) + N_EXPERTS * (TM_FFN - RUN_ALIGN)
    n_tiles = -(-max_rows // TM_FFN)

    cpad = (cnt + (RUN_ALIGN - 1)) // RUN_ALIGN * RUN_ALIGN
    run_loc = jnp.cumsum(cpad, axis=1) - cpad
    used = cpad.sum(axis=0)
    region = (used + (TM_FFN - 1)) // TM_FFN * TM_FFN
    incl = jnp.cumsum(region)
    offs = incl - region
    n_live = incl[-1] // TM_FFN
    run_glob = offs[None, :] + jnp.cumsum(cpad, axis=0) - cpad

    experts = jnp.arange(N_EXPERTS, dtype=i32)

    def per_expert_row(e_idx, table):
        onehot = (e_idx[:, None] == experts).astype(F32)
        return jnp.dot(onehot, table.astype(F32), precision=lax.Precision.HIGHEST).astype(i32)

    g_row = jnp.arange(n_tiles * CHUNKS_PER_TILE, dtype=i32) * RUN_ALIGN
    g_exp = jnp.minimum(jnp.sum(incl[None, :] <= g_row[:, None], axis=1), N_EXPERTS - 1).astype(i32)
    rs = per_expert_row(g_exp, run_glob.T)
    re = rs + per_expert_row(g_exp, cpad.T)
    ls = per_expert_row(g_exp, (jnp.arange(n_blocks, dtype=i32)[:, None] * LOCAL_ROWS + run_loc).T)
    inside = (g_row[:, None] >= rs) & (g_row[:, None] < re)
    valid = jnp.any(inside, axis=1)
    local_row = g_row + jnp.sum(jnp.where(inside, ls - rs, 0), axis=1)
    src8 = jnp.where(valid, local_row // RUN_ALIGN, LOCAL_CHUNKS - 1).astype(i32)

    tile = jnp.arange(n_tiles, dtype=i32)
    row0 = jnp.minimum(tile, n_live - 1) * TM_FFN
    e_of = jnp.minimum(jnp.sum(incl[None, :] <= row0[:, None], axis=1), N_EXPERTS - 1).astype(i32)
    run_end = jnp.sum(jnp.where(e_of[:, None] == experts, (offs + used)[None, :], 0), axis=1)
    n_run = jnp.clip((run_end - row0) // RUN_ALIGN, 0, CHUNKS_PER_TILE)
    n_out = (n_run + (SCATTER_GROUP - 1)) // SCATTER_GROUP * SCATTER_GROUP
    n_out = jnp.where(tile < n_live, n_out, 0).astype(i32)
    c_in_tile = jnp.arange(n_tiles * CHUNKS_PER_TILE, dtype=i32) % CHUNKS_PER_TILE
    filler = jnp.logical_not(valid) & (c_in_tile < jnp.repeat(n_out, CHUNKS_PER_TILE))
    u = jnp.cumsum(filler.astype(i32)) - 1
    spare = (u // 3) * LOCAL_CHUNKS + (LOCAL_CHUNKS - 4) + u % 3
    dst8 = jnp.where(valid, local_row // RUN_ALIGN, jnp.where(filler, spare, 0)).astype(i32)
    shape = (n_tiles, 1, CHUNKS_PER_TILE)
    return (n_live.reshape(1).astype(i32), e_of, n_out, src8.reshape(shape), dst8.reshape(shape))


def kernel(x_prompt, x_sample, c, state_ssd_fwd, state_ssd_bwd, c_ctx, w_ada, b_ada, w_in,
           ssd_conv_w, ssd_conv_b, ssd_a_log, ssd_dt_bias, ssd_d, ssd_norm_w, ssd_w_out,
           sc_conv_w, sc_w_out, w_o, ln1_g, ln1_b, w_router, b_router, w_gate_up, b_gate_up,
           w_down, b_down, ln2_g, ln2_b):
    n_ctx_req, ctx_len, _ = x_prompt.shape
    n_lat_req, lat_len, _ = x_sample.shape
    n_ctx = n_ctx_req * ctx_len
    n_lat = n_lat_req * lat_len
    t = n_ctx + n_lat
    assert w_ada.shape[0] == 1, "single trunk layer"
    assert ctx_len % SSD_CHUNK == 0 and lat_len % TS_SSD == 0 and SSD_CHUNK % GRID_W == 0
    assert n_ctx % TM_PROJ == 0 and lat_len % TM_PROJ == 0 and TM_POST % ctx_len == 0
    assert CONV_SUB % ctx_len == 0 and CONV_SUB % GRID_W == 0 and n_ctx % TB_CONV == 0
    assert ctx_len & (ctx_len - 1) == 0 and GRID_W & (GRID_W - 1) == 0

    x_ctx = x_prompt.reshape(n_ctx, D_MODEL)
    x_lat = x_sample.reshape(n_lat, D_MODEL)

    cvec = jnp.concatenate([c_ctx[None, :], c, jnp.zeros((7 - n_lat_req, D_MODEL), F32)], 0)
    mod = _mod_call(cvec, w_ada[0], b_ada[0])

    w = w_in[0]
    o_dt = D_INNER + XBC_DIM
    w_main = jnp.concatenate([w[:, :o_dt], w[:, o_dt + SSD_HEADS:]], axis=1).astype(BF16)
    w_dt = jnp.pad(w[:, o_dt:o_dt + SSD_HEADS], ((0, 0), (0, LANES - SSD_HEADS))).astype(BF16)
    proj, dt_raw = _inproj_call(x_ctx, x_lat, mod, w_main, w_dt, lat_len)

    xbc = _conv_call(proj, ssd_conv_w[0], ssd_conv_b[0], n_ctx, ctx_len)

    pad_h = ((0, 0), (0, LANES - SSD_HEADS))
    a_log = jnp.pad(ssd_a_log[0], pad_h).reshape(2, 1, LANES)
    dt_bias = jnp.pad(ssd_dt_bias[0], pad_h).reshape(2, 1, LANES)
    dvec = jnp.repeat(ssd_d[0], SSD_HEADDIM).reshape(1, D_INNER)
    h0_lat = jnp.stack([state_ssd_fwd[:, 0].reshape(n_lat_req, D_INNER, SSD_STATE),
                        state_ssd_bwd[:, 0].reshape(n_lat_req, D_INNER, SSD_STATE)], axis=1)
    y_ctx, h_ctx = _ssd_call(xbc, dt_raw, None, a_log, dt_bias, dvec, 0, n_ctx_req, ctx_len, True)
    (y_lat,) = _ssd_call(xbc, dt_raw, h0_lat, a_log, dt_bias, dvec, n_ctx, n_lat_req, lat_len, False)

    w_r = jnp.pad(w_router[0], ((0, 0), (0, LANES - N_EXPERTS)))
    b_r = jnp.pad(b_router[0], (0, LANES - N_EXPERTS)).reshape(1, LANES)
    x1, h2_local, route, top_p, cnt = _post_call(
        x_ctx, x_lat, mod, proj, y_ctx, y_lat, ssd_norm_w[0].reshape(1, D_INNER),
        ssd_w_out[0].astype(BF16), sc_conv_w[0], sc_w_out[0].astype(BF16), w_o[0].astype(BF16),
        ln1_g[0].reshape(1, D_MODEL), ln1_b[0].reshape(1, D_MODEL), w_r, b_r, ctx_len, lat_len)

    tables = _routing_tables(cnt[:, 0, :N_EXPERTS])
    y_local = _ffn_call(tables, h2_local,
                        w_gate_up[0], b_gate_up[0].reshape(N_EXPERTS, 1, 2 * D_EXPERT),
                        w_down[0], b_down[0].reshape(N_EXPERTS, 1, D_MODEL))
    out_ctx, out_lat = _combine_call(x1, route, top_p, cnt, mod, ln2_g[0].reshape(1, D_MODEL),
                                     ln2_b[0].reshape(1, D_MODEL), y_local, n_ctx, lat_len)

    state_shape = (n_ctx_req, 1, SSD_HEADS, SSD_HEADDIM, SSD_STATE)
    return (out_ctx.reshape(n_ctx_req, ctx_len, D_MODEL), out_lat.reshape(n_lat_req, lat_len, D_MODEL),
            h_ctx[0].reshape(state_shape), h_ctx[1].reshape(state_shape))
```

```python
import functools

import jax
import jax.numpy as jnp
from jax import lax
from jax.experimental import pallas as pl
from jax.experimental.pallas import tpu as pltpu

F32 = jnp.float32
BF16 = jnp.bfloat16

D_MODEL = 1024
GRID_W = 64
D_INNER = 2048
SSD_HEADDIM = 64
SSD_HEADS = 32
SSD_GROUPS = 4
SSD_HPG = 8
SSD_STATE = 128
SSD_CONV = 5
SSD_CHUNK = 128
XBC_DIM = D_INNER + 2 * SSD_GROUPS * SSD_STATE
SC_CONV = 3
N_EXPERTS = 32
TOP_K = 4
D_EXPERT = 1024
SWIGLU_LIMIT = 7.0
SWIGLU_ALPHA = 1.702
LN_EPS = 1e-5
RMS_EPS = 1e-5
DEEPNORM_ALPHA = 2.0 ** 0.25
LOG2E = 1.4426950408889634

LANES = 128
MAIN_COLS = 10240
COL_Z = 0
COL_XBC = 2048
COL_SCB = 5120
COL_SCC = 6144
COL_SCV = 7168
COL_GSSD = 8192
COL_GSC = 9216

VMEM_LIMIT = 56 * 1024 * 1024

TM_PROJ = 1024
TN_PROJ = 2048
TB_CONV = 1024
CONV_SUB = 256
TS_SSD = 512
TM_POST = 256
TM_FFN = 512
RUN_ALIGN = 8
CHUNKS_PER_TILE = TM_FFN // RUN_ALIGN
SCATTER_GROUP = 8
LOCAL_ROWS = TM_POST * TOP_K + N_EXPERTS * RUN_ALIGN
LOCAL_CHUNKS = LOCAL_ROWS // RUN_ALIGN


def _cparams(sem):
    return pltpu.CompilerParams(dimension_semantics=sem, vmem_limit_bytes=VMEM_LIMIT)


def _split3(v):
    hi = v.astype(BF16)
    r1 = v - hi.astype(F32)
    mid = r1.astype(BF16)
    lo = (r1 - mid.astype(F32)).astype(BF16)
    return hi, mid, lo


def _dot(a, b):
    return jnp.dot(a, b, preferred_element_type=F32)


def _dot_exact_lhs(m_bf16, v_f32):
    hi, mid, lo = _split3(v_f32)
    return _dot(m_bf16, hi) + _dot(m_bf16, mid) + _dot(m_bf16, lo)


def _dot_x3(a_f32, b_f32):
    a_hi = a_f32.astype(BF16)
    a_lo = (a_f32 - a_hi.astype(F32)).astype(BF16)
    b_hi = b_f32.astype(BF16)
    b_lo = (b_f32 - b_hi.astype(F32)).astype(BF16)
    return _dot(a_hi, b_hi) + _dot(a_lo, b_hi) + _dot(a_hi, b_lo)


def _silu(v):
    return v * jax.nn.sigmoid(v)


def _softplus(v):
    return jnp.maximum(v, 0.0) + jnp.log(1.0 + jnp.exp(-jnp.abs(v)))


def _mod_kernel(c_ref, w_ref, b_ref, o_ref):
    o_ref[...] = _dot_x3(_silu(c_ref[...]), w_ref[...]) + b_ref[...]


def _mod_call(cvec, w_ada, b_ada):
    n = w_ada.shape[1]
    tn = 1536
    return pl.pallas_call(
        _mod_kernel,
        out_shape=jax.ShapeDtypeStruct((cvec.shape[0], n), F32),
        grid=(n // tn,),
        in_specs=[
            pl.BlockSpec(cvec.shape, lambda j: (0, 0)),
            pl.BlockSpec((D_MODEL, tn), lambda j: (0, j)),
            pl.BlockSpec((1, tn), lambda j: (0, j)),
        ],
        out_specs=pl.BlockSpec((cvec.shape[0], tn), lambda j: (0, j)),
        compiler_params=_cparams(("arbitrary",)),
        name="mod",
    )(cvec, w_ada, b_ada.reshape(1, n))


def _mod_row(block, rows_per_block, n_ctx_tokens, lat_len):
    tok = block * rows_per_block
    return jnp.where(tok < n_ctx_tokens, 0, 1 + (tok - n_ctx_tokens) // lat_len)


def _ctx_lat_specs(block_shape, n_ctx_blocks, lead=()):
    tail = (0,) * (len(block_shape) - len(lead) - 1)
    ctx = pl.BlockSpec(block_shape, lambda i, *_: lead + (jnp.minimum(i, n_ctx_blocks - 1),) + tail)
    lat = pl.BlockSpec(block_shape, lambda i, *_: lead + (jnp.maximum(i - n_ctx_blocks, 0),) + tail)
    return ctx, lat


def _shift_rows(x, off, pos, row_len):
    if off == 0:
        return x
    n = x.shape[0]
    rolled = pltpu.roll(x, (-off) % n, 0)
    ok = (pos + off >= 0) & (pos + off < row_len)
    return jnp.where(ok, rolled, 0.0)


def _row_pos(n, row_len):
    t = lax.broadcasted_iota(jnp.int32, (n, 1), 0)
    return jnp.bitwise_and(t, row_len - 1)


def _inproj_kernel(xc_ref, xl_ref, mod_ref, w_ref, wdt_ref, o_ref, dt_ref, h_scr, *, n_ctx, lat_len):
    i = pl.program_id(0)
    j = pl.program_id(1)

    @pl.when(j == 0)
    def _():
        r = _mod_row(i, TM_PROJ, n_ctx, lat_len)
        shift = mod_ref[pl.ds(r, 1), 0:D_MODEL]
        scale = mod_ref[pl.ds(r, 1), D_MODEL:2 * D_MODEL]
        x = jnp.where(i * TM_PROJ < n_ctx, xc_ref[...], xl_ref[...])
        h = (x * (1.0 + scale) + shift).astype(BF16)
        h_scr[...] = h
        dt_ref[...] = _dot(h, wdt_ref[...])

    o_ref[...] = _dot(h_scr[...], w_ref[...]).astype(BF16)


def _inproj_call(x_ctx, x_lat, mod, w_main, w_dt, lat_len):
    n_ctx = x_ctx.shape[0]
    t = n_ctx + x_lat.shape[0]
    kern = functools.partial(_inproj_kernel, n_ctx=n_ctx, lat_len=lat_len)
    return pl.pallas_call(
        kern,
        out_shape=(jax.ShapeDtypeStruct((t, MAIN_COLS), BF16),
                   jax.ShapeDtypeStruct((t, LANES), F32)),
        grid=(t // TM_PROJ, MAIN_COLS // TN_PROJ),
        in_specs=[
            *_ctx_lat_specs((TM_PROJ, D_MODEL), n_ctx // TM_PROJ),
            pl.BlockSpec(mod.shape, lambda i, j: (0, 0)),
            pl.BlockSpec((D_MODEL, TN_PROJ), lambda i, j: (0, j)),
            pl.BlockSpec((D_MODEL, LANES), lambda i, j: (0, 0)),
        ],
        out_specs=(pl.BlockSpec((TM_PROJ, TN_PROJ), lambda i, j: (i, j)),
                   pl.BlockSpec((TM_PROJ, LANES), lambda i, j: (i, 0))),
        scratch_shapes=[pltpu.VMEM((TM_PROJ, D_MODEL), BF16)],
        compiler_params=_cparams(("arbitrary", "arbitrary")),
        name="inproj",
    )(x_ctx, x_lat, mod, w_main, w_dt)


def _conv_kernel(x_ref, s_ref, w_ref, b_ref, o_ref):
    half = SSD_CONV // 2
    for r in range(0, TB_CONV, CONV_SUB):
        x = x_ref[r:r + CONV_SUB, :]
        acc = x.astype(F32) * w_ref[half:half + 1, :]
        for k in range(SSD_CONV):
            if k != half:
                tap = k if k < half else k - 1
                acc = acc + _dot(s_ref[0, tap], x) * w_ref[k:k + 1, :]
        o_ref[r:r + CONV_SUB, :] = _silu(acc + b_ref[...]).astype(BF16)


def _shift_matrices(n, row_lens, width):
    t = jnp.arange(n, dtype=jnp.int32)
    half = width // 2
    mats = []
    for row_len in row_lens:
        taps = []
        for off in [o for o in range(-half, half + 1) if o != 0]:
            pos = t % row_len + off
            hit = (t[None, :] == t[:, None] + off) & ((pos >= 0) & (pos < row_len))[:, None]
            taps.append(hit)
        mats.append(jnp.stack(taps))
    return jnp.stack(mats).astype(BF16)


def _conv_call(proj, conv_w, conv_b, n_ctx, ctx_len):
    t = proj.shape[0]
    tc = 1024
    n_ctx_blocks = n_ctx // TB_CONV
    shifts = _shift_matrices(CONV_SUB, (ctx_len, GRID_W), SSD_CONV)
    return pl.pallas_call(
        _conv_kernel,
        out_shape=jax.ShapeDtypeStruct((t, XBC_DIM), BF16),
        grid=(t // TB_CONV, XBC_DIM // tc),
        in_specs=[
            pl.BlockSpec((TB_CONV, tc), lambda i, j: (i, COL_XBC // tc + j)),
            pl.BlockSpec((1, SSD_CONV - 1, CONV_SUB, CONV_SUB),
                         lambda i, j: (jnp.where(i < n_ctx_blocks, 0, 1), 0, 0, 0)),
            pl.BlockSpec((SSD_CONV, tc), lambda i, j: (0, j)),
            pl.BlockSpec((1, tc), lambda i, j: (0, j)),
        ],
        out_specs=pl.BlockSpec((TB_CONV, tc), lambda i, j: (i, j)),
        compiler_params=_cparams(("arbitrary", "arbitrary")),
        name="conv",
    )(proj, shifts, conv_w, conv_b.reshape(1, XBC_DIM))


def _ssd_kernel(*refs, has_h0, want_final):
    xs_ref, b_ref, c_ref, dt_ref = refs[:4]
    rest = list(refs[4:])
    h0_ref = rest.pop(0) if has_h0 else None
    alog_ref, bias_ref, dvec_ref, y_ref = rest[:4]
    hfin_ref = rest[4] if want_final else None
    ht_scr = rest[-1]
    d = pl.program_id(1)
    st = pl.program_id(2)
    q = SSD_CHUNK
    n_chunks = xs_ref.shape[0] // q
    n_blk = D_INNER // LANES
    is_fwd = d == 0

    @pl.when(st == 0)
    def _():
        if has_h0:
            for j in range(n_blk):
                ht_scr[:, j * LANES:(j + 1) * LANES] = h0_ref[0, 0, j * LANES:(j + 1) * LANES, :].T
        else:
            ht_scr[...] = jnp.zeros_like(ht_scr)

    a2_neg = -jnp.exp(alog_ref[0]) * LOG2E
    bias = bias_ref[0]
    row = lax.broadcasted_iota(jnp.int32, (q, q), 0)
    col = lax.broadcasted_iota(jnp.int32, (q, q), 1)
    keep = (row - col) * jnp.where(is_fwd, 1, -1) >= 0
    tri = keep.astype(BF16)
    lane_lo = lax.broadcasted_iota(jnp.int32, (q, LANES), 1) < SSD_HEADDIM
    lane_lo1 = lane_lo[0:1, :]
    d_on = jnp.where(is_fwd, 1.0, 0.0)
    neg_inf = jnp.float32(-jnp.inf)

    def chunk_body(ci, carry):
        c = jnp.where(is_fwd, ci, n_chunks - 1 - ci)
        c0 = pl.multiple_of(c * q, q)
        dt = _softplus(dt_ref[pl.ds(c0, q), :] + bias)
        acs = _dot_exact_lhs(tri, dt * a2_neg)
        acs_t = acs.T
        dt_t = dt.T
        tot_row = jnp.where(is_fwd, acs[q - 1:q, :], acs[0:1, :])
        tot_col = jnp.where(is_fwd, acs_t[:, q - 1:q], acs_t[:, 0:1])
        e_tot = jnp.exp2(tot_row)
        src_t = acs_t - jnp.log2(dt_t)
        dte_t = jnp.exp2(tot_col - acs_t) * dt_t

        for g in range(SSD_GROUPS):
            bg = b_ref[pl.ds(c0, q), g * SSD_STATE:(g + 1) * SSD_STATE]
            cg = c_ref[pl.ds(c0, q), g * SSD_STATE:(g + 1) * SSD_STATE]
            cb = lax.dot_general(cg, bg, (((1,), (1,)), ((), ())),
                                 preferred_element_type=F32)
            cgf = cg.astype(F32)
            bg_t = bg.astype(F32).T
            for p in range(SSD_HPG // 2):
                h0 = g * SSD_HPG + 2 * p
                lo = h0 * SSD_HEADDIM
                xp = xs_ref[pl.ds(c0, q), lo:lo + LANES]
                htp = ht_scr[:, lo:lo + LANES]
                rhs = jnp.concatenate([xp, htp.astype(BF16)], axis=0)
                ys, sts = [], []
                for hh in (h0, h0 + 1):
                    tgt = jnp.broadcast_to(acs[:, hh:hh + 1], (q, q))
                    seg = jnp.where(keep, tgt - src_t[hh:hh + 1, :], neg_inf)
                    w = (cb * jnp.exp2(seg)).astype(BF16)
                    ce = (cgf * jnp.exp2(tgt)).astype(BF16)
                    ys.append(_dot(jnp.concatenate([w, ce], axis=1), rhs))
                    sts.append(_dot((bg_t * dte_t[hh:hh + 1, :]).astype(BF16), xp))
                y = (jnp.where(lane_lo, ys[0], ys[1])
                     + (d_on * dvec_ref[:, lo:lo + LANES]) * xp.astype(F32))
                y_ref[0, pl.ds(c0, q), lo:lo + LANES] = y.astype(BF16)
                dec = jnp.where(lane_lo1, e_tot[:, h0:h0 + 1], e_tot[:, h0 + 1:h0 + 2])
                ht_scr[:, lo:lo + LANES] = htp * dec + jnp.where(lane_lo, sts[0], sts[1])
        return carry

    lax.fori_loop(0, n_chunks, chunk_body, 0)

    if want_final:
        @pl.when(st == pl.num_programs(2) - 1)
        def _():
            for j in range(n_blk):
                hfin_ref[0, 0, j * LANES:(j + 1) * LANES, :] = ht_scr[:, j * LANES:(j + 1) * LANES].T


def _ssd_call(xbc, dt_raw, h0, a_log, dt_bias, dvec, tok0, n_seq, seq_len, want_final):
    ts = min(seq_len, TS_SSD)
    assert seq_len % ts == 0 and tok0 % ts == 0 and ts % SSD_CHUNK == 0
    n_steps = seq_len // ts
    blk0 = tok0 // ts
    bc_w = SSD_GROUPS * SSD_STATE

    def tok_blk(b, d, s):
        return blk0 + b * n_steps + jnp.where(d == 0, s, n_steps - 1 - s)

    state_spec = pl.BlockSpec((1, 1, D_INNER, SSD_STATE), lambda b, d, s: (b, d, 0, 0))
    in_specs = [
        pl.BlockSpec((ts, D_INNER), lambda b, d, s: (tok_blk(b, d, s), 0)),
        pl.BlockSpec((ts, bc_w), lambda b, d, s: (tok_blk(b, d, s), D_INNER // bc_w)),
        pl.BlockSpec((ts, bc_w), lambda b, d, s: (tok_blk(b, d, s), D_INNER // bc_w + 1)),
        pl.BlockSpec((ts, LANES), lambda b, d, s: (tok_blk(b, d, s), 0)),
    ]
    args = [xbc, xbc, xbc, dt_raw]
    if h0 is not None:
        in_specs.append(state_spec)
        args.append(h0)
    in_specs += [
        pl.BlockSpec((1, 1, LANES), lambda b, d, s: (d, 0, 0)),
        pl.BlockSpec((1, 1, LANES), lambda b, d, s: (d, 0, 0)),
        pl.BlockSpec((1, D_INNER), lambda b, d, s: (0, 0)),
    ]
    args += [a_log, dt_bias, dvec]
    out_shape = [jax.ShapeDtypeStruct((2, n_seq * seq_len, D_INNER), BF16)]
    out_specs = [pl.BlockSpec((1, ts, D_INNER),
                              lambda b, d, s: (d, tok_blk(b, d, s) - blk0, 0))]
    if want_final:
        out_shape.append(jax.ShapeDtypeStruct((2, n_seq, D_INNER, SSD_STATE), F32))
        out_specs.append(pl.BlockSpec((1, 1, D_INNER, SSD_STATE), lambda b, d, s: (d, b, 0, 0)))
    kern = functools.partial(_ssd_kernel, has_h0=h0 is not None, want_final=want_final)
    return pl.pallas_call(
        kern,
        out_shape=tuple(out_shape),
        grid=(n_seq, 2, n_steps),
        in_specs=in_specs,
        out_specs=tuple(out_specs),
        scratch_shapes=[pltpu.VMEM((SSD_STATE, D_INNER), F32)],
        compiler_params=_cparams(("arbitrary", "arbitrary", "arbitrary")),
        name="ssd_final" if want_final else "ssd",
    )(*args)


def _post_kernel(xc_ref, xl_ref, mod_ref, z_ref, scb_ref, scc_ref, scv_ref, gssd_ref, gsc_ref,
                 yc_ref, yl_ref, normw_ref, wssd_ref, scw_ref, wsc_ref, wo_ref, g1_ref, b1_ref,
                 wr_ref, br_ref, x1_ref, h2l_ref, ti_ref, tp_ref, cnt_ref,
                 *, n_ctx, ctx_len, lat_len):
    i = pl.program_id(0)
    tm = TM_POST
    r = _mod_row(i, tm, n_ctx, lat_len)
    is_ctx = i * tm < n_ctx

    def mod_vec(k):
        return mod_ref[pl.ds(r, 1), k * D_MODEL:(k + 1) * D_MODEL]

    gate1, shift2, scale2 = mod_vec(2), mod_vec(3), mod_vec(4)

    y_f = jnp.where(is_ctx, yc_ref[0], yl_ref[0]).astype(F32)
    y_b = jnp.where(is_ctx, yc_ref[1], yl_ref[1]).astype(F32)
    y = (y_f + y_b) * _silu(z_ref[...].astype(F32))
    gw = D_INNER // SSD_GROUPS
    parts = []
    for g in range(SSD_GROUPS):
        yg = y[:, g * gw:(g + 1) * gw]
        ms = jnp.mean(yg * yg, axis=-1, keepdims=True)
        parts.append(yg * lax.rsqrt(ms + RMS_EPS))
    yn = (jnp.concatenate(parts, axis=-1) * normw_ref[...]).astype(BF16)
    y_ssd = _dot(yn, wssd_ref[...])

    row_len = jnp.where(i * tm < n_ctx, ctx_len, GRID_W)
    pos = _row_pos(tm, row_len)
    u = scc_ref[...].astype(F32) * scv_ref[...].astype(F32)
    half = SC_CONV // 2
    cv = u * scw_ref[half:half + 1, :]
    for k in range(SC_CONV):
        if k != half:
            cv = cv + _shift_rows(u, k - half, pos, row_len) * scw_ref[k:k + 1, :]
    y_sc = _dot((scb_ref[...].astype(F32) * cv).astype(BF16), wsc_ref[...])

    mixed = (jax.nn.sigmoid(gssd_ref[...].astype(F32)) * y_ssd
             + jax.nn.sigmoid(gsc_ref[...].astype(F32)) * y_sc)
    o = _dot(mixed.astype(BF16), wo_ref[...])

    x = jnp.where(is_ctx, xc_ref[...], xl_ref[...])
    res = DEEPNORM_ALPHA * x + gate1 * o
    mu = jnp.mean(res, axis=-1, keepdims=True)
    cen = res - mu
    var = jnp.mean(cen * cen, axis=-1, keepdims=True)
    x1 = cen * lax.rsqrt(var + LN_EPS) * g1_ref[...] + b1_ref[...]
    x1_ref[...] = x1
    h2 = x1 * (1.0 + scale2) + shift2

    logits = _dot_x3(h2, wr_ref[...]) + br_ref[...]
    lane = lax.broadcasted_iota(jnp.int32, (tm, LANES), 1)
    lane_f = lane.astype(F32)
    neg = jnp.float32(-jnp.inf)
    work = jnp.where(lane < N_EXPERTS, logits, neg)
    vals, idxs, hits = [], [], []
    for _ in range(TOP_K):
        m = jnp.max(work, axis=-1, keepdims=True)
        idx = jnp.min(jnp.where(work == m, lane_f, float(LANES)), axis=-1, keepdims=True)
        hit = lane_f == idx
        vals.append(m)
        idxs.append(idx)
        hits.append(hit)
        work = jnp.where(hit, neg, work)
    es = [jnp.exp(v - vals[0]) for v in vals]
    denom = es[0] + es[1] + es[2] + es[3]

    chosen = jnp.where(hits[0] | hits[1] | hits[2] | hits[3], 1.0, 0.0)
    cnt = jnp.sum(chosen, axis=0, keepdims=True)
    cnt_ref[0] = cnt.astype(jnp.int32)
    cpad = jnp.floor((cnt + (RUN_ALIGN - 1)) * (1.0 / RUN_ALIGN)) * RUN_ALIGN
    er = lax.broadcasted_iota(jnp.int32, (LANES, LANES), 0)
    ec = lax.broadcasted_iota(jnp.int32, (LANES, LANES), 1)
    run_off = _dot(jnp.broadcast_to(cpad, (8, LANES)).astype(BF16), (er < ec).astype(BF16))[0:1]
    rr = lax.broadcasted_iota(jnp.int32, (tm, tm), 0)
    cc = lax.broadcasted_iota(jnp.int32, (tm, tm), 1)
    slot = _dot((rr > cc).astype(BF16), chosen.astype(BF16)) + run_off
    lrows = [jnp.sum(jnp.where(hits[k], slot, 0.0), axis=-1, keepdims=True) for k in range(TOP_K)]

    ti = jnp.zeros((tm, LANES), F32)
    tp = jnp.zeros((tm, LANES), F32)
    lmat = jnp.full((tm, LANES), -1.0, F32)
    for k in range(TOP_K):
        ti = jnp.where(lane == k, idxs[k], ti)
        ti = jnp.where(lane == TOP_K + k, lrows[k], ti)
        lmat = jnp.where(lane == k, lrows[k], lmat)
        tp = jnp.where(lane == k, es[k] / denom, tp)
    ti_ref[...] = ti.astype(jnp.int32)
    tp_ref[...] = tp

    lrow_t = jnp.concatenate([lmat[j * LANES:(j + 1) * LANES, :].T for j in range(tm // LANES)],
                             axis=1)
    jrow = lax.broadcasted_iota(jnp.int32, (LOCAL_ROWS, tm), 0).astype(F32)
    sel_t = jnp.where(jrow == lrow_t[0:1, :], 1.0, 0.0)
    for k in range(1, TOP_K):
        sel_t = sel_t + jnp.where(jrow == lrow_t[k:k + 1, :], 1.0, 0.0)
    h2l_ref[...] = _dot(sel_t.astype(BF16), h2.astype(BF16))


def _post_call(x_ctx, x_lat, mod, proj, y_ctx, y_lat, norm_w, w_ssd, sc_w, w_sc, w_o, ln_g, ln_b,
               w_r, b_r, ctx_len, lat_len):
    n_ctx = x_ctx.shape[0]
    t = n_ctx + x_lat.shape[0]
    tm = TM_POST
    kern = functools.partial(_post_kernel, n_ctx=n_ctx, ctx_len=ctx_len, lat_len=lat_len)

    def colblk(off, width):
        return pl.BlockSpec((tm, width), lambda i: (i, off // width))

    def whole(a):
        return pl.BlockSpec(a.shape, lambda i: (0,) * a.ndim, pipeline_mode=pl.Buffered(1))

    return pl.pallas_call(
        kern,
        out_shape=(jax.ShapeDtypeStruct((t, D_MODEL), F32),
                   jax.ShapeDtypeStruct((t // tm * LOCAL_ROWS, D_MODEL), F32),
                   jax.ShapeDtypeStruct((t, LANES), jnp.int32),
                   jax.ShapeDtypeStruct((t, LANES), F32),
                   jax.ShapeDtypeStruct((t // tm, 1, LANES), jnp.int32)),
        grid=(t // tm,),
        in_specs=[
            *_ctx_lat_specs((tm, D_MODEL), n_ctx // tm),
            whole(mod),
            colblk(COL_Z, D_INNER),
            colblk(COL_SCB, D_MODEL), colblk(COL_SCC, D_MODEL), colblk(COL_SCV, D_MODEL),
            colblk(COL_GSSD, D_MODEL), colblk(COL_GSC, D_MODEL),
            *_ctx_lat_specs((2, tm, D_INNER), n_ctx // tm, lead=(0,)),
            whole(norm_w), whole(w_ssd), whole(sc_w), whole(w_sc), whole(w_o),
            whole(ln_g), whole(ln_b), whole(w_r), whole(b_r),
        ],
        out_specs=(pl.BlockSpec((tm, D_MODEL), lambda i: (i, 0)),
                   pl.BlockSpec((LOCAL_ROWS, D_MODEL), lambda i: (i, 0)),
                   pl.BlockSpec((tm, LANES), lambda i: (i, 0)),
                   pl.BlockSpec((tm, LANES), lambda i: (i, 0)),
                   pl.BlockSpec((1, 1, LANES), lambda i: (i, 0, 0))),
        compiler_params=_cparams(("arbitrary",)),
        name="post",
    )(x_ctx, x_lat, mod, proj, proj, proj, proj, proj, proj, y_ctx, y_lat,
      norm_w, w_ssd, sc_w, w_sc, w_o, ln_g, ln_b, w_r, b_r)


def _ffn_kernel(nlive_ref, exp_ref, nv_ref, src_cur_ref, src_nxt_ref, dst_cur_ref,
                h2l_hbm, wgu_ref, bgu_ref, wd_ref, bd_ref,
                yl_hbm, wgu_scr, wd_scr, x_buf, y_buf, sem_x, sem_y):
    s = pl.program_id(0)
    par = s & 1
    n_live = nlive_ref[0]
    nv = nv_ref[s]
    new_expert = (s == 0) | (exp_ref[s] != exp_ref[jnp.maximum(s - 1, 0)])

    def gather(tbl_ref, slot):
        for c in range(CHUNKS_PER_TILE):
            src = pl.multiple_of(tbl_ref[0, 0, c] * RUN_ALIGN, RUN_ALIGN)
            pltpu.make_async_copy(h2l_hbm.at[pl.ds(src, RUN_ALIGN)],
                                  x_buf.at[slot, pl.ds(c * RUN_ALIGN, RUN_ALIGN)],
                                  sem_x.at[slot]).start()

    def out_copy(slot, c, dst):
        return pltpu.make_async_copy(y_buf.at[slot, pl.ds(c * RUN_ALIGN, RUN_ALIGN)],
                                     yl_hbm.at[pl.ds(dst, RUN_ALIGN)], sem_y.at[slot])

    def wait_out(slot, count):
        rows = pl.multiple_of(count * RUN_ALIGN, RUN_ALIGN)
        pltpu.make_async_copy(y_buf.at[slot, pl.ds(0, rows)], yl_hbm.at[pl.ds(0, rows)],
                              sem_y.at[slot]).wait()

    @pl.when(s == 0)
    def _():
        gather(src_cur_ref, 0)

    @pl.when(s < n_live)
    def _():
        pltpu.make_async_copy(h2l_hbm.at[pl.ds(0, TM_FFN)], x_buf.at[par], sem_x.at[par]).wait()

        @pl.when(new_expert)
        def _():
            wgu_scr[...] = wgu_ref[0].astype(BF16)
            wd_scr[...] = wd_ref[0].astype(BF16)

        gather(src_nxt_ref, 1 - par)
        gu = _dot(x_buf[par].astype(BF16), wgu_scr[...]) + bgu_ref[0]
        g = jnp.minimum(gu[:, :D_EXPERT], SWIGLU_LIMIT)
        u = jnp.clip(gu[:, D_EXPERT:], -SWIGLU_LIMIT, SWIGLU_LIMIT)
        act = (u + 1.0) * g * jax.nn.sigmoid(SWIGLU_ALPHA * g)
        y = _dot(act.astype(BF16), wd_scr[...]) + bd_ref[0]

        @pl.when(s >= 2)
        def _():
            wait_out(par, nv_ref[jnp.maximum(s - 2, 0)])

        y_buf[par] = y

        for grp in range(CHUNKS_PER_TILE // SCATTER_GROUP):
            @pl.when(grp * SCATTER_GROUP < nv)
            def _():
                for c in range(grp * SCATTER_GROUP, (grp + 1) * SCATTER_GROUP):
                    dst = pl.multiple_of(dst_cur_ref[0, 0, c] * RUN_ALIGN, RUN_ALIGN)
                    out_copy(par, c, dst).start()

        @pl.when(s == n_live - 1)
        def _():
            pltpu.make_async_copy(h2l_hbm.at[pl.ds(0, TM_FFN)], x_buf.at[1 - par],
                                  sem_x.at[1 - par]).wait()
            wait_out(par, nv)

            @pl.when(s >= 1)
            def _():
                wait_out(1 - par, nv_ref[jnp.maximum(s - 1, 0)])


def _ffn_call(tables, h2l, w_gu, b_gu, w_d, b_d):
    n_live, exp_id, n_valid, src8, dst8 = tables
    n_tiles = src8.shape[0]

    def tbl_spec(index_fn):
        return pl.BlockSpec((1, 1, CHUNKS_PER_TILE), index_fn, memory_space=pltpu.SMEM)

    def per_expert(shape):
        return pl.BlockSpec((1,) + shape, lambda s, nl, ei, *_: (ei[s], 0, 0))

    grid_spec = pltpu.PrefetchScalarGridSpec(
        num_scalar_prefetch=3,
        grid=(n_tiles,),
        in_specs=[
            tbl_spec(lambda s, *_: (s, 0, 0)),
            tbl_spec(lambda s, *_: (jnp.minimum(s + 1, n_tiles - 1), 0, 0)),
            tbl_spec(lambda s, *_: (s, 0, 0)),
            pl.BlockSpec(memory_space=pl.ANY),
            per_expert((D_MODEL, 2 * D_EXPERT)), per_expert((1, 2 * D_EXPERT)),
            per_expert((D_EXPERT, D_MODEL)), per_expert((1, D_MODEL)),
        ],
        out_specs=pl.BlockSpec(memory_space=pl.ANY),
        scratch_shapes=[pltpu.VMEM((D_MODEL, 2 * D_EXPERT), BF16),
                        pltpu.VMEM((D_EXPERT, D_MODEL), BF16),
                        pltpu.VMEM((2, TM_FFN, D_MODEL), F32),
                        pltpu.VMEM((2, TM_FFN, D_MODEL), F32),
                        pltpu.SemaphoreType.DMA((2,)),
                        pltpu.SemaphoreType.DMA((2,))],
    )
    tables_and_chunks = (n_live, exp_id, n_valid, src8, src8, dst8)
    return pl.pallas_call(
        _ffn_kernel,
        out_shape=jax.ShapeDtypeStruct(h2l.shape, F32),
        grid_spec=grid_spec,
        input_output_aliases={len(tables_and_chunks): 0},
        compiler_params=pltpu.CompilerParams(dimension_semantics=("arbitrary",),
                                             vmem_limit_bytes=VMEM_LIMIT,
                                             has_side_effects=True),
        name="ffn",
    )(*tables_and_chunks, h2l, w_gu, b_gu, w_d, b_d)


def _combine_kernel(x1_ref, ti_ref, tp_ref, mod_ref, g2_ref, b2_ref, yl_ref,
                    oc_ref, ol_ref, *, n_ctx, lat_len):
    i = pl.program_id(0)
    tb = TM_POST

    jl = lax.broadcasted_iota(jnp.int32, (tb, LOCAL_ROWS), 1)
    ti = ti_ref[...]
    tp = tp_ref[...]
    sel = jnp.where(jl == ti[:, TOP_K:TOP_K + 1], tp[:, 0:1], 0.0)
    for k in range(1, TOP_K):
        sel = sel + jnp.where(jl == ti[:, TOP_K + k:TOP_K + k + 1], tp[:, k:k + 1], 0.0)
    moe = _dot(sel.astype(BF16), yl_ref[...].astype(BF16))

    r = _mod_row(i, tb, n_ctx, lat_len)
    gate2 = mod_ref[pl.ds(r, 1), 5 * D_MODEL:6 * D_MODEL]
    res = DEEPNORM_ALPHA * x1_ref[...] + gate2 * moe
    mu = jnp.mean(res, axis=-1, keepdims=True)
    cen = res - mu
    var = jnp.mean(cen * cen, axis=-1, keepdims=True)
    out = cen * lax.rsqrt(var + LN_EPS) * g2_ref[...] + b2_ref[...]

    @pl.when(i * tb < n_ctx)
    def _():
        oc_ref[...] = out

    @pl.when(i * tb >= n_ctx)
    def _():
        ol_ref[...] = out


def _combine_call(x1, route, tp, mod, ln_g, ln_b, y_local, n_ctx, lat_len):
    t = x1.shape[0]
    tb = TM_POST
    kern = functools.partial(_combine_kernel, n_ctx=n_ctx, lat_len=lat_len)
    return pl.pallas_call(
        kern,
        out_shape=(jax.ShapeDtypeStruct((n_ctx, D_MODEL), F32),
                   jax.ShapeDtypeStruct((t - n_ctx, D_MODEL), F32)),
        grid=(t // tb,),
        in_specs=[
            pl.BlockSpec((tb, D_MODEL), lambda i: (i, 0)),
            pl.BlockSpec((tb, LANES), lambda i: (i, 0)),
            pl.BlockSpec((tb, LANES), lambda i: (i, 0)),
            pl.BlockSpec(mod.shape, lambda i: (0, 0)),
            pl.BlockSpec((1, D_MODEL), lambda i: (0, 0)),
            pl.BlockSpec((1, D_MODEL), lambda i: (0, 0)),
            pl.BlockSpec((LOCAL_ROWS, D_MODEL), lambda i: (i, 0)),
        ],
        out_specs=_ctx_lat_specs((tb, D_MODEL), n_ctx // tb),
        compiler_params=_cparams(("arbitrary",)),
        name="combine",
    )(x1, route, tp, mod, ln_g, ln_b, y_local)


def _routing_tables(cnt):
    i32 = jnp.int32
    n_blocks = cnt.shape[0]
    max_rows = n_blocks * (TM_POST * TOP_K + N_EXPERTS * (RUN_ALIGN - 1)) + N_EXPERTS * (TM_FFN - RUN_ALIGN)
    n_tiles = -(-max_rows // TM_FFN) + 1

    cpad = (cnt + (RUN_ALIGN - 1)) // RUN_ALIGN * RUN_ALIGN
    run_loc = jnp.cumsum(cpad, axis=1) - cpad
    used = cpad.sum(axis=0)
    region = (used + (TM_FFN - 1)) // TM_FFN * TM_FFN
    incl = jnp.cumsum(region)
    offs = incl - region
    n_live = incl[-1] // TM_FFN
    run_glob = offs[None, :] + jnp.cumsum(cpad, axis=0) - cpad

    experts = jnp.arange(N_EXPERTS, dtype=i32)

    def per_expert_row(e_idx, table):
        onehot = (e_idx[:, None] == experts).astype(F32)
        return jnp.dot(onehot, table.astype(F32), precision=lax.Precision.HIGHEST).astype(i32)

    g_row = jnp.arange(n_tiles * CHUNKS_PER_TILE, dtype=i32) * RUN_ALIGN
    g_exp = jnp.minimum(jnp.sum(incl[None, :] <= g_row[:, None], axis=1), N_EXPERTS - 1).astype(i32)
    rs = per_expert_row(g_exp, run_glob.T)
    re = rs + per_expert_row(g_exp, cpad.T)
    ls = per_expert_row(g_exp, (jnp.arange(n_blocks, dtype=i32)[:, None] * LOCAL_ROWS + run_loc).T)
    inside = (g_row[:, None] >= rs) & (g_row[:, None] < re)
    valid = jnp.any(inside, axis=1)
    local_row = g_row + jnp.sum(jnp.where(inside, ls - rs, 0), axis=1)
    src8 = jnp.where(valid, local_row // RUN_ALIGN, LOCAL_CHUNKS - 1).astype(i32)

    tile = jnp.arange(n_tiles, dtype=i32)
    row0 = jnp.minimum(tile, n_live - 1) * TM_FFN
    e_of = jnp.minimum(jnp.sum(incl[None, :] <= row0[:, None], axis=1), N_EXPERTS - 1).astype(i32)
    run_end = jnp.sum(jnp.where(e_of[:, None] == experts, (offs + used)[None, :], 0), axis=1)
    n_run = jnp.clip((run_end - row0) // RUN_ALIGN, 0, CHUNKS_PER_TILE)
    n_out = (n_run + (SCATTER_GROUP - 1)) // SCATTER_GROUP * SCATTER_GROUP
    n_out = jnp.where(tile < n_live, n_out, 0).astype(i32)
    c_in_tile = jnp.arange(n_tiles * CHUNKS_PER_TILE, dtype=i32) % CHUNKS_PER_TILE
    filler = jnp.logical_not(valid) & (c_in_tile < jnp.repeat(n_out, CHUNKS_PER_TILE))
    u = jnp.cumsum(filler.astype(i32)) - 1
    spare = (u // 3) * LOCAL_CHUNKS + (LOCAL_CHUNKS - 4) + u % 3
    dst8 = jnp.where(valid, local_row // RUN_ALIGN, jnp.where(filler, spare, 0)).astype(i32)
    shape = (n_tiles, 1, CHUNKS_PER_TILE)
    return (n_live.reshape(1).astype(i32), e_of, n_out, src8.reshape(shape), dst8.reshape(shape))


def kernel(x_prompt, x_sample, c, state_ssd_fwd, state_ssd_bwd, c_ctx, w_ada, b_ada, w_in,
           ssd_conv_w, ssd_conv_b, ssd_a_log, ssd_dt_bias, ssd_d, ssd_norm_w, ssd_w_out,
           sc_conv_w, sc_w_out, w_o, ln1_g, ln1_b, w_router, b_router, w_gate_up, b_gate_up,
           w_down, b_down, ln2_g, ln2_b):
    n_ctx_req, ctx_len, _ = x_prompt.shape
    n_lat_req, lat_len, _ = x_sample.shape
    n_ctx = n_ctx_req * ctx_len
    n_lat = n_lat_req * lat_len
    t = n_ctx + n_lat
    assert w_ada.shape[0] == 1, "single trunk layer"
    assert ctx_len % SSD_CHUNK == 0 and lat_len % TS_SSD == 0 and SSD_CHUNK % GRID_W == 0
    assert n_ctx % TM_PROJ == 0 and lat_len % TM_PROJ == 0 and TM_POST % ctx_len == 0
    assert CONV_SUB % ctx_len == 0 and CONV_SUB % GRID_W == 0 and n_ctx % TB_CONV == 0
    assert ctx_len & (ctx_len - 1) == 0 and GRID_W & (GRID_W - 1) == 0

    x_ctx = x_prompt.reshape(n_ctx, D_MODEL)
    x_lat = x_sample.reshape(n_lat, D_MODEL)

    cvec = jnp.concatenate([c_ctx[None, :], c, jnp.zeros((7 - n_lat_req, D_MODEL), F32)], 0)
    mod = _mod_call(cvec, w_ada[0], b_ada[0])

    w = w_in[0]
    o_dt = D_INNER + XBC_DIM
    w_main = jnp.concatenate([w[:, :o_dt], w[:, o_dt + SSD_HEADS:]], axis=1).astype(BF16)
    w_dt = jnp.pad(w[:, o_dt:o_dt + SSD_HEADS], ((0, 0), (0, LANES - SSD_HEADS))).astype(BF16)
    proj, dt_raw = _inproj_call(x_ctx, x_lat, mod, w_main, w_dt, lat_len)

    xbc = _conv_call(proj, ssd_conv_w[0], ssd_conv_b[0], n_ctx, ctx_len)

    pad_h = ((0, 0), (0, LANES - SSD_HEADS))
    a_log = jnp.pad(ssd_a_log[0], pad_h).reshape(2, 1, LANES)
    dt_bias = jnp.pad(ssd_dt_bias[0], pad_h).reshape(2, 1, LANES)
    dvec = jnp.repeat(ssd_d[0], SSD_HEADDIM).reshape(1, D_INNER)
    h0_lat = jnp.stack([state_ssd_fwd[:, 0].reshape(n_lat_req, D_INNER, SSD_STATE),
                        state_ssd_bwd[:, 0].reshape(n_lat_req, D_INNER, SSD_STATE)], axis=1)
    y_ctx, h_ctx = _ssd_call(xbc, dt_raw, None, a_log, dt_bias, dvec, 0, n_ctx_req, ctx_len, True)
    (y_lat,) = _ssd_call(xbc, dt_raw, h0_lat, a_log, dt_bias, dvec, n_ctx, n_lat_req, lat_len, False)

    w_r = jnp.pad(w_router[0], ((0, 0), (0, LANES - N_EXPERTS)))
    b_r = jnp.pad(b_router[0], (0, LANES - N_EXPERTS)).reshape(1, LANES)
    x1, h2_local, route, top_p, cnt = _post_call(
        x_ctx, x_lat, mod, proj, y_ctx, y_lat, ssd_norm_w[0].reshape(1, D_INNER),
        ssd_w_out[0].astype(BF16), sc_conv_w[0], sc_w_out[0].astype(BF16), w_o[0].astype(BF16),
        ln1_g[0].reshape(1, D_MODEL), ln1_b[0].reshape(1, D_MODEL), w_r, b_r, ctx_len, lat_len)

    tables = _routing_tables(cnt[:, 0, :N_EXPERTS])
    y_local = _ffn_call(tables, h2_local,
                        w_gate_up[0], b_gate_up[0].reshape(N_EXPERTS, 1, 2 * D_EXPERT),
                        w_down[0], b_down[0].reshape(N_EXPERTS, 1, D_MODEL))
    out_ctx, out_lat = _combine_call(x1, route, top_p, mod, ln2_g[0].reshape(1, D_MODEL),
                                     ln2_b[0].reshape(1, D_MODEL), y_local, n_ctx, lat_len)

    state_shape = (n_ctx_req, 1, SSD_HEADS, SSD_HEADDIM, SSD_STATE)
    return (out_ctx.reshape(n_ctx_req, ctx_len, D_MODEL), out_lat.reshape(n_lat_req, lat_len, D_MODEL),
            h_ctx[0].reshape(state_shape), h_ctx[1].reshape(state_shape))
```

```python
import functools

import jax
import jax.numpy as jnp
from jax import lax
from jax.experimental import pallas as pl
from jax.experimental.pallas import tpu as pltpu

F32 = jnp.float32
BF16 = jnp.bfloat16

D_MODEL = 1024
GRID_W = 64
D_INNER = 2048
SSD_HEADDIM = 64
SSD_HEADS = 32
SSD_GROUPS = 4
SSD_HPG = 8
SSD_STATE = 128
SSD_CONV = 5
SSD_CHUNK = 128
XBC_DIM = D_INNER + 2 * SSD_GROUPS * SSD_STATE
SC_CONV = 3
N_EXPERTS = 32
TOP_K = 4
D_EXPERT = 1024
SWIGLU_LIMIT = 7.0
SWIGLU_ALPHA = 1.702
LN_EPS = 1e-5
RMS_EPS = 1e-5
DEEPNORM_ALPHA = 2.0 ** 0.25
LOG2E = 1.4426950408889634

LANES = 128
MAIN_COLS = 10240
COL_Z = 0
COL_XBC = 2048
COL_SCB = 5120
COL_SCC = 6144
COL_SCV = 7168
COL_GSSD = 8192
COL_GSC = 9216

VMEM_LIMIT = 56 * 1024 * 1024

TM_PROJ = 1024
TN_PROJ = 2560
TB_CONV = 1024
CONV_SUB = 256
TS_SSD = 512
TM_POST = 256
TM_FFN = 512
RUN_ALIGN = 8
CHUNKS_PER_TILE = TM_FFN // RUN_ALIGN
SCATTER_GROUP = 8
LOCAL_ROWS = TM_POST * TOP_K + N_EXPERTS * RUN_ALIGN
LOCAL_CHUNKS = LOCAL_ROWS // RUN_ALIGN


def _cparams(sem):
    return pltpu.CompilerParams(dimension_semantics=sem, vmem_limit_bytes=VMEM_LIMIT)


def _split3(v):
    hi = v.astype(BF16)
    r1 = v - hi.astype(F32)
    mid = r1.astype(BF16)
    lo = (r1 - mid.astype(F32)).astype(BF16)
    return hi, mid, lo


def _dot(a, b):
    return jnp.dot(a, b, preferred_element_type=F32)


def _dot_exact_lhs(m_bf16, v_f32):
    hi, mid, lo = _split3(v_f32)
    return _dot(m_bf16, hi) + _dot(m_bf16, mid) + _dot(m_bf16, lo)


def _dot_x3(a_f32, b_f32):
    a_hi = a_f32.astype(BF16)
    a_lo = (a_f32 - a_hi.astype(F32)).astype(BF16)
    b_hi = b_f32.astype(BF16)
    b_lo = (b_f32 - b_hi.astype(F32)).astype(BF16)
    return _dot(a_hi, b_hi) + _dot(a_lo, b_hi) + _dot(a_hi, b_lo)


def _silu(v):
    return v * jax.nn.sigmoid(v)


def _softplus(v):
    return jnp.maximum(v, 0.0) + jnp.log(1.0 + jnp.exp(-jnp.abs(v)))


def _mod_kernel(c_ref, w_ref, b_ref, o_ref):
    o_ref[...] = _dot_x3(_silu(c_ref[...]), w_ref[...]) + b_ref[...]


def _mod_call(cvec, w_ada, b_ada):
    n = w_ada.shape[1]
    tn = 1536
    return pl.pallas_call(
        _mod_kernel,
        out_shape=jax.ShapeDtypeStruct((cvec.shape[0], n), F32),
        grid=(n // tn,),
        in_specs=[
            pl.BlockSpec(cvec.shape, lambda j: (0, 0)),
            pl.BlockSpec((D_MODEL, tn), lambda j: (0, j)),
            pl.BlockSpec((1, tn), lambda j: (0, j)),
        ],
        out_specs=pl.BlockSpec((cvec.shape[0], tn), lambda j: (0, j)),
        compiler_params=_cparams(("arbitrary",)),
        name="mod",
    )(cvec, w_ada, b_ada.reshape(1, n))


def _mod_row(block, rows_per_block, n_ctx_tokens, lat_len):
    tok = block * rows_per_block
    return jnp.where(tok < n_ctx_tokens, 0, 1 + (tok - n_ctx_tokens) // lat_len)


def _ctx_lat_specs(block_shape, n_ctx_blocks, lead=()):
    tail = (0,) * (len(block_shape) - len(lead) - 1)
    ctx = pl.BlockSpec(block_shape, lambda i, *_: lead + (jnp.minimum(i, n_ctx_blocks - 1),) + tail)
    lat = pl.BlockSpec(block_shape, lambda i, *_: lead + (jnp.maximum(i - n_ctx_blocks, 0),) + tail)
    return ctx, lat


def _shift_rows(x, off, pos, row_len):
    if off == 0:
        return x
    n = x.shape[0]
    rolled = pltpu.roll(x, (-off) % n, 0)
    ok = (pos + off >= 0) & (pos + off < row_len)
    return jnp.where(ok, rolled, 0.0)


def _row_pos(n, row_len):
    t = lax.broadcasted_iota(jnp.int32, (n, 1), 0)
    return jnp.bitwise_and(t, row_len - 1)


def _inproj_kernel(xc_ref, xl_ref, mod_ref, wa_ref, wb_ref, wdt_ref, o_ref, dt_ref, h_scr,
                   *, n_ctx, lat_len, blocks_a):
    i = pl.program_id(0)
    j = pl.program_id(1)

    @pl.when(j == 0)
    def _():
        r = _mod_row(i, TM_PROJ, n_ctx, lat_len)
        shift = mod_ref[pl.ds(r, 1), 0:D_MODEL]
        scale = mod_ref[pl.ds(r, 1), D_MODEL:2 * D_MODEL]
        x = jnp.where(i * TM_PROJ < n_ctx, xc_ref[...], xl_ref[...])
        h = (x * (1.0 + scale) + shift).astype(BF16)
        h_scr[...] = h
        dt_ref[...] = _dot(h, wdt_ref[...])

    @pl.when(j < blocks_a)
    def _():
        o_ref[...] = _dot(h_scr[...], wa_ref[...]).astype(BF16)

    @pl.when(j >= blocks_a)
    def _():
        o_ref[...] = _dot(h_scr[...], wb_ref[...]).astype(BF16)


def _inproj_call(x_ctx, x_lat, mod, w_a, w_b, w_dt, lat_len):
    n_ctx = x_ctx.shape[0]
    t = n_ctx + x_lat.shape[0]
    blocks_a = w_a.shape[1] // TN_PROJ
    blocks_b = w_b.shape[1] // TN_PROJ
    assert w_a.shape[1] % TN_PROJ == 0 and w_b.shape[1] % TN_PROJ == 0
    kern = functools.partial(_inproj_kernel, n_ctx=n_ctx, lat_len=lat_len, blocks_a=blocks_a)
    return pl.pallas_call(
        kern,
        out_shape=(jax.ShapeDtypeStruct((t, MAIN_COLS), BF16),
                   jax.ShapeDtypeStruct((t, LANES), F32)),
        grid=(t // TM_PROJ, blocks_a + blocks_b),
        in_specs=[
            *_ctx_lat_specs((TM_PROJ, D_MODEL), n_ctx // TM_PROJ),
            pl.BlockSpec(mod.shape, lambda i, j: (0, 0)),
            pl.BlockSpec((D_MODEL, TN_PROJ), lambda i, j: (0, jnp.minimum(j, blocks_a - 1))),
            pl.BlockSpec((D_MODEL, TN_PROJ), lambda i, j: (0, jnp.maximum(j - blocks_a, 0))),
            pl.BlockSpec((D_MODEL, LANES), lambda i, j: (0, 0)),
        ],
        out_specs=(pl.BlockSpec((TM_PROJ, TN_PROJ), lambda i, j: (i, j)),
                   pl.BlockSpec((TM_PROJ, LANES), lambda i, j: (i, 0))),
        scratch_shapes=[pltpu.VMEM((TM_PROJ, D_MODEL), BF16)],
        compiler_params=_cparams(("arbitrary", "arbitrary")),
        name="inproj",
    )(x_ctx, x_lat, mod, w_a, w_b, w_dt)


def _conv_kernel(x_ref, s_ref, w_ref, b_ref, o_ref):
    half = SSD_CONV // 2
    for r in range(0, TB_CONV, CONV_SUB):
        x = x_ref[r:r + CONV_SUB, :]
        acc = x.astype(F32) * w_ref[half:half + 1, :]
        for k in range(SSD_CONV):
            if k != half:
                tap = k if k < half else k - 1
                acc = acc + _dot(s_ref[0, tap], x) * w_ref[k:k + 1, :]
        o_ref[r:r + CONV_SUB, :] = _silu(acc + b_ref[...]).astype(BF16)


def _shift_matrices(n, row_lens, width):
    t = jnp.arange(n, dtype=jnp.int32)
    half = width // 2
    mats = []
    for row_len in row_lens:
        taps = []
        for off in [o for o in range(-half, half + 1) if o != 0]:
            pos = t % row_len + off
            hit = (t[None, :] == t[:, None] + off) & ((pos >= 0) & (pos < row_len))[:, None]
            taps.append(hit)
        mats.append(jnp.stack(taps))
    return jnp.stack(mats).astype(BF16)


def _conv_call(proj, conv_w, conv_b, n_ctx, ctx_len):
    t = proj.shape[0]
    tc = 1024
    n_ctx_blocks = n_ctx // TB_CONV
    shifts = _shift_matrices(CONV_SUB, (ctx_len, GRID_W), SSD_CONV)
    return pl.pallas_call(
        _conv_kernel,
        out_shape=jax.ShapeDtypeStruct((t, XBC_DIM), BF16),
        grid=(t // TB_CONV, XBC_DIM // tc),
        in_specs=[
            pl.BlockSpec((TB_CONV, tc), lambda i, j: (i, COL_XBC // tc + j)),
            pl.BlockSpec((1, SSD_CONV - 1, CONV_SUB, CONV_SUB),
                         lambda i, j: (jnp.where(i < n_ctx_blocks, 0, 1), 0, 0, 0)),
            pl.BlockSpec((SSD_CONV, tc), lambda i, j: (0, j)),
            pl.BlockSpec((1, tc), lambda i, j: (0, j)),
        ],
        out_specs=pl.BlockSpec((TB_CONV, tc), lambda i, j: (i, j)),
        compiler_params=_cparams(("arbitrary", "arbitrary")),
        name="conv",
    )(proj, shifts, conv_w, conv_b.reshape(1, XBC_DIM))


def _ssd_kernel(*refs, has_h0, want_final):
    xs_ref, b_ref, c_ref, dt_ref = refs[:4]
    rest = list(refs[4:])
    h0_ref = rest.pop(0) if has_h0 else None
    alog_ref, bias_ref, dvec_ref, y_ref = rest[:4]
    hfin_ref = rest[4] if want_final else None
    ht_scr = rest[-1]
    d = pl.program_id(1)
    st = pl.program_id(2)
    q = SSD_CHUNK
    n_chunks = xs_ref.shape[0] // q
    n_blk = D_INNER // LANES
    is_fwd = d == 0

    @pl.when(st == 0)
    def _():
        if has_h0:
            for j in range(n_blk):
                ht_scr[:, j * LANES:(j + 1) * LANES] = h0_ref[0, 0, j * LANES:(j + 1) * LANES, :].T
        else:
            ht_scr[...] = jnp.zeros_like(ht_scr)

    a2_neg = -jnp.exp(alog_ref[0]) * LOG2E
    bias = bias_ref[0]
    row = lax.broadcasted_iota(jnp.int32, (q, q), 0)
    col = lax.broadcasted_iota(jnp.int32, (q, q), 1)
    keep = (row - col) * jnp.where(is_fwd, 1, -1) >= 0
    tri = keep.astype(BF16)
    lane_lo = lax.broadcasted_iota(jnp.int32, (q, LANES), 1) < SSD_HEADDIM
    lane_lo1 = lane_lo[0:1, :]
    d_on = jnp.where(is_fwd, 1.0, 0.0)
    neg_inf = jnp.float32(-jnp.inf)

    def chunk_body(ci, carry):
        c = jnp.where(is_fwd, ci, n_chunks - 1 - ci)
        c0 = pl.multiple_of(c * q, q)
        dt = _softplus(dt_ref[pl.ds(c0, q), :] + bias)
        acs = _dot_exact_lhs(tri, dt * a2_neg)
        acs_t = acs.T
        dt_t = dt.T
        tot_row = jnp.where(is_fwd, acs[q - 1:q, :], acs[0:1, :])
        tot_col = jnp.where(is_fwd, acs_t[:, q - 1:q], acs_t[:, 0:1])
        e_tot = jnp.exp2(tot_row)
        src_t = acs_t - jnp.log2(dt_t)
        dte_t = jnp.exp2(tot_col - acs_t) * dt_t

        for g in range(SSD_GROUPS):
            bg = b_ref[pl.ds(c0, q), g * SSD_STATE:(g + 1) * SSD_STATE]
            cg = c_ref[pl.ds(c0, q), g * SSD_STATE:(g + 1) * SSD_STATE]
            cb = lax.dot_general(cg, bg, (((1,), (1,)), ((), ())),
                                 preferred_element_type=F32)
            cgf = cg.astype(F32)
            bg_t = bg.astype(F32).T
            for p in range(SSD_HPG // 2):
                h0 = g * SSD_HPG + 2 * p
                lo = h0 * SSD_HEADDIM
                xp = xs_ref[pl.ds(c0, q), lo:lo + LANES]
                htp = ht_scr[:, lo:lo + LANES]
                rhs = jnp.concatenate([xp, htp.astype(BF16)], axis=0)
                ys, sts = [], []
                for hh in (h0, h0 + 1):
                    tgt = jnp.broadcast_to(acs[:, hh:hh + 1], (q, q))
                    seg = jnp.where(keep, tgt - src_t[hh:hh + 1, :], neg_inf)
                    w = (cb * jnp.exp2(seg)).astype(BF16)
                    ce = (cgf * jnp.exp2(tgt)).astype(BF16)
                    ys.append(_dot(jnp.concatenate([w, ce], axis=1), rhs))
                    sts.append(_dot((bg_t * dte_t[hh:hh + 1, :]).astype(BF16), xp))
                y = (jnp.where(lane_lo, ys[0], ys[1])
                     + (d_on * dvec_ref[:, lo:lo + LANES]) * xp.astype(F32))
                y_ref[0, pl.ds(c0, q), lo:lo + LANES] = y.astype(BF16)
                dec = jnp.where(lane_lo1, e_tot[:, h0:h0 + 1], e_tot[:, h0 + 1:h0 + 2])
                ht_scr[:, lo:lo + LANES] = htp * dec + jnp.where(lane_lo, sts[0], sts[1])
        return carry

    lax.fori_loop(0, n_chunks, chunk_body, 0)

    if want_final:
        @pl.when(st == pl.num_programs(2) - 1)
        def _():
            for j in range(n_blk):
                hfin_ref[0, 0, j * LANES:(j + 1) * LANES, :] = ht_scr[:, j * LANES:(j + 1) * LANES].T


def _ssd_call(xbc, dt_raw, h0, a_log, dt_bias, dvec, tok0, n_seq, seq_len, want_final):
    ts = min(seq_len, TS_SSD)
    assert seq_len % ts == 0 and tok0 % ts == 0 and ts % SSD_CHUNK == 0
    n_steps = seq_len // ts
    blk0 = tok0 // ts
    bc_w = SSD_GROUPS * SSD_STATE

    def tok_blk(b, d, s):
        return blk0 + b * n_steps + jnp.where(d == 0, s, n_steps - 1 - s)

    state_spec = pl.BlockSpec((1, 1, D_INNER, SSD_STATE), lambda b, d, s: (b, d, 0, 0))
    in_specs = [
        pl.BlockSpec((ts, D_INNER), lambda b, d, s: (tok_blk(b, d, s), 0)),
        pl.BlockSpec((ts, bc_w), lambda b, d, s: (tok_blk(b, d, s), D_INNER // bc_w)),
        pl.BlockSpec((ts, bc_w), lambda b, d, s: (tok_blk(b, d, s), D_INNER // bc_w + 1)),
        pl.BlockSpec((ts, LANES), lambda b, d, s: (tok_blk(b, d, s), 0)),
    ]
    args = [xbc, xbc, xbc, dt_raw]
    if h0 is not None:
        in_specs.append(state_spec)
        args.append(h0)
    in_specs += [
        pl.BlockSpec((1, 1, LANES), lambda b, d, s: (d, 0, 0)),
        pl.BlockSpec((1, 1, LANES), lambda b, d, s: (d, 0, 0)),
        pl.BlockSpec((1, D_INNER), lambda b, d, s: (0, 0)),
    ]
    args += [a_log, dt_bias, dvec]
    out_shape = [jax.ShapeDtypeStruct((2, n_seq * seq_len, D_INNER), BF16)]
    out_specs = [pl.BlockSpec((1, ts, D_INNER),
                              lambda b, d, s: (d, tok_blk(b, d, s) - blk0, 0))]
    if want_final:
        out_shape.append(jax.ShapeDtypeStruct((2, n_seq, D_INNER, SSD_STATE), F32))
        out_specs.append(pl.BlockSpec((1, 1, D_INNER, SSD_STATE), lambda b, d, s: (d, b, 0, 0)))
    kern = functools.partial(_ssd_kernel, has_h0=h0 is not None, want_final=want_final)
    return pl.pallas_call(
        kern,
        out_shape=tuple(out_shape),
        grid=(n_seq, 2, n_steps),
        in_specs=in_specs,
        out_specs=tuple(out_specs),
        scratch_shapes=[pltpu.VMEM((SSD_STATE, D_INNER), F32)],
        compiler_params=_cparams(("arbitrary", "arbitrary", "arbitrary")),
        name="ssd_final" if want_final else "ssd",
    )(*args)


def _post_kernel(xc_ref, xl_ref, mod_ref, z_ref, scb_ref, scc_ref, scv_ref, gssd_ref, gsc_ref,
                 yc_ref, yl_ref, normw_ref, wssd_ref, scw_ref, wsc_ref, wo_ref, g1_ref, b1_ref,
                 wr_ref, br_ref, x1_ref, h2l_ref, ti_ref, tp_ref, cnt_ref,
                 *, n_ctx, ctx_len, lat_len):
    i = pl.program_id(0)
    tm = TM_POST
    r = _mod_row(i, tm, n_ctx, lat_len)
    is_ctx = i * tm < n_ctx

    def mod_vec(k):
        return mod_ref[pl.ds(r, 1), k * D_MODEL:(k + 1) * D_MODEL]

    gate1, shift2, scale2 = mod_vec(2), mod_vec(3), mod_vec(4)

    y_f = jnp.where(is_ctx, yc_ref[0], yl_ref[0]).astype(F32)
    y_b = jnp.where(is_ctx, yc_ref[1], yl_ref[1]).astype(F32)
    y = (y_f + y_b) * _silu(z_ref[...].astype(F32))
    gw = D_INNER // SSD_GROUPS
    parts = []
    for g in range(SSD_GROUPS):
        yg = y[:, g * gw:(g + 1) * gw]
        ms = jnp.mean(yg * yg, axis=-1, keepdims=True)
        parts.append(yg * lax.rsqrt(ms + RMS_EPS))
    yn = (jnp.concatenate(parts, axis=-1) * normw_ref[...]).astype(BF16)
    y_ssd = _dot(yn, wssd_ref[...])

    row_len = jnp.where(i * tm < n_ctx, ctx_len, GRID_W)
    pos = _row_pos(tm, row_len)
    u = scc_ref[...].astype(F32) * scv_ref[...].astype(F32)
    half = SC_CONV // 2
    cv = u * scw_ref[half:half + 1, :]
    for k in range(SC_CONV):
        if k != half:
            cv = cv + _shift_rows(u, k - half, pos, row_len) * scw_ref[k:k + 1, :]
    y_sc = _dot((scb_ref[...].astype(F32) * cv).astype(BF16), wsc_ref[...])

    mixed = (jax.nn.sigmoid(gssd_ref[...].astype(F32)) * y_ssd
             + jax.nn.sigmoid(gsc_ref[...].astype(F32)) * y_sc)
    o = _dot(mixed.astype(BF16), wo_ref[...])

    x = jnp.where(is_ctx, xc_ref[...], xl_ref[...])
    res = DEEPNORM_ALPHA * x + gate1 * o
    mu = jnp.mean(res, axis=-1, keepdims=True)
    cen = res - mu
    var = jnp.mean(cen * cen, axis=-1, keepdims=True)
    x1 = cen * lax.rsqrt(var + LN_EPS) * g1_ref[...] + b1_ref[...]
    x1_ref[...] = x1
    h2 = x1 * (1.0 + scale2) + shift2

    logits = _dot_x3(h2, wr_ref[...]) + br_ref[...]
    lane = lax.broadcasted_iota(jnp.int32, (tm, LANES), 1)
    lane_f = lane.astype(F32)
    neg = jnp.float32(-jnp.inf)
    work = jnp.where(lane < N_EXPERTS, logits, neg)
    vals, idxs, hits = [], [], []
    for _ in range(TOP_K):
        m = jnp.max(work, axis=-1, keepdims=True)
        idx = jnp.min(jnp.where(work == m, lane_f, float(LANES)), axis=-1, keepdims=True)
        hit = lane_f == idx
        vals.append(m)
        idxs.append(idx)
        hits.append(hit)
        work = jnp.where(hit, neg, work)
    es = [jnp.exp(v - vals[0]) for v in vals]
    denom = es[0] + es[1] + es[2] + es[3]

    chosen = jnp.where(hits[0] | hits[1] | hits[2] | hits[3], 1.0, 0.0)
    cnt = jnp.sum(chosen, axis=0, keepdims=True)
    cnt_ref[0] = cnt.astype(jnp.int32)
    cpad = jnp.floor((cnt + (RUN_ALIGN - 1)) * (1.0 / RUN_ALIGN)) * RUN_ALIGN
    er = lax.broadcasted_iota(jnp.int32, (LANES, LANES), 0)
    ec = lax.broadcasted_iota(jnp.int32, (LANES, LANES), 1)
    run_off = _dot(jnp.broadcast_to(cpad, (8, LANES)).astype(BF16), (er < ec).astype(BF16))[0:1]
    rr = lax.broadcasted_iota(jnp.int32, (tm, tm), 0)
    cc = lax.broadcasted_iota(jnp.int32, (tm, tm), 1)
    slot = _dot((rr > cc).astype(BF16), chosen.astype(BF16)) + run_off
    lrows = [jnp.sum(jnp.where(hits[k], slot, 0.0), axis=-1, keepdims=True) for k in range(TOP_K)]

    ti = jnp.zeros((tm, LANES), F32)
    tp = jnp.zeros((tm, LANES), F32)
    lmat = jnp.full((tm, LANES), -1.0, F32)
    for k in range(TOP_K):
        ti = jnp.where(lane == k, idxs[k], ti)
        ti = jnp.where(lane == TOP_K + k, lrows[k], ti)
        lmat = jnp.where(lane == k, lrows[k], lmat)
        tp = jnp.where(lane == k, es[k] / denom, tp)
    ti_ref[...] = ti.astype(jnp.int32)
    tp_ref[...] = tp

    lrow_t = jnp.concatenate([lmat[j * LANES:(j + 1) * LANES, :].T for j in range(tm // LANES)],
                             axis=1)
    jrow = lax.broadcasted_iota(jnp.int32, (LOCAL_ROWS, tm), 0).astype(F32)
    sel_t = jnp.where(jrow == lrow_t[0:1, :], 1.0, 0.0)
    for k in range(1, TOP_K):
        sel_t = sel_t + jnp.where(jrow == lrow_t[k:k + 1, :], 1.0, 0.0)
    h2l_ref[...] = _dot(sel_t.astype(BF16), h2.astype(BF16))


def _post_call(x_ctx, x_lat, mod, proj, y_ctx, y_lat, norm_w, w_ssd, sc_w, w_sc, w_o, ln_g, ln_b,
               w_r, b_r, ctx_len, lat_len):
    n_ctx = x_ctx.shape[0]
    t = n_ctx + x_lat.shape[0]
    tm = TM_POST
    kern = functools.partial(_post_kernel, n_ctx=n_ctx, ctx_len=ctx_len, lat_len=lat_len)

    def colblk(off, width):
        return pl.BlockSpec((tm, width), lambda i: (i, off // width))

    def whole(a):
        return pl.BlockSpec(a.shape, lambda i: (0,) * a.ndim, pipeline_mode=pl.Buffered(1))

    return pl.pallas_call(
        kern,
        out_shape=(jax.ShapeDtypeStruct((t, D_MODEL), F32),
                   jax.ShapeDtypeStruct((t // tm * LOCAL_ROWS, D_MODEL), F32),
                   jax.ShapeDtypeStruct((t, LANES), jnp.int32),
                   jax.ShapeDtypeStruct((t, LANES), F32),
                   jax.ShapeDtypeStruct((t // tm, 1, LANES), jnp.int32)),
        grid=(t // tm,),
        in_specs=[
            *_ctx_lat_specs((tm, D_MODEL), n_ctx // tm),
            whole(mod),
            colblk(COL_Z, D_INNER),
            colblk(COL_SCB, D_MODEL), colblk(COL_SCC, D_MODEL), colblk(COL_SCV, D_MODEL),
            colblk(COL_GSSD, D_MODEL), colblk(COL_GSC, D_MODEL),
            *_ctx_lat_specs((2, tm, D_INNER), n_ctx // tm, lead=(0,)),
            whole(norm_w), whole(w_ssd), whole(sc_w), whole(w_sc), whole(w_o),
            whole(ln_g), whole(ln_b), whole(w_r), whole(b_r),
        ],
        out_specs=(pl.BlockSpec((tm, D_MODEL), lambda i: (i, 0)),
                   pl.BlockSpec((LOCAL_ROWS, D_MODEL), lambda i: (i, 0)),
                   pl.BlockSpec((tm, LANES), lambda i: (i, 0)),
                   pl.BlockSpec((tm, LANES), lambda i: (i, 0)),
                   pl.BlockSpec((1, 1, LANES), lambda i: (i, 0, 0))),
        compiler_params=_cparams(("arbitrary",)),
        name="post",
    )(x_ctx, x_lat, mod, proj, proj, proj, proj, proj, proj, y_ctx, y_lat,
      norm_w, w_ssd, sc_w, w_sc, w_o, ln_g, ln_b, w_r, b_r)


def _ffn_kernel(nlive_ref, exp_ref, nv_ref, src_cur_ref, src_nxt_ref, dst_cur_ref,
                h2l_hbm, wgu_ref, bgu_ref, wd_ref, bd_ref,
                yl_hbm, wgu_scr, wd_scr, x_buf, y_buf, sem_x, sem_y):
    s = pl.program_id(0)
    par = s & 1
    n_live = nlive_ref[0]
    nv = nv_ref[s]
    new_expert = (s == 0) | (exp_ref[s] != exp_ref[jnp.maximum(s - 1, 0)])

    def gather(tbl_ref, slot):
        for c in range(CHUNKS_PER_TILE):
            src = pl.multiple_of(tbl_ref[0, 0, c] * RUN_ALIGN, RUN_ALIGN)
            pltpu.make_async_copy(h2l_hbm.at[pl.ds(src, RUN_ALIGN)],
                                  x_buf.at[slot, pl.ds(c * RUN_ALIGN, RUN_ALIGN)],
                                  sem_x.at[slot]).start()

    def out_copy(slot, c, dst):
        return pltpu.make_async_copy(y_buf.at[slot, pl.ds(c * RUN_ALIGN, RUN_ALIGN)],
                                     yl_hbm.at[pl.ds(dst, RUN_ALIGN)], sem_y.at[slot])

    def wait_out(slot, count):
        rows = pl.multiple_of(count * RUN_ALIGN, RUN_ALIGN)
        pltpu.make_async_copy(y_buf.at[slot, pl.ds(0, rows)], yl_hbm.at[pl.ds(0, rows)],
                              sem_y.at[slot]).wait()

    @pl.when(s == 0)
    def _():
        gather(src_cur_ref, 0)

    @pl.when(s < n_live)
    def _():
        pltpu.make_async_copy(h2l_hbm.at[pl.ds(0, TM_FFN)], x_buf.at[par], sem_x.at[par]).wait()

        @pl.when(new_expert)
        def _():
            wgu_scr[...] = wgu_ref[0].astype(BF16)
            wd_scr[...] = wd_ref[0].astype(BF16)

        gather(src_nxt_ref, 1 - par)
        gu = _dot(x_buf[par].astype(BF16), wgu_scr[...]) + bgu_ref[0]
        g = jnp.minimum(gu[:, :D_EXPERT], SWIGLU_LIMIT)
        u = jnp.clip(gu[:, D_EXPERT:], -SWIGLU_LIMIT, SWIGLU_LIMIT)
        act = (u + 1.0) * g * jax.nn.sigmoid(SWIGLU_ALPHA * g)
        y = _dot(act.astype(BF16), wd_scr[...]) + bd_ref[0]

        @pl.when(s >= 2)
        def _():
            wait_out(par, nv_ref[jnp.maximum(s - 2, 0)])

        y_buf[par] = y

        for grp in range(CHUNKS_PER_TILE // SCATTER_GROUP):
            @pl.when(grp * SCATTER_GROUP < nv)
            def _():
                for c in range(grp * SCATTER_GROUP, (grp + 1) * SCATTER_GROUP):
                    dst = pl.multiple_of(dst_cur_ref[0, 0, c] * RUN_ALIGN, RUN_ALIGN)
                    out_copy(par, c, dst).start()

        @pl.when(s == n_live - 1)
        def _():
            pltpu.make_async_copy(h2l_hbm.at[pl.ds(0, TM_FFN)], x_buf.at[1 - par],
                                  sem_x.at[1 - par]).wait()
            wait_out(par, nv)

            @pl.when(s >= 1)
            def _():
                wait_out(1 - par, nv_ref[jnp.maximum(s - 1, 0)])


def _ffn_call(tables, h2l, w_gu, b_gu, w_d, b_d):
    n_live, exp_id, n_valid, src8, dst8 = tables
    n_tiles = src8.shape[0]

    def tbl_spec(index_fn):
        return pl.BlockSpec((1, 1, CHUNKS_PER_TILE), index_fn, memory_space=pltpu.SMEM)

    def per_expert(shape):
        return pl.BlockSpec((1,) + shape, lambda s, nl, ei, *_: (ei[s], 0, 0))

    grid_spec = pltpu.PrefetchScalarGridSpec(
        num_scalar_prefetch=3,
        grid=(n_tiles,),
        in_specs=[
            tbl_spec(lambda s, *_: (s, 0, 0)),
            tbl_spec(lambda s, *_: (jnp.minimum(s + 1, n_tiles - 1), 0, 0)),
            tbl_spec(lambda s, *_: (s, 0, 0)),
            pl.BlockSpec(memory_space=pl.ANY),
            per_expert((D_MODEL, 2 * D_EXPERT)), per_expert((1, 2 * D_EXPERT)),
            per_expert((D_EXPERT, D_MODEL)), per_expert((1, D_MODEL)),
        ],
        out_specs=pl.BlockSpec(memory_space=pl.ANY),
        scratch_shapes=[pltpu.VMEM((D_MODEL, 2 * D_EXPERT), BF16),
                        pltpu.VMEM((D_EXPERT, D_MODEL), BF16),
                        pltpu.VMEM((2, TM_FFN, D_MODEL), F32),
                        pltpu.VMEM((2, TM_FFN, D_MODEL), F32),
                        pltpu.SemaphoreType.DMA((2,)),
                        pltpu.SemaphoreType.DMA((2,))],
    )
    tables_and_chunks = (n_live, exp_id, n_valid, src8, src8, dst8)
    return pl.pallas_call(
        _ffn_kernel,
        out_shape=jax.ShapeDtypeStruct(h2l.shape, F32),
        grid_spec=grid_spec,
        input_output_aliases={len(tables_and_chunks): 0},
        compiler_params=pltpu.CompilerParams(dimension_semantics=("arbitrary",),
                                             vmem_limit_bytes=VMEM_LIMIT,
                                             has_side_effects=True),
        name="ffn",
    )(*tables_and_chunks, h2l, w_gu, b_gu, w_d, b_d)


def _combine_kernel(x1_ref, ti_ref, tp_ref, mod_ref, g2_ref, b2_ref, yl_ref,
                    oc_ref, ol_ref, *, n_ctx, lat_len):
    i = pl.program_id(0)
    tb = TM_POST

    jl = lax.broadcasted_iota(jnp.int32, (tb, LOCAL_ROWS), 1)
    ti = ti_ref[...]
    tp = tp_ref[...]
    sel = jnp.where(jl == ti[:, TOP_K:TOP_K + 1], tp[:, 0:1], 0.0)
    for k in range(1, TOP_K):
        sel = sel + jnp.where(jl == ti[:, TOP_K + k:TOP_K + k + 1], tp[:, k:k + 1], 0.0)
    moe = _dot(sel.astype(BF16), yl_ref[...].astype(BF16))

    r = _mod_row(i, tb, n_ctx, lat_len)
    gate2 = mod_ref[pl.ds(r, 1), 5 * D_MODEL:6 * D_MODEL]
    res = DEEPNORM_ALPHA * x1_ref[...] + gate2 * moe
    mu = jnp.mean(res, axis=-1, keepdims=True)
    cen = res - mu
    var = jnp.mean(cen * cen, axis=-1, keepdims=True)
    out = cen * lax.rsqrt(var + LN_EPS) * g2_ref[...] + b2_ref[...]

    @pl.when(i * tb < n_ctx)
    def _():
        oc_ref[...] = out

    @pl.when(i * tb >= n_ctx)
    def _():
        ol_ref[...] = out


def _combine_call(x1, route, tp, mod, ln_g, ln_b, y_local, n_ctx, lat_len):
    t = x1.shape[0]
    tb = TM_POST
    kern = functools.partial(_combine_kernel, n_ctx=n_ctx, lat_len=lat_len)
    return pl.pallas_call(
        kern,
        out_shape=(jax.ShapeDtypeStruct((n_ctx, D_MODEL), F32),
                   jax.ShapeDtypeStruct((t - n_ctx, D_MODEL), F32)),
        grid=(t // tb,),
        in_specs=[
            pl.BlockSpec((tb, D_MODEL), lambda i: (i, 0)),
            pl.BlockSpec((tb, LANES), lambda i: (i, 0)),
            pl.BlockSpec((tb, LANES), lambda i: (i, 0)),
            pl.BlockSpec(mod.shape, lambda i: (0, 0)),
            pl.BlockSpec((1, D_MODEL), lambda i: (0, 0)),
            pl.BlockSpec((1, D_MODEL), lambda i: (0, 0)),
            pl.BlockSpec((LOCAL_ROWS, D_MODEL), lambda i: (i, 0)),
        ],
        out_specs=_ctx_lat_specs((tb, D_MODEL), n_ctx // tb),
        compiler_params=_cparams(("arbitrary",)),
        name="combine",
    )(x1, route, tp, mod, ln_g, ln_b, y_local)


def _routing_tables(cnt):
    i32 = jnp.int32
    n_blocks = cnt.shape[0]
    max_rows = n_blocks * (TM_POST * TOP_K + N_EXPERTS * (RUN_ALIGN - 1)) + N_EXPERTS * (TM_FFN - RUN_ALIGN)
    n_tiles = -(-max_rows // TM_FFN) + 1

    cpad = (cnt + (RUN_ALIGN - 1)) // RUN_ALIGN * RUN_ALIGN
    run_loc = jnp.cumsum(cpad, axis=1) - cpad
    used = cpad.sum(axis=0)
    region = (used + (TM_FFN - 1)) // TM_FFN * TM_FFN
    incl = jnp.cumsum(region)
    offs = incl - region
    n_live = incl[-1] // TM_FFN
    run_glob = offs[None, :] + jnp.cumsum(cpad, axis=0) - cpad

    experts = jnp.arange(N_EXPERTS, dtype=i32)

    def per_expert_row(e_idx, table):
        onehot = (e_idx[:, None] == experts).astype(F32)
        return jnp.dot(onehot, table.astype(F32), precision=lax.Precision.HIGHEST).astype(i32)

    g_row = jnp.arange(n_tiles * CHUNKS_PER_TILE, dtype=i32) * RUN_ALIGN
    g_exp = jnp.minimum(jnp.sum(incl[None, :] <= g_row[:, None], axis=1), N_EXPERTS - 1).astype(i32)
    rs = per_expert_row(g_exp, run_glob.T)
    re = rs + per_expert_row(g_exp, cpad.T)
    ls = per_expert_row(g_exp, (jnp.arange(n_blocks, dtype=i32)[:, None] * LOCAL_ROWS + run_loc).T)
    inside = (g_row[:, None] >= rs) & (g_row[:, None] < re)
    valid = jnp.any(inside, axis=1)
    local_row = g_row + jnp.sum(jnp.where(inside, ls - rs, 0), axis=1)
    src8 = jnp.where(valid, local_row // RUN_ALIGN, LOCAL_CHUNKS - 1).astype(i32)

    tile = jnp.arange(n_tiles, dtype=i32)
    row0 = jnp.minimum(tile, n_live - 1) * TM_FFN
    e_of = jnp.minimum(jnp.sum(incl[None, :] <= row0[:, None], axis=1), N_EXPERTS - 1).astype(i32)
    run_end = jnp.sum(jnp.where(e_of[:, None] == experts, (offs + used)[None, :], 0), axis=1)
    n_run = jnp.clip((run_end - row0) // RUN_ALIGN, 0, CHUNKS_PER_TILE)
    n_out = (n_run + (SCATTER_GROUP - 1)) // SCATTER_GROUP * SCATTER_GROUP
    n_out = jnp.where(tile < n_live, n_out, 0).astype(i32)
    c_in_tile = jnp.arange(n_tiles * CHUNKS_PER_TILE, dtype=i32) % CHUNKS_PER_TILE
    filler = jnp.logical_not(valid) & (c_in_tile < jnp.repeat(n_out, CHUNKS_PER_TILE))
    u = jnp.cumsum(filler.astype(i32)) - 1
    spare = (u // 3) * LOCAL_CHUNKS + (LOCAL_CHUNKS - 4) + u % 3
    dst8 = jnp.where(valid, local_row // RUN_ALIGN, jnp.where(filler, spare, 0)).astype(i32)
    shape = (n_tiles, 1, CHUNKS_PER_TILE)
    return (n_live.reshape(1).astype(i32), e_of, n_out, src8.reshape(shape), dst8.reshape(shape))


def kernel(x_prompt, x_sample, c, state_ssd_fwd, state_ssd_bwd, c_ctx, w_ada, b_ada, w_in,
           ssd_conv_w, ssd_conv_b, ssd_a_log, ssd_dt_bias, ssd_d, ssd_norm_w, ssd_w_out,
           sc_conv_w, sc_w_out, w_o, ln1_g, ln1_b, w_router, b_router, w_gate_up, b_gate_up,
           w_down, b_down, ln2_g, ln2_b):
    n_ctx_req, ctx_len, _ = x_prompt.shape
    n_lat_req, lat_len, _ = x_sample.shape
    n_ctx = n_ctx_req * ctx_len
    n_lat = n_lat_req * lat_len
    t = n_ctx + n_lat
    assert w_ada.shape[0] == 1, "single trunk layer"
    assert ctx_len % SSD_CHUNK == 0 and lat_len % TS_SSD == 0 and SSD_CHUNK % GRID_W == 0
    assert n_ctx % TM_PROJ == 0 and lat_len % TM_PROJ == 0 and TM_POST % ctx_len == 0
    assert CONV_SUB % ctx_len == 0 and CONV_SUB % GRID_W == 0 and n_ctx % TB_CONV == 0
    assert ctx_len & (ctx_len - 1) == 0 and GRID_W & (GRID_W - 1) == 0

    x_ctx = x_prompt.reshape(n_ctx, D_MODEL)
    x_lat = x_sample.reshape(n_lat, D_MODEL)

    cvec = jnp.concatenate([c_ctx[None, :], c, jnp.zeros((7 - n_lat_req, D_MODEL), F32)], 0)
    mod = _mod_call(cvec, w_ada[0], b_ada[0])

    w = w_in[0]
    o_dt = D_INNER + XBC_DIM
    w_a = w[:, :o_dt].astype(BF16)
    w_b = w[:, o_dt + SSD_HEADS:].astype(BF16)
    w_dt = jnp.pad(w[:, o_dt:o_dt + SSD_HEADS], ((0, 0), (0, LANES - SSD_HEADS))).astype(BF16)
    proj, dt_raw = _inproj_call(x_ctx, x_lat, mod, w_a, w_b, w_dt, lat_len)

    xbc = _conv_call(proj, ssd_conv_w[0], ssd_conv_b[0], n_ctx, ctx_len)

    pad_h = ((0, 0), (0, LANES - SSD_HEADS))
    a_log = jnp.pad(ssd_a_log[0], pad_h).reshape(2, 1, LANES)
    dt_bias = jnp.pad(ssd_dt_bias[0], pad_h).reshape(2, 1, LANES)
    dvec = jnp.repeat(ssd_d[0], SSD_HEADDIM).reshape(1, D_INNER)
    h0_lat = jnp.stack([state_ssd_fwd[:, 0].reshape(n_lat_req, D_INNER, SSD_STATE),
                        state_ssd_bwd[:, 0].reshape(n_lat_req, D_INNER, SSD_STATE)], axis=1)
    y_ctx, h_ctx = _ssd_call(xbc, dt_raw, None, a_log, dt_bias, dvec, 0, n_ctx_req, ctx_len, True)
    (y_lat,) = _ssd_call(xbc, dt_raw, h0_lat, a_log, dt_bias, dvec, n_ctx, n_lat_req, lat_len, False)

    w_r = jnp.pad(w_router[0], ((0, 0), (0, LANES - N_EXPERTS)))
    b_r = jnp.pad(b_router[0], (0, LANES - N_EXPERTS)).reshape(1, LANES)
    x1, h2_local, route, top_p, cnt = _post_call(
        x_ctx, x_lat, mod, proj, y_ctx, y_lat, ssd_norm_w[0].reshape(1, D_INNER),
        ssd_w_out[0].astype(BF16), sc_conv_w[0], sc_w_out[0].astype(BF16), w_o[0].astype(BF16),
        ln1_g[0].reshape(1, D_MODEL), ln1_b[0].reshape(1, D_MODEL), w_r, b_r, ctx_len, lat_len)

    tables = _routing_tables(cnt[:, 0, :N_EXPERTS])
    y_local = _ffn_call(tables, h2_local,
                        w_gate_up[0], b_gate_up[0].reshape(N_EXPERTS, 1, 2 * D_EXPERT),
                        w_down[0], b_down[0].reshape(N_EXPERTS, 1, D_MODEL))
    out_ctx, out_lat = _combine_call(x1, route, top_p, mod, ln2_g[0].reshape(1, D_MODEL),
                                     ln2_b[0].reshape(1, D_MODEL), y_local, n_ctx, lat_len)

    state_shape = (n_ctx_req, 1, SSD_HEADS, SSD_HEADDIM, SSD_STATE)
    return (out_ctx.reshape(n_ctx_req, ctx_len, D_MODEL), out_lat.reshape(n_lat_req, lat_len, D_MODEL),
            h_ctx[0].reshape(state_shape), h_ctx[1].reshape(state_shape))
```

```python
import functools

import jax
import jax.numpy as jnp
from jax import lax
from jax.experimental import pallas as pl
from jax.experimental.pallas import tpu as pltpu

F32 = jnp.float32
BF16 = jnp.bfloat16

D_MODEL = 1024
GRID_W = 64
D_INNER = 2048
SSD_HEADDIM = 64
SSD_HEADS = 32
SSD_GROUPS = 4
SSD_HPG = 8
SSD_STATE = 128
SSD_CONV = 5
SSD_CHUNK = 128
XBC_DIM = D_INNER + 2 * SSD_GROUPS * SSD_STATE
SC_CONV = 3
N_EXPERTS = 32
TOP_K = 4
D_EXPERT = 1024
SWIGLU_LIMIT = 7.0
SWIGLU_ALPHA = 1.702
LN_EPS = 1e-5
RMS_EPS = 1e-5
DEEPNORM_ALPHA = 2.0 ** 0.25
LOG2E = 1.4426950408889634

LANES = 128
MAIN_COLS = 10240
COL_Z = 0
COL_XBC = 2048
COL_SCB = 5120
COL_SCC = 6144
COL_SCV = 7168
COL_GSSD = 8192
COL_GSC = 9216

VMEM_LIMIT = 56 * 1024 * 1024

TM_PROJ = 1024
TN_PROJ = 2048
TB_CONV = 1024
CONV_SUB = 256
TS_SSD = 512
TM_POST = 256
TM_FFN = 512
RUN_ALIGN = 8
CHUNKS_PER_TILE = TM_FFN // RUN_ALIGN
SCATTER_GROUP = 8
LOCAL_ROWS = TM_POST * TOP_K + N_EXPERTS * RUN_ALIGN
LOCAL_CHUNKS = LOCAL_ROWS // RUN_ALIGN


def _cparams(sem):
    return pltpu.CompilerParams(dimension_semantics=sem, vmem_limit_bytes=VMEM_LIMIT)


def _split3(v):
    hi = v.astype(BF16)
    r1 = v - hi.astype(F32)
    mid = r1.astype(BF16)
    lo = (r1 - mid.astype(F32)).astype(BF16)
    return hi, mid, lo


def _dot(a, b):
    return jnp.dot(a, b, preferred_element_type=F32)


def _dot_exact_lhs(m_bf16, v_f32):
    hi, mid, lo = _split3(v_f32)
    return _dot(m_bf16, hi) + _dot(m_bf16, mid) + _dot(m_bf16, lo)


def _dot_x3(a_f32, b_f32):
    a_hi = a_f32.astype(BF16)
    a_lo = (a_f32 - a_hi.astype(F32)).astype(BF16)
    b_hi = b_f32.astype(BF16)
    b_lo = (b_f32 - b_hi.astype(F32)).astype(BF16)
    return _dot(a_hi, b_hi) + _dot(a_lo, b_hi) + _dot(a_hi, b_lo)


def _silu(v):
    return v * jax.nn.sigmoid(v)


def _softplus(v):
    return jnp.maximum(v, 0.0) + jnp.log(1.0 + jnp.exp(-jnp.abs(v)))


def _mod_kernel(c_ref, w_ref, b_ref, o_ref):
    o_ref[...] = _dot_x3(_silu(c_ref[...]), w_ref[...]) + b_ref[...]


def _mod_call(cvec, w_ada, b_ada):
    n = w_ada.shape[1]
    tn = 1536
    return pl.pallas_call(
        _mod_kernel,
        out_shape=jax.ShapeDtypeStruct((cvec.shape[0], n), F32),
        grid=(n // tn,),
        in_specs=[
            pl.BlockSpec(cvec.shape, lambda j: (0, 0)),
            pl.BlockSpec((D_MODEL, tn), lambda j: (0, j)),
            pl.BlockSpec((1, tn), lambda j: (0, j)),
        ],
        out_specs=pl.BlockSpec((cvec.shape[0], tn), lambda j: (0, j)),
        compiler_params=_cparams(("arbitrary",)),
        name="mod",
    )(cvec, w_ada, b_ada.reshape(1, n))


def _mod_row(block, rows_per_block, n_ctx_tokens, lat_len):
    tok = block * rows_per_block
    return jnp.where(tok < n_ctx_tokens, 0, 1 + (tok - n_ctx_tokens) // lat_len)


def _ctx_lat_specs(block_shape, n_ctx_blocks, lead=()):
    tail = (0,) * (len(block_shape) - len(lead) - 1)
    ctx = pl.BlockSpec(block_shape, lambda i, *_: lead + (jnp.minimum(i, n_ctx_blocks - 1),) + tail)
    lat = pl.BlockSpec(block_shape, lambda i, *_: lead + (jnp.maximum(i - n_ctx_blocks, 0),) + tail)
    return ctx, lat


def _shift_rows(x, off, pos, row_len):
    if off == 0:
        return x
    n = x.shape[0]
    rolled = pltpu.roll(x, (-off) % n, 0)
    ok = (pos + off >= 0) & (pos + off < row_len)
    return jnp.where(ok, rolled, 0.0)


def _row_pos(n, row_len):
    t = lax.broadcasted_iota(jnp.int32, (n, 1), 0)
    return jnp.bitwise_and(t, row_len - 1)


def _inproj_kernel(xc_ref, xl_ref, mod_ref, w_ref, wdt_ref, o_ref, dt_ref, h_scr, *, n_ctx, lat_len):
    i = pl.program_id(0)
    j = pl.program_id(1)

    @pl.when(j == 0)
    def _():
        r = _mod_row(i, TM_PROJ, n_ctx, lat_len)
        shift = mod_ref[pl.ds(r, 1), 0:D_MODEL]
        scale = mod_ref[pl.ds(r, 1), D_MODEL:2 * D_MODEL]
        x = jnp.where(i * TM_PROJ < n_ctx, xc_ref[...], xl_ref[...])
        h = (x * (1.0 + scale) + shift).astype(BF16)
        h_scr[...] = h
        dt_ref[...] = _dot(h, wdt_ref[...])

    o_ref[...] = _dot(h_scr[...], w_ref[...]).astype(BF16)


def _inproj_call(x_ctx, x_lat, mod, w_main, w_dt, lat_len):
    n_ctx = x_ctx.shape[0]
    t = n_ctx + x_lat.shape[0]
    kern = functools.partial(_inproj_kernel, n_ctx=n_ctx, lat_len=lat_len)
    return pl.pallas_call(
        kern,
        out_shape=(jax.ShapeDtypeStruct((t, MAIN_COLS), BF16),
                   jax.ShapeDtypeStruct((t, LANES), F32)),
        grid=(t // TM_PROJ, MAIN_COLS // TN_PROJ),
        in_specs=[
            *_ctx_lat_specs((TM_PROJ, D_MODEL), n_ctx // TM_PROJ),
            pl.BlockSpec(mod.shape, lambda i, j: (0, 0)),
            pl.BlockSpec((D_MODEL, TN_PROJ), lambda i, j: (0, j)),
            pl.BlockSpec((D_MODEL, LANES), lambda i, j: (0, 0)),
        ],
        out_specs=(pl.BlockSpec((TM_PROJ, TN_PROJ), lambda i, j: (i, j)),
                   pl.BlockSpec((TM_PROJ, LANES), lambda i, j: (i, 0))),
        scratch_shapes=[pltpu.VMEM((TM_PROJ, D_MODEL), BF16)],
        compiler_params=_cparams(("arbitrary", "arbitrary")),
        name="inproj",
    )(x_ctx, x_lat, mod, w_main, w_dt)


def _conv_kernel(x_ref, s_ref, w_ref, b_ref, o_ref):
    half = SSD_CONV // 2
    for r in range(0, TB_CONV, CONV_SUB):
        x = x_ref[r:r + CONV_SUB, :]
        acc = x.astype(F32) * w_ref[half:half + 1, :]
        for k in range(SSD_CONV):
            if k != half:
                tap = k if k < half else k - 1
                acc = acc + _dot(s_ref[0, tap], x) * w_ref[k:k + 1, :]
        o_ref[r:r + CONV_SUB, :] = _silu(acc + b_ref[...]).astype(BF16)


def _shift_matrices(n, row_lens, width):
    t = jnp.arange(n, dtype=jnp.int32)
    half = width // 2
    mats = []
    for row_len in row_lens:
        taps = []
        for off in [o for o in range(-half, half + 1) if o != 0]:
            pos = t % row_len + off
            hit = (t[None, :] == t[:, None] + off) & ((pos >= 0) & (pos < row_len))[:, None]
            taps.append(hit)
        mats.append(jnp.stack(taps))
    return jnp.stack(mats).astype(BF16)


def _conv_call(proj, conv_w, conv_b, n_ctx, ctx_len):
    t = proj.shape[0]
    tc = 1024
    n_ctx_blocks = n_ctx // TB_CONV
    shifts = _shift_matrices(CONV_SUB, (ctx_len, GRID_W), SSD_CONV)
    return pl.pallas_call(
        _conv_kernel,
        out_shape=jax.ShapeDtypeStruct((t, XBC_DIM), BF16),
        grid=(t // TB_CONV, XBC_DIM // tc),
        in_specs=[
            pl.BlockSpec((TB_CONV, tc), lambda i, j: (i, COL_XBC // tc + j)),
            pl.BlockSpec((1, SSD_CONV - 1, CONV_SUB, CONV_SUB),
                         lambda i, j: (jnp.where(i < n_ctx_blocks, 0, 1), 0, 0, 0)),
            pl.BlockSpec((SSD_CONV, tc), lambda i, j: (0, j)),
            pl.BlockSpec((1, tc), lambda i, j: (0, j)),
        ],
        out_specs=pl.BlockSpec((TB_CONV, tc), lambda i, j: (i, j)),
        compiler_params=_cparams(("arbitrary", "arbitrary")),
        name="conv",
    )(proj, shifts, conv_w, conv_b.reshape(1, XBC_DIM))


def _ssd_kernel(*refs, has_h0, want_final):
    xs_ref, b_ref, c_ref, dt_ref = refs[:4]
    rest = list(refs[4:])
    h0_ref = rest.pop(0) if has_h0 else None
    alog_ref, bias_ref, dvec_ref, y_ref = rest[:4]
    hfin_ref = rest[4] if want_final else None
    ht_scr = rest[-1]
    d = pl.program_id(1)
    st = pl.program_id(2)
    q = SSD_CHUNK
    n_chunks = xs_ref.shape[0] // q
    n_blk = D_INNER // LANES
    is_fwd = d == 0

    @pl.when(st == 0)
    def _():
        if has_h0:
            for j in range(n_blk):
                ht_scr[:, j * LANES:(j + 1) * LANES] = h0_ref[0, 0, j * LANES:(j + 1) * LANES, :].T
        else:
            ht_scr[...] = jnp.zeros_like(ht_scr)

    a2_neg = -jnp.exp(alog_ref[0]) * LOG2E
    bias = bias_ref[0]
    row = lax.broadcasted_iota(jnp.int32, (q, q), 0)
    col = lax.broadcasted_iota(jnp.int32, (q, q), 1)
    keep = (row - col) * jnp.where(is_fwd, 1, -1) >= 0
    tri = keep.astype(BF16)
    lane_lo = lax.broadcasted_iota(jnp.int32, (q, LANES), 1) < SSD_HEADDIM
    lane_lo1 = lane_lo[0:1, :]
    d_on = jnp.where(is_fwd, 1.0, 0.0)
    neg_inf = jnp.float32(-jnp.inf)

    def chunk_body(ci, carry):
        c = jnp.where(is_fwd, ci, n_chunks - 1 - ci)
        c0 = pl.multiple_of(c * q, q)
        dt = _softplus(dt_ref[pl.ds(c0, q), :] + bias)
        acs = _dot_exact_lhs(tri, dt * a2_neg)
        acs_t = acs.T
        dt_t = dt.T
        tot_row = jnp.where(is_fwd, acs[q - 1:q, :], acs[0:1, :])
        tot_col = jnp.where(is_fwd, acs_t[:, q - 1:q], acs_t[:, 0:1])
        e_tot = jnp.exp2(tot_row)
        src_t = acs_t - jnp.log2(dt_t)
        dte_t = jnp.exp2(tot_col - acs_t) * dt_t

        for g in range(SSD_GROUPS):
            bg = b_ref[pl.ds(c0, q), g * SSD_STATE:(g + 1) * SSD_STATE]
            cg = c_ref[pl.ds(c0, q), g * SSD_STATE:(g + 1) * SSD_STATE]
            cb = lax.dot_general(cg, bg, (((1,), (1,)), ((), ())),
                                 preferred_element_type=F32)
            cgf = cg.astype(F32)
            bg_t = bg.astype(F32).T
            for p in range(SSD_HPG // 2):
                h0 = g * SSD_HPG + 2 * p
                lo = h0 * SSD_HEADDIM
                xp = xs_ref[pl.ds(c0, q), lo:lo + LANES]
                htp = ht_scr[:, lo:lo + LANES]
                rhs = jnp.concatenate([xp, htp.astype(BF16)], axis=0)
                ys, sts = [], []
                for hh in (h0, h0 + 1):
                    tgt = jnp.broadcast_to(acs[:, hh:hh + 1], (q, q))
                    seg = jnp.where(keep, tgt - src_t[hh:hh + 1, :], neg_inf)
                    w = (cb * jnp.exp2(seg)).astype(BF16)
                    ce = (cgf * jnp.exp2(tgt)).astype(BF16)
                    ys.append(_dot(jnp.concatenate([w, ce], axis=1), rhs))
                    sts.append(_dot((bg_t * dte_t[hh:hh + 1, :]).astype(BF16), xp))
                y = (jnp.where(lane_lo, ys[0], ys[1])
                     + (d_on * dvec_ref[:, lo:lo + LANES]) * xp.astype(F32))
                y_ref[0, pl.ds(c0, q), lo:lo + LANES] = y.astype(BF16)
                dec = jnp.where(lane_lo1, e_tot[:, h0:h0 + 1], e_tot[:, h0 + 1:h0 + 2])
                ht_scr[:, lo:lo + LANES] = htp * dec + jnp.where(lane_lo, sts[0], sts[1])
        return carry

    lax.fori_loop(0, n_chunks, chunk_body, 0)

    if want_final:
        @pl.when(st == pl.num_programs(2) - 1)
        def _():
            for j in range(n_blk):
                hfin_ref[0, 0, j * LANES:(j + 1) * LANES, :] = ht_scr[:, j * LANES:(j + 1) * LANES].T


def _ssd_call(xbc, dt_raw, h0, a_log, dt_bias, dvec, tok0, n_seq, seq_len, want_final):
    ts = min(seq_len, TS_SSD)
    assert seq_len % ts == 0 and tok0 % ts == 0 and ts % SSD_CHUNK == 0
    n_steps = seq_len // ts
    blk0 = tok0 // ts
    bc_w = SSD_GROUPS * SSD_STATE

    def tok_blk(b, d, s):
        return blk0 + b * n_steps + jnp.where(d == 0, s, n_steps - 1 - s)

    state_spec = pl.BlockSpec((1, 1, D_INNER, SSD_STATE), lambda b, d, s: (b, d, 0, 0))
    in_specs = [
        pl.BlockSpec((ts, D_INNER), lambda b, d, s: (tok_blk(b, d, s), 0)),
        pl.BlockSpec((ts, bc_w), lambda b, d, s: (tok_blk(b, d, s), D_INNER // bc_w)),
        pl.BlockSpec((ts, bc_w), lambda b, d, s: (tok_blk(b, d, s), D_INNER // bc_w + 1)),
        pl.BlockSpec((ts, LANES), lambda b, d, s: (tok_blk(b, d, s), 0)),
    ]
    args = [xbc, xbc, xbc, dt_raw]
    if h0 is not None:
        in_specs.append(state_spec)
        args.append(h0)
    in_specs += [
        pl.BlockSpec((1, 1, LANES), lambda b, d, s: (d, 0, 0)),
        pl.BlockSpec((1, 1, LANES), lambda b, d, s: (d, 0, 0)),
        pl.BlockSpec((1, D_INNER), lambda b, d, s: (0, 0)),
    ]
    args += [a_log, dt_bias, dvec]
    out_shape = [jax.ShapeDtypeStruct((2, n_seq * seq_len, D_INNER), BF16)]
    out_specs = [pl.BlockSpec((1, ts, D_INNER),
                              lambda b, d, s: (d, tok_blk(b, d, s) - blk0, 0))]
    if want_final:
        out_shape.append(jax.ShapeDtypeStruct((2, n_seq, D_INNER, SSD_STATE), F32))
        out_specs.append(pl.BlockSpec((1, 1, D_INNER, SSD_STATE), lambda b, d, s: (d, b, 0, 0)))
    kern = functools.partial(_ssd_kernel, has_h0=h0 is not None, want_final=want_final)
    return pl.pallas_call(
        kern,
        out_shape=tuple(out_shape),
        grid=(n_seq, 2, n_steps),
        in_specs=in_specs,
        out_specs=tuple(out_specs),
        scratch_shapes=[pltpu.VMEM((SSD_STATE, D_INNER), F32)],
        compiler_params=_cparams(("arbitrary", "arbitrary", "arbitrary")),
        name="ssd_final" if want_final else "ssd",
    )(*args)


def _post_kernel(xc_ref, xl_ref, mod_ref, z_ref, scb_ref, scc_ref, scv_ref, gssd_ref, gsc_ref,
                 yc_ref, yl_ref, normw_ref, wssd_ref, scw_ref, wsc_ref, wo_ref, g1_ref, b1_ref,
                 wr_ref, br_ref, x1_ref, h2l_ref, ti_ref, tp_ref, cnt_ref,
                 *, n_ctx, ctx_len, lat_len):
    i = pl.program_id(0)
    tm = TM_POST
    r = _mod_row(i, tm, n_ctx, lat_len)
    is_ctx = i * tm < n_ctx

    def mod_vec(k):
        return mod_ref[pl.ds(r, 1), k * D_MODEL:(k + 1) * D_MODEL]

    gate1, shift2, scale2 = mod_vec(2), mod_vec(3), mod_vec(4)

    y_f = jnp.where(is_ctx, yc_ref[0], yl_ref[0]).astype(F32)
    y_b = jnp.where(is_ctx, yc_ref[1], yl_ref[1]).astype(F32)
    y = (y_f + y_b) * _silu(z_ref[...].astype(F32))
    gw = D_INNER // SSD_GROUPS
    parts = []
    for g in range(SSD_GROUPS):
        yg = y[:, g * gw:(g + 1) * gw]
        ms = jnp.mean(yg * yg, axis=-1, keepdims=True)
        parts.append(yg * lax.rsqrt(ms + RMS_EPS))
    yn = (jnp.concatenate(parts, axis=-1) * normw_ref[...]).astype(BF16)
    y_ssd = _dot(yn, wssd_ref[...])

    row_len = jnp.where(i * tm < n_ctx, ctx_len, GRID_W)
    pos = _row_pos(tm, row_len)
    u = scc_ref[...].astype(F32) * scv_ref[...].astype(F32)
    half = SC_CONV // 2
    cv = u * scw_ref[half:half + 1, :]
    for k in range(SC_CONV):
        if k != half:
            cv = cv + _shift_rows(u, k - half, pos, row_len) * scw_ref[k:k + 1, :]
    y_sc = _dot((scb_ref[...].astype(F32) * cv).astype(BF16), wsc_ref[...])

    mixed = (jax.nn.sigmoid(gssd_ref[...].astype(F32)) * y_ssd
             + jax.nn.sigmoid(gsc_ref[...].astype(F32)) * y_sc)
    o = _dot(mixed.astype(BF16), wo_ref[...])

    x = jnp.where(is_ctx, xc_ref[...], xl_ref[...])
    res = DEEPNORM_ALPHA * x + gate1 * o
    mu = jnp.mean(res, axis=-1, keepdims=True)
    cen = res - mu
    var = jnp.mean(cen * cen, axis=-1, keepdims=True)
    x1 = cen * lax.rsqrt(var + LN_EPS) * g1_ref[...] + b1_ref[...]
    x1_ref[...] = x1
    h2 = x1 * (1.0 + scale2) + shift2

    logits = _dot_x3(h2, wr_ref[...]) + br_ref[...]
    lane = lax.broadcasted_iota(jnp.int32, (tm, LANES), 1)
    lane_f = lane.astype(F32)
    neg = jnp.float32(-jnp.inf)
    work = jnp.where(lane < N_EXPERTS, logits, neg)
    vals, idxs, hits = [], [], []
    for _ in range(TOP_K):
        m = jnp.max(work, axis=-1, keepdims=True)
        idx = jnp.min(jnp.where(work == m, lane_f, float(LANES)), axis=-1, keepdims=True)
        hit = lane_f == idx
        vals.append(m)
        idxs.append(idx)
        hits.append(hit)
        work = jnp.where(hit, neg, work)
    es = [jnp.exp(v - vals[0]) for v in vals]
    denom = es[0] + es[1] + es[2] + es[3]

    chosen = jnp.where(hits[0] | hits[1] | hits[2] | hits[3], 1.0, 0.0)
    cnt = jnp.sum(chosen, axis=0, keepdims=True)
    cnt_ref[0] = cnt.astype(jnp.int32)
    cpad = jnp.floor((cnt + (RUN_ALIGN - 1)) * (1.0 / RUN_ALIGN)) * RUN_ALIGN
    er = lax.broadcasted_iota(jnp.int32, (LANES, LANES), 0)
    ec = lax.broadcasted_iota(jnp.int32, (LANES, LANES), 1)
    run_off = _dot(jnp.broadcast_to(cpad, (8, LANES)).astype(BF16), (er < ec).astype(BF16))[0:1]
    rr = lax.broadcasted_iota(jnp.int32, (tm, tm), 0)
    cc = lax.broadcasted_iota(jnp.int32, (tm, tm), 1)
    slot = _dot((rr > cc).astype(BF16), chosen.astype(BF16)) + run_off
    lrows = [jnp.sum(jnp.where(hits[k], slot, 0.0), axis=-1, keepdims=True) for k in range(TOP_K)]

    ti = jnp.zeros((tm, LANES), F32)
    tp = jnp.zeros((tm, LANES), F32)
    lmat = jnp.full((tm, LANES), -1.0, F32)
    for k in range(TOP_K):
        ti = jnp.where(lane == k, idxs[k], ti)
        ti = jnp.where(lane == TOP_K + k, lrows[k], ti)
        lmat = jnp.where(lane == k, lrows[k], lmat)
        tp = jnp.where(lane == k, es[k] / denom, tp)
    ti_ref[...] = ti.astype(jnp.int32)
    tp_ref[...] = tp

    lrow_t = jnp.concatenate([lmat[j * LANES:(j + 1) * LANES, :].T for j in range(tm // LANES)],
                             axis=1)
    jrow = lax.broadcasted_iota(jnp.int32, (LOCAL_ROWS, tm), 0).astype(F32)
    sel_t = jnp.where(jrow == lrow_t[0:1, :], 1.0, 0.0)
    for k in range(1, TOP_K):
        sel_t = sel_t + jnp.where(jrow == lrow_t[k:k + 1, :], 1.0, 0.0)
    h2l_ref[...] = _dot(sel_t.astype(BF16), h2.astype(BF16))


def _post_call(x_ctx, x_lat, mod, proj, y_ctx, y_lat, norm_w, w_ssd, sc_w, w_sc, w_o, ln_g, ln_b,
               w_r, b_r, ctx_len, lat_len):
    n_ctx = x_ctx.shape[0]
    t = n_ctx + x_lat.shape[0]
    tm = TM_POST
    kern = functools.partial(_post_kernel, n_ctx=n_ctx, ctx_len=ctx_len, lat_len=lat_len)

    def colblk(off, width):
        return pl.BlockSpec((tm, width), lambda i: (i, off // width))

    def whole(a):
        return pl.BlockSpec(a.shape, lambda i: (0,) * a.ndim, pipeline_mode=pl.Buffered(1))

    return pl.pallas_call(
        kern,
        out_shape=(jax.ShapeDtypeStruct((t, D_MODEL), F32),
                   jax.ShapeDtypeStruct((t // tm * LOCAL_ROWS, D_MODEL), F32),
                   jax.ShapeDtypeStruct((t, LANES), jnp.int32),
                   jax.ShapeDtypeStruct((t, LANES), F32),
                   jax.ShapeDtypeStruct((t // tm, 1, LANES), jnp.int32)),
        grid=(t // tm,),
        in_specs=[
            *_ctx_lat_specs((tm, D_MODEL), n_ctx // tm),
            whole(mod),
            colblk(COL_Z, D_INNER),
            colblk(COL_SCB, D_MODEL), colblk(COL_SCC, D_MODEL), colblk(COL_SCV, D_MODEL),
            colblk(COL_GSSD, D_MODEL), colblk(COL_GSC, D_MODEL),
            *_ctx_lat_specs((2, tm, D_INNER), n_ctx // tm, lead=(0,)),
            whole(norm_w), whole(w_ssd), whole(sc_w), whole(w_sc), whole(w_o),
            whole(ln_g), whole(ln_b), whole(w_r), whole(b_r),
        ],
        out_specs=(pl.BlockSpec((tm, D_MODEL), lambda i: (i, 0)),
                   pl.BlockSpec((LOCAL_ROWS, D_MODEL), lambda i: (i, 0)),
                   pl.BlockSpec((tm, LANES), lambda i: (i, 0)),
                   pl.BlockSpec((tm, LANES), lambda i: (i, 0)),
                   pl.BlockSpec((1, 1, LANES), lambda i: (i, 0, 0))),
        compiler_params=_cparams(("arbitrary",)),
        name="post",
    )(x_ctx, x_lat, mod, proj, proj, proj, proj, proj, proj, y_ctx, y_lat,
      norm_w, w_ssd, sc_w, w_sc, w_o, ln_g, ln_b, w_r, b_r)


def _ffn_kernel(nlive_ref, exp_ref, nv_ref, wslot_ref, wnext_ref,
                src_cur_ref, src_nxt_ref, dst_cur_ref,
                h2l_hbm, wgu_hbm, bgu_ref, wd_hbm, bd_ref,
                yl_hbm, wgu_f32, wd_f32, wgu_scr, wd_scr, x_buf, y_buf, sem_x, sem_y, sem_w):
    s = pl.program_id(0)
    par = s & 1
    n_live = nlive_ref[0]
    nv = nv_ref[s]
    new_expert = (s == 0) | (exp_ref[s] != exp_ref[jnp.maximum(s - 1, 0)])
    wslot = wslot_ref[s]

    def weight_copies(expert, slot):
        return (pltpu.make_async_copy(wgu_hbm.at[expert], wgu_f32.at[slot], sem_w.at[slot]),
                pltpu.make_async_copy(wd_hbm.at[expert], wd_f32.at[slot], sem_w.at[slot]))

    def gather(tbl_ref, slot):
        for c in range(CHUNKS_PER_TILE):
            src = pl.multiple_of(tbl_ref[0, 0, c] * RUN_ALIGN, RUN_ALIGN)
            pltpu.make_async_copy(h2l_hbm.at[pl.ds(src, RUN_ALIGN)],
                                  x_buf.at[slot, pl.ds(c * RUN_ALIGN, RUN_ALIGN)],
                                  sem_x.at[slot]).start()

    def out_copy(slot, c, dst):
        return pltpu.make_async_copy(y_buf.at[slot, pl.ds(c * RUN_ALIGN, RUN_ALIGN)],
                                     yl_hbm.at[pl.ds(dst, RUN_ALIGN)], sem_y.at[slot])

    def wait_out(slot, count):
        rows = pl.multiple_of(count * RUN_ALIGN, RUN_ALIGN)
        pltpu.make_async_copy(y_buf.at[slot, pl.ds(0, rows)], yl_hbm.at[pl.ds(0, rows)],
                              sem_y.at[slot]).wait()

    @pl.when(s == 0)
    def _():
        gather(src_cur_ref, 0)
        for cp in weight_copies(exp_ref[0], 0):
            cp.start()

    @pl.when(s < n_live)
    def _():
        pltpu.make_async_copy(h2l_hbm.at[pl.ds(0, TM_FFN)], x_buf.at[par], sem_x.at[par]).wait()

        @pl.when(new_expert)
        def _():
            for cp in weight_copies(0, wslot):
                cp.wait()
            wgu_scr[...] = wgu_f32[wslot].astype(BF16)
            wd_scr[...] = wd_f32[wslot].astype(BF16)

            @pl.when(wnext_ref[s] >= 0)
            def _():
                for cp in weight_copies(wnext_ref[s], 1 - wslot):
                    cp.start()

        gather(src_nxt_ref, 1 - par)
        gu = _dot(x_buf[par].astype(BF16), wgu_scr[...]) + bgu_ref[0]
        g = jnp.minimum(gu[:, :D_EXPERT], SWIGLU_LIMIT)
        u = jnp.clip(gu[:, D_EXPERT:], -SWIGLU_LIMIT, SWIGLU_LIMIT)
        act = (u + 1.0) * g * jax.nn.sigmoid(SWIGLU_ALPHA * g)
        y = _dot(act.astype(BF16), wd_scr[...]) + bd_ref[0]

        @pl.when(s >= 2)
        def _():
            wait_out(par, nv_ref[jnp.maximum(s - 2, 0)])

        y_buf[par] = y

        for grp in range(CHUNKS_PER_TILE // SCATTER_GROUP):
            @pl.when(grp * SCATTER_GROUP < nv)
            def _():
                for c in range(grp * SCATTER_GROUP, (grp + 1) * SCATTER_GROUP):
                    dst = pl.multiple_of(dst_cur_ref[0, 0, c] * RUN_ALIGN, RUN_ALIGN)
                    out_copy(par, c, dst).start()

        @pl.when(s == n_live - 1)
        def _():
            pltpu.make_async_copy(h2l_hbm.at[pl.ds(0, TM_FFN)], x_buf.at[1 - par],
                                  sem_x.at[1 - par]).wait()
            wait_out(par, nv)

            @pl.when(s >= 1)
            def _():
                wait_out(1 - par, nv_ref[jnp.maximum(s - 1, 0)])


def _ffn_call(tables, h2l, w_gu, b_gu, w_d, b_d):
    n_live, exp_id, n_valid, w_slot, w_next, src8, dst8 = tables
    n_tiles = src8.shape[0]

    def tbl_spec(index_fn):
        return pl.BlockSpec((1, 1, CHUNKS_PER_TILE), index_fn, memory_space=pltpu.SMEM)

    def per_expert(shape):
        return pl.BlockSpec((1,) + shape, lambda s, nl, ei, *_: (ei[s], 0, 0))

    grid_spec = pltpu.PrefetchScalarGridSpec(
        num_scalar_prefetch=5,
        grid=(n_tiles,),
        in_specs=[
            tbl_spec(lambda s, *_: (s, 0, 0)),
            tbl_spec(lambda s, *_: (jnp.minimum(s + 1, n_tiles - 1), 0, 0)),
            tbl_spec(lambda s, *_: (s, 0, 0)),
            pl.BlockSpec(memory_space=pl.ANY),
            pl.BlockSpec(memory_space=pl.ANY), per_expert((1, 2 * D_EXPERT)),
            pl.BlockSpec(memory_space=pl.ANY), per_expert((1, D_MODEL)),
        ],
        out_specs=pl.BlockSpec(memory_space=pl.ANY),
        scratch_shapes=[pltpu.VMEM((2, D_MODEL, 2 * D_EXPERT), F32),
                        pltpu.VMEM((2, D_EXPERT, D_MODEL), F32),
                        pltpu.VMEM((D_MODEL, 2 * D_EXPERT), BF16),
                        pltpu.VMEM((D_EXPERT, D_MODEL), BF16),
                        pltpu.VMEM((2, TM_FFN, D_MODEL), F32),
                        pltpu.VMEM((2, TM_FFN, D_MODEL), F32),
                        pltpu.SemaphoreType.DMA((2,)),
                        pltpu.SemaphoreType.DMA((2,)),
                        pltpu.SemaphoreType.DMA((2,))],
    )
    tables_and_chunks = (n_live, exp_id, n_valid, w_slot, w_next, src8, src8, dst8)
    return pl.pallas_call(
        _ffn_kernel,
        out_shape=jax.ShapeDtypeStruct(h2l.shape, F32),
        grid_spec=grid_spec,
        input_output_aliases={len(tables_and_chunks): 0},
        compiler_params=pltpu.CompilerParams(dimension_semantics=("arbitrary",),
                                             vmem_limit_bytes=VMEM_LIMIT,
                                             has_side_effects=True),
        name="ffn",
    )(*tables_and_chunks, h2l, w_gu, b_gu, w_d, b_d)


def _combine_kernel(x1_ref, ti_ref, tp_ref, mod_ref, g2_ref, b2_ref, yl_ref,
                    oc_ref, ol_ref, *, n_ctx, lat_len):
    i = pl.program_id(0)
    tb = TM_POST

    jl = lax.broadcasted_iota(jnp.int32, (tb, LOCAL_ROWS), 1)
    ti = ti_ref[...]
    tp = tp_ref[...]
    sel = jnp.where(jl == ti[:, TOP_K:TOP_K + 1], tp[:, 0:1], 0.0)
    for k in range(1, TOP_K):
        sel = sel + jnp.where(jl == ti[:, TOP_K + k:TOP_K + k + 1], tp[:, k:k + 1], 0.0)
    moe = _dot(sel.astype(BF16), yl_ref[...].astype(BF16))

    r = _mod_row(i, tb, n_ctx, lat_len)
    gate2 = mod_ref[pl.ds(r, 1), 5 * D_MODEL:6 * D_MODEL]
    res = DEEPNORM_ALPHA * x1_ref[...] + gate2 * moe
    mu = jnp.mean(res, axis=-1, keepdims=True)
    cen = res - mu
    var = jnp.mean(cen * cen, axis=-1, keepdims=True)
    out = cen * lax.rsqrt(var + LN_EPS) * g2_ref[...] + b2_ref[...]

    @pl.when(i * tb < n_ctx)
    def _():
        oc_ref[...] = out

    @pl.when(i * tb >= n_ctx)
    def _():
        ol_ref[...] = out


def _combine_call(x1, route, tp, mod, ln_g, ln_b, y_local, n_ctx, lat_len):
    t = x1.shape[0]
    tb = TM_POST
    kern = functools.partial(_combine_kernel, n_ctx=n_ctx, lat_len=lat_len)
    return pl.pallas_call(
        kern,
        out_shape=(jax.ShapeDtypeStruct((n_ctx, D_MODEL), F32),
                   jax.ShapeDtypeStruct((t - n_ctx, D_MODEL), F32)),
        grid=(t // tb,),
        in_specs=[
            pl.BlockSpec((tb, D_MODEL), lambda i: (i, 0)),
            pl.BlockSpec((tb, LANES), lambda i: (i, 0)),
            pl.BlockSpec((tb, LANES), lambda i: (i, 0)),
            pl.BlockSpec(mod.shape, lambda i: (0, 0)),
            pl.BlockSpec((1, D_MODEL), lambda i: (0, 0)),
            pl.BlockSpec((1, D_MODEL), lambda i: (0, 0)),
            pl.BlockSpec((LOCAL_ROWS, D_MODEL), lambda i: (i, 0)),
        ],
        out_specs=_ctx_lat_specs((tb, D_MODEL), n_ctx // tb),
        compiler_params=_cparams(("arbitrary",)),
        name="combine",
    )(x1, route, tp, mod, ln_g, ln_b, y_local)


def _routing_tables(cnt):
    i32 = jnp.int32
    n_blocks = cnt.shape[0]
    max_rows = n_blocks * (TM_POST * TOP_K + N_EXPERTS * (RUN_ALIGN - 1)) + N_EXPERTS * (TM_FFN - RUN_ALIGN)
    n_tiles = -(-max_rows // TM_FFN) + 1

    cpad = (cnt + (RUN_ALIGN - 1)) // RUN_ALIGN * RUN_ALIGN
    run_loc = jnp.cumsum(cpad, axis=1) - cpad
    used = cpad.sum(axis=0)
    region = (used + (TM_FFN - 1)) // TM_FFN * TM_FFN
    incl = jnp.cumsum(region)
    offs = incl - region
    n_live = incl[-1] // TM_FFN
    run_glob = offs[None, :] + jnp.cumsum(cpad, axis=0) - cpad

    experts = jnp.arange(N_EXPERTS, dtype=i32)

    def per_expert_row(e_idx, table):
        onehot = (e_idx[:, None] == experts).astype(F32)
        return jnp.dot(onehot, table.astype(F32), precision=lax.Precision.HIGHEST).astype(i32)

    g_row = jnp.arange(n_tiles * CHUNKS_PER_TILE, dtype=i32) * RUN_ALIGN
    g_exp = jnp.minimum(jnp.sum(incl[None, :] <= g_row[:, None], axis=1), N_EXPERTS - 1).astype(i32)
    rs = per_expert_row(g_exp, run_glob.T)
    re = rs + per_expert_row(g_exp, cpad.T)
    ls = per_expert_row(g_exp, (jnp.arange(n_blocks, dtype=i32)[:, None] * LOCAL_ROWS + run_loc).T)
    inside = (g_row[:, None] >= rs) & (g_row[:, None] < re)
    valid = jnp.any(inside, axis=1)
    local_row = g_row + jnp.sum(jnp.where(inside, ls - rs, 0), axis=1)
    src8 = jnp.where(valid, local_row // RUN_ALIGN, LOCAL_CHUNKS - 1).astype(i32)

    tile = jnp.arange(n_tiles, dtype=i32)
    row0 = jnp.minimum(tile, n_live - 1) * TM_FFN
    e_of = jnp.minimum(jnp.sum(incl[None, :] <= row0[:, None], axis=1), N_EXPERTS - 1).astype(i32)
    run_end = jnp.sum(jnp.where(e_of[:, None] == experts, (offs + used)[None, :], 0), axis=1)
    n_run = jnp.clip((run_end - row0) // RUN_ALIGN, 0, CHUNKS_PER_TILE)
    n_out = (n_run + (SCATTER_GROUP - 1)) // SCATTER_GROUP * SCATTER_GROUP
    n_out = jnp.where(tile < n_live, n_out, 0).astype(i32)
    c_in_tile = jnp.arange(n_tiles * CHUNKS_PER_TILE, dtype=i32) % CHUNKS_PER_TILE
    filler = jnp.logical_not(valid) & (c_in_tile < jnp.repeat(n_out, CHUNKS_PER_TILE))
    u = jnp.cumsum(filler.astype(i32)) - 1
    spare = (u // 3) * LOCAL_CHUNKS + (LOCAL_CHUNKS - 4) + u % 3
    dst8 = jnp.where(valid, local_row // RUN_ALIGN, jnp.where(filler, spare, 0)).astype(i32)

    has_rows = used > 0
    ordinal = jnp.cumsum(has_rows.astype(i32)) - 1
    w_slot = (jnp.sum(jnp.where(e_of[:, None] == experts, ordinal[None, :], 0), axis=1) % 2).astype(i32)
    later = (experts[None, :] > e_of[:, None]) & has_rows[None, :]
    w_next = jnp.min(jnp.where(later, experts[None, :], N_EXPERTS), axis=1)
    w_next = jnp.where(w_next < N_EXPERTS, w_next, -1).astype(i32)
    shape = (n_tiles, 1, CHUNKS_PER_TILE)
    return (n_live.reshape(1).astype(i32), e_of, n_out, w_slot, w_next,
            src8.reshape(shape), dst8.reshape(shape))


def kernel(x_prompt, x_sample, c, state_ssd_fwd, state_ssd_bwd, c_ctx, w_ada, b_ada, w_in,
           ssd_conv_w, ssd_conv_b, ssd_a_log, ssd_dt_bias, ssd_d, ssd_norm_w, ssd_w_out,
           sc_conv_w, sc_w_out, w_o, ln1_g, ln1_b, w_router, b_router, w_gate_up, b_gate_up,
           w_down, b_down, ln2_g, ln2_b):
    n_ctx_req, ctx_len, _ = x_prompt.shape
    n_lat_req, lat_len, _ = x_sample.shape
    n_ctx = n_ctx_req * ctx_len
    n_lat = n_lat_req * lat_len
    t = n_ctx + n_lat
    assert w_ada.shape[0] == 1, "single trunk layer"
    assert ctx_len % SSD_CHUNK == 0 and lat_len % TS_SSD == 0 and SSD_CHUNK % GRID_W == 0
    assert n_ctx % TM_PROJ == 0 and lat_len % TM_PROJ == 0 and TM_POST % ctx_len == 0
    assert CONV_SUB % ctx_len == 0 and CONV_SUB % GRID_W == 0 and n_ctx % TB_CONV == 0
    assert ctx_len & (ctx_len - 1) == 0 and GRID_W & (GRID_W - 1) == 0

    x_ctx = x_prompt.reshape(n_ctx, D_MODEL)
    x_lat = x_sample.reshape(n_lat, D_MODEL)

    cvec = jnp.concatenate([c_ctx[None, :], c, jnp.zeros((7 - n_lat_req, D_MODEL), F32)], 0)
    mod = _mod_call(cvec, w_ada[0], b_ada[0])

    w = w_in[0]
    o_dt = D_INNER + XBC_DIM
    w_main = jnp.concatenate([w[:, :o_dt], w[:, o_dt + SSD_HEADS:]], axis=1).astype(BF16)
    w_dt = jnp.pad(w[:, o_dt:o_dt + SSD_HEADS], ((0, 0), (0, LANES - SSD_HEADS))).astype(BF16)
    proj, dt_raw = _inproj_call(x_ctx, x_lat, mod, w_main, w_dt, lat_len)

    xbc = _conv_call(proj, ssd_conv_w[0], ssd_conv_b[0], n_ctx, ctx_len)

    pad_h = ((0, 0), (0, LANES - SSD_HEADS))
    a_log = jnp.pad(ssd_a_log[0], pad_h).reshape(2, 1, LANES)
    dt_bias = jnp.pad(ssd_dt_bias[0], pad_h).reshape(2, 1, LANES)
    dvec = jnp.repeat(ssd_d[0], SSD_HEADDIM).reshape(1, D_INNER)
    h0_lat = jnp.stack([state_ssd_fwd[:, 0].reshape(n_lat_req, D_INNER, SSD_STATE),
                        state_ssd_bwd[:, 0].reshape(n_lat_req, D_INNER, SSD_STATE)], axis=1)
    y_ctx, h_ctx = _ssd_call(xbc, dt_raw, None, a_log, dt_bias, dvec, 0, n_ctx_req, ctx_len, True)
    (y_lat,) = _ssd_call(xbc, dt_raw, h0_lat, a_log, dt_bias, dvec, n_ctx, n_lat_req, lat_len, False)

    w_r = jnp.pad(w_router[0], ((0, 0), (0, LANES - N_EXPERTS)))
    b_r = jnp.pad(b_router[0], (0, LANES - N_EXPERTS)).reshape(1, LANES)
    x1, h2_local, route, top_p, cnt = _post_call(
        x_ctx, x_lat, mod, proj, y_ctx, y_lat, ssd_norm_w[0].reshape(1, D_INNER),
        ssd_w_out[0].astype(BF16), sc_conv_w[0], sc_w_out[0].astype(BF16), w_o[0].astype(BF16),
        ln1_g[0].reshape(1, D_MODEL), ln1_b[0].reshape(1, D_MODEL), w_r, b_r, ctx_len, lat_len)

    tables = _routing_tables(cnt[:, 0, :N_EXPERTS])
    y_local = _ffn_call(tables, h2_local,
                        w_gate_up[0], b_gate_up[0].reshape(N_EXPERTS, 1, 2 * D_EXPERT),
                        w_down[0], b_down[0].reshape(N_EXPERTS, 1, D_MODEL))
    out_ctx, out_lat = _combine_call(x1, route, top_p, mod, ln2_g[0].reshape(1, D_MODEL),
                                     ln2_b[0].reshape(1, D_MODEL), y_local, n_ctx, lat_len)

    state_shape = (n_ctx_req, 1, SSD_HEADS, SSD_HEADDIM, SSD_STATE)
    return (out_ctx.reshape(n_ctx_req, ctx_len, D_MODEL), out_lat.reshape(n_lat_req, lat_len, D_MODEL),
            h_ctx[0].reshape(state_shape), h_ctx[1].reshape(state_shape))
```

```python
import functools

import jax
import jax.numpy as jnp
from jax import lax
from jax.experimental import pallas as pl
from jax.experimental.pallas import tpu as pltpu

F32 = jnp.float32
BF16 = jnp.bfloat16

D_MODEL = 1024
GRID_W = 64
D_INNER = 2048
SSD_HEADDIM = 64
SSD_HEADS = 32
SSD_GROUPS = 4
SSD_HPG = 8
SSD_STATE = 128
SSD_CONV = 5
SSD_CHUNK = 128
XBC_DIM = D_INNER + 2 * SSD_GROUPS * SSD_STATE
SC_CONV = 3
N_EXPERTS = 32
TOP_K = 4
D_EXPERT = 1024
SWIGLU_LIMIT = 7.0
SWIGLU_ALPHA = 1.702
LN_EPS = 1e-5
RMS_EPS = 1e-5
DEEPNORM_ALPHA = 2.0 ** 0.25
LOG2E = 1.4426950408889634

LANES = 128
MAIN_COLS = 10240
COL_Z = 0
COL_XBC = 2048
COL_SCB = 5120
COL_SCC = 6144
COL_SCV = 7168
COL_GSSD = 8192
COL_GSC = 9216

VMEM_LIMIT = 56 * 1024 * 1024

TM_PROJ = 1024
TN_PROJ = 2048
TB_CONV = 1024
CONV_SUB = 256
TS_SSD = 512
TM_POST = 256
TM_FFN = 512
RUN_ALIGN = 8
CHUNKS_PER_TILE = TM_FFN // RUN_ALIGN
SCATTER_GROUP = 8
LOCAL_ROWS = TM_POST * TOP_K + N_EXPERTS * RUN_ALIGN
LOCAL_CHUNKS = LOCAL_ROWS // RUN_ALIGN


def _cparams(sem):
    return pltpu.CompilerParams(dimension_semantics=sem, vmem_limit_bytes=VMEM_LIMIT)


def _split3(v):
    hi = v.astype(BF16)
    r1 = v - hi.astype(F32)
    mid = r1.astype(BF16)
    lo = (r1 - mid.astype(F32)).astype(BF16)
    return hi, mid, lo


def _dot(a, b):
    return jnp.dot(a, b, preferred_element_type=F32)


def _dot_exact_lhs(m_bf16, v_f32):
    hi, mid, lo = _split3(v_f32)
    return _dot(m_bf16, hi) + _dot(m_bf16, mid) + _dot(m_bf16, lo)


def _dot_x3(a_f32, b_f32):
    a_hi = a_f32.astype(BF16)
    a_lo = (a_f32 - a_hi.astype(F32)).astype(BF16)
    b_hi = b_f32.astype(BF16)
    b_lo = (b_f32 - b_hi.astype(F32)).astype(BF16)
    return _dot(a_hi, b_hi) + _dot(a_lo, b_hi) + _dot(a_hi, b_lo)


def _silu(v):
    return v * jax.nn.sigmoid(v)


def _softplus(v):
    return jnp.maximum(v, 0.0) + jnp.log(1.0 + jnp.exp(-jnp.abs(v)))


def _mod_kernel(c_ref, w_ref, b_ref, o_ref):
    o_ref[...] = _dot_x3(_silu(c_ref[...]), w_ref[...]) + b_ref[...]


def _mod_call(cvec, w_ada, b_ada):
    n = w_ada.shape[1]
    tn = 1536
    return pl.pallas_call(
        _mod_kernel,
        out_shape=jax.ShapeDtypeStruct((cvec.shape[0], n), F32),
        grid=(n // tn,),
        in_specs=[
            pl.BlockSpec(cvec.shape, lambda j: (0, 0)),
            pl.BlockSpec((D_MODEL, tn), lambda j: (0, j)),
            pl.BlockSpec((1, tn), lambda j: (0, j)),
        ],
        out_specs=pl.BlockSpec((cvec.shape[0], tn), lambda j: (0, j)),
        compiler_params=_cparams(("arbitrary",)),
        name="mod",
    )(cvec, w_ada, b_ada.reshape(1, n))


def _mod_row(block, rows_per_block, n_ctx_tokens, lat_len):
    tok = block * rows_per_block
    return jnp.where(tok < n_ctx_tokens, 0, 1 + (tok - n_ctx_tokens) // lat_len)


def _ctx_lat_specs(block_shape, n_ctx_blocks, lead=()):
    tail = (0,) * (len(block_shape) - len(lead) - 1)
    ctx = pl.BlockSpec(block_shape, lambda i, *_: lead + (jnp.minimum(i, n_ctx_blocks - 1),) + tail)
    lat = pl.BlockSpec(block_shape, lambda i, *_: lead + (jnp.maximum(i - n_ctx_blocks, 0),) + tail)
    return ctx, lat


def _shift_rows(x, off, pos, row_len):
    if off == 0:
        return x
    n = x.shape[0]
    rolled = pltpu.roll(x, (-off) % n, 0)
    ok = (pos + off >= 0) & (pos + off < row_len)
    return jnp.where(ok, rolled, 0.0)


def _row_pos(n, row_len):
    t = lax.broadcasted_iota(jnp.int32, (n, 1), 0)
    return jnp.bitwise_and(t, row_len - 1)


def _inproj_kernel(xc_ref, xl_ref, mod_ref, w_ref, wdt_ref, o_ref, dt_ref, h_scr, *, n_ctx, lat_len):
    i = pl.program_id(0)
    j = pl.program_id(1)

    @pl.when(j == 0)
    def _():
        r = _mod_row(i, TM_PROJ, n_ctx, lat_len)
        shift = mod_ref[pl.ds(r, 1), 0:D_MODEL]
        scale = mod_ref[pl.ds(r, 1), D_MODEL:2 * D_MODEL]
        x = jnp.where(i * TM_PROJ < n_ctx, xc_ref[...], xl_ref[...])
        h = (x * (1.0 + scale) + shift).astype(BF16)
        h_scr[...] = h
        dt_ref[...] = _dot(h, wdt_ref[...])

    o_ref[...] = _dot(h_scr[...], w_ref[...]).astype(BF16)


def _inproj_call(x_ctx, x_lat, mod, w_main, w_dt, lat_len):
    n_ctx = x_ctx.shape[0]
    t = n_ctx + x_lat.shape[0]
    kern = functools.partial(_inproj_kernel, n_ctx=n_ctx, lat_len=lat_len)
    return pl.pallas_call(
        kern,
        out_shape=(jax.ShapeDtypeStruct((t, MAIN_COLS), BF16),
                   jax.ShapeDtypeStruct((t, LANES), F32)),
        grid=(t // TM_PROJ, MAIN_COLS // TN_PROJ),
        in_specs=[
            *_ctx_lat_specs((TM_PROJ, D_MODEL), n_ctx // TM_PROJ),
            pl.BlockSpec(mod.shape, lambda i, j: (0, 0)),
            pl.BlockSpec((D_MODEL, TN_PROJ), lambda i, j: (0, j)),
            pl.BlockSpec((D_MODEL, LANES), lambda i, j: (0, 0)),
        ],
        out_specs=(pl.BlockSpec((TM_PROJ, TN_PROJ), lambda i, j: (i, j)),
                   pl.BlockSpec((TM_PROJ, LANES), lambda i, j: (i, 0))),
        scratch_shapes=[pltpu.VMEM((TM_PROJ, D_MODEL), BF16)],
        compiler_params=_cparams(("arbitrary", "arbitrary")),
        name="inproj",
    )(x_ctx, x_lat, mod, w_main, w_dt)


def _conv_kernel(x_ref, s_ref, w_ref, b_ref, o_ref):
    half = SSD_CONV // 2
    for r in range(0, TB_CONV, CONV_SUB):
        x = x_ref[r:r + CONV_SUB, :]
        acc = x.astype(F32) * w_ref[half:half + 1, :]
        for k in range(SSD_CONV):
            if k != half:
                tap = k if k < half else k - 1
                acc = acc + _dot(s_ref[0, tap], x) * w_ref[k:k + 1, :]
        o_ref[r:r + CONV_SUB, :] = _silu(acc + b_ref[...]).astype(BF16)


def _shift_matrices(n, row_lens, width):
    t = jnp.arange(n, dtype=jnp.int32)
    half = width // 2
    mats = []
    for row_len in row_lens:
        taps = []
        for off in [o for o in range(-half, half + 1) if o != 0]:
            pos = t % row_len + off
            hit = (t[None, :] == t[:, None] + off) & ((pos >= 0) & (pos < row_len))[:, None]
            taps.append(hit)
        mats.append(jnp.stack(taps))
    return jnp.stack(mats).astype(BF16)


def _conv_call(proj, conv_w, conv_b, n_ctx, ctx_len):
    t = proj.shape[0]
    tc = 1024
    n_ctx_blocks = n_ctx // TB_CONV
    shifts = _shift_matrices(CONV_SUB, (ctx_len, GRID_W), SSD_CONV)
    return pl.pallas_call(
        _conv_kernel,
        out_shape=jax.ShapeDtypeStruct((t, XBC_DIM), BF16),
        grid=(t // TB_CONV, XBC_DIM // tc),
        in_specs=[
            pl.BlockSpec((TB_CONV, tc), lambda i, j: (i, COL_XBC // tc + j)),
            pl.BlockSpec((1, SSD_CONV - 1, CONV_SUB, CONV_SUB),
                         lambda i, j: (jnp.where(i < n_ctx_blocks, 0, 1), 0, 0, 0)),
            pl.BlockSpec((SSD_CONV, tc), lambda i, j: (0, j)),
            pl.BlockSpec((1, tc), lambda i, j: (0, j)),
        ],
        out_specs=pl.BlockSpec((TB_CONV, tc), lambda i, j: (i, j)),
        compiler_params=_cparams(("arbitrary", "arbitrary")),
        name="conv",
    )(proj, shifts, conv_w, conv_b.reshape(1, XBC_DIM))


def _ssd_kernel(*refs, has_h0, want_final):
    xs_ref, b_ref, c_ref, dt_ref = refs[:4]
    rest = list(refs[4:])
    h0_ref = rest.pop(0) if has_h0 else None
    alog_ref, bias_ref, dvec_ref, y_ref = rest[:4]
    hfin_ref = rest[4] if want_final else None
    ht_scr = rest[-1]
    d = pl.program_id(1)
    st = pl.program_id(2)
    q = SSD_CHUNK
    n_chunks = xs_ref.shape[0] // q
    n_blk = D_INNER // LANES
    is_fwd = d == 0

    @pl.when(st == 0)
    def _():
        if has_h0:
            for j in range(n_blk):
                ht_scr[:, j * LANES:(j + 1) * LANES] = h0_ref[0, 0, j * LANES:(j + 1) * LANES, :].T
        else:
            ht_scr[...] = jnp.zeros_like(ht_scr)

    a2_neg = -jnp.exp(alog_ref[0]) * LOG2E
    bias = bias_ref[0]
    row = lax.broadcasted_iota(jnp.int32, (q, q), 0)
    col = lax.broadcasted_iota(jnp.int32, (q, q), 1)
    keep = (row - col) * jnp.where(is_fwd, 1, -1) >= 0
    tri = keep.astype(BF16)
    lane_lo = lax.broadcasted_iota(jnp.int32, (q, LANES), 1) < SSD_HEADDIM
    lane_lo1 = lane_lo[0:1, :]
    d_on = jnp.where(is_fwd, 1.0, 0.0)
    neg_inf = jnp.float32(-jnp.inf)

    def chunk_body(ci, carry):
        c = jnp.where(is_fwd, ci, n_chunks - 1 - ci)
        c0 = pl.multiple_of(c * q, q)
        dt = _softplus(dt_ref[pl.ds(c0, q), :] + bias)
        acs = _dot_exact_lhs(tri, dt * a2_neg)
        acs_t = acs.T
        dt_t = dt.T
        tot_row = jnp.where(is_fwd, acs[q - 1:q, :], acs[0:1, :])
        tot_col = jnp.where(is_fwd, acs_t[:, q - 1:q], acs_t[:, 0:1])
        e_tot = jnp.exp2(tot_row)
        src_t = acs_t - jnp.log2(dt_t)
        dte_t = jnp.exp2(tot_col - acs_t) * dt_t

        for g in range(SSD_GROUPS):
            bg = b_ref[pl.ds(c0, q), g * SSD_STATE:(g + 1) * SSD_STATE]
            cg = c_ref[pl.ds(c0, q), g * SSD_STATE:(g + 1) * SSD_STATE]
            cb = lax.dot_general(cg, bg, (((1,), (1,)), ((), ())),
                                 preferred_element_type=F32)
            cgf = cg.astype(F32)
            bg_t = bg.astype(F32).T
            for p in range(SSD_HPG // 2):
                h0 = g * SSD_HPG + 2 * p
                lo = h0 * SSD_HEADDIM
                xp = xs_ref[pl.ds(c0, q), lo:lo + LANES]
                htp = ht_scr[:, lo:lo + LANES]
                rhs = jnp.concatenate([xp, htp.astype(BF16)], axis=0)
                ys, sts = [], []
                for hh in (h0, h0 + 1):
                    tgt = jnp.broadcast_to(acs[:, hh:hh + 1], (q, q))
                    seg = jnp.where(keep, tgt - src_t[hh:hh + 1, :], neg_inf)
                    w = (cb * jnp.exp2(seg)).astype(BF16)
                    ce = (cgf * jnp.exp2(tgt)).astype(BF16)
                    ys.append(_dot(jnp.concatenate([w, ce], axis=1), rhs))
                    sts.append(_dot((bg_t * dte_t[hh:hh + 1, :]).astype(BF16), xp))
                y = (jnp.where(lane_lo, ys[0], ys[1])
                     + (d_on * dvec_ref[:, lo:lo + LANES]) * xp.astype(F32))
                y_ref[0, pl.ds(c0, q), lo:lo + LANES] = y.astype(BF16)
                dec = jnp.where(lane_lo1, e_tot[:, h0:h0 + 1], e_tot[:, h0 + 1:h0 + 2])
                ht_scr[:, lo:lo + LANES] = htp * dec + jnp.where(lane_lo, sts[0], sts[1])
        return carry

    lax.fori_loop(0, n_chunks, chunk_body, 0)

    if want_final:
        @pl.when(st == pl.num_programs(2) - 1)
        def _():
            for j in range(n_blk):
                hfin_ref[0, 0, j * LANES:(j + 1) * LANES, :] = ht_scr[:, j * LANES:(j + 1) * LANES].T


def _ssd_call(xbc, dt_raw, h0, a_log, dt_bias, dvec, tok0, n_seq, seq_len, want_final):
    ts = min(seq_len, TS_SSD)
    assert seq_len % ts == 0 and tok0 % ts == 0 and ts % SSD_CHUNK == 0
    n_steps = seq_len // ts
    blk0 = tok0 // ts
    bc_w = SSD_GROUPS * SSD_STATE

    def tok_blk(b, d, s):
        return blk0 + b * n_steps + jnp.where(d == 0, s, n_steps - 1 - s)

    state_spec = pl.BlockSpec((1, 1, D_INNER, SSD_STATE), lambda b, d, s: (b, d, 0, 0))
    in_specs = [
        pl.BlockSpec((ts, D_INNER), lambda b, d, s: (tok_blk(b, d, s), 0)),
        pl.BlockSpec((ts, bc_w), lambda b, d, s: (tok_blk(b, d, s), D_INNER // bc_w)),
        pl.BlockSpec((ts, bc_w), lambda b, d, s: (tok_blk(b, d, s), D_INNER // bc_w + 1)),
        pl.BlockSpec((ts, LANES), lambda b, d, s: (tok_blk(b, d, s), 0)),
    ]
    args = [xbc, xbc, xbc, dt_raw]
    if h0 is not None:
        in_specs.append(state_spec)
        args.append(h0)
    in_specs += [
        pl.BlockSpec((1, 1, LANES), lambda b, d, s: (d, 0, 0)),
        pl.BlockSpec((1, 1, LANES), lambda b, d, s: (d, 0, 0)),
        pl.BlockSpec((1, D_INNER), lambda b, d, s: (0, 0)),
    ]
    args += [a_log, dt_bias, dvec]
    out_shape = [jax.ShapeDtypeStruct((2, n_seq * seq_len, D_INNER), BF16)]
    out_specs = [pl.BlockSpec((1, ts, D_INNER),
                              lambda b, d, s: (d, tok_blk(b, d, s) - blk0, 0))]
    if want_final:
        out_shape.append(jax.ShapeDtypeStruct((2, n_seq, D_INNER, SSD_STATE), F32))
        out_specs.append(pl.BlockSpec((1, 1, D_INNER, SSD_STATE), lambda b, d, s: (d, b, 0, 0)))
    kern = functools.partial(_ssd_kernel, has_h0=h0 is not None, want_final=want_final)
    return pl.pallas_call(
        kern,
        out_shape=tuple(out_shape),
        grid=(n_seq, 2, n_steps),
        in_specs=in_specs,
        out_specs=tuple(out_specs),
        scratch_shapes=[pltpu.VMEM((SSD_STATE, D_INNER), F32)],
        compiler_params=_cparams(("arbitrary", "arbitrary", "arbitrary")),
        name="ssd_final" if want_final else "ssd",
    )(*args)


def _post_kernel(xc_ref, xl_ref, mod_ref, z_ref, scb_ref, scc_ref, scv_ref, gssd_ref, gsc_ref,
                 yc_ref, yl_ref, normw_ref, wssd_ref, scw_ref, wsc_ref, wo_ref, g1_ref, b1_ref,
                 wr_ref, br_ref, x1_ref, h2l_ref, ti_ref, tp_ref, cnt_ref,
                 *, n_ctx, ctx_len, lat_len):
    i = pl.program_id(0)
    tm = TM_POST
    r = _mod_row(i, tm, n_ctx, lat_len)
    is_ctx = i * tm < n_ctx

    def mod_vec(k):
        return mod_ref[pl.ds(r, 1), k * D_MODEL:(k + 1) * D_MODEL]

    gate1, shift2, scale2 = mod_vec(2), mod_vec(3), mod_vec(4)

    y_f = jnp.where(is_ctx, yc_ref[0], yl_ref[0]).astype(F32)
    y_b = jnp.where(is_ctx, yc_ref[1], yl_ref[1]).astype(F32)
    y = (y_f + y_b) * _silu(z_ref[...].astype(F32))
    gw = D_INNER // SSD_GROUPS
    parts = []
    for g in range(SSD_GROUPS):
        yg = y[:, g * gw:(g + 1) * gw]
        ms = jnp.mean(yg * yg, axis=-1, keepdims=True)
        parts.append(yg * lax.rsqrt(ms + RMS_EPS))
    yn = (jnp.concatenate(parts, axis=-1) * normw_ref[...]).astype(BF16)
    y_ssd = _dot(yn, wssd_ref[...])

    row_len = jnp.where(i * tm < n_ctx, ctx_len, GRID_W)
    pos = _row_pos(tm, row_len)
    u = scc_ref[...].astype(F32) * scv_ref[...].astype(F32)
    half = SC_CONV // 2
    cv = u * scw_ref[half:half + 1, :]
    for k in range(SC_CONV):
        if k != half:
            cv = cv + _shift_rows(u, k - half, pos, row_len) * scw_ref[k:k + 1, :]
    y_sc = _dot((scb_ref[...].astype(F32) * cv).astype(BF16), wsc_ref[...])

    mixed = (jax.nn.sigmoid(gssd_ref[...].astype(F32)) * y_ssd
             + jax.nn.sigmoid(gsc_ref[...].astype(F32)) * y_sc)
    o = _dot(mixed.astype(BF16), wo_ref[...])

    x = jnp.where(is_ctx, xc_ref[...], xl_ref[...])
    res = DEEPNORM_ALPHA * x + gate1 * o
    mu = jnp.mean(res, axis=-1, keepdims=True)
    cen = res - mu
    var = jnp.mean(cen * cen, axis=-1, keepdims=True)
    x1 = cen * lax.rsqrt(var + LN_EPS) * g1_ref[...] + b1_ref[...]
    x1_ref[...] = x1
    h2 = x1 * (1.0 + scale2) + shift2

    logits = _dot_x3(h2, wr_ref[...]) + br_ref[...]
    lane = lax.broadcasted_iota(jnp.int32, (tm, LANES), 1)
    lane_f = lane.astype(F32)
    neg = jnp.float32(-jnp.inf)
    work = jnp.where(lane < N_EXPERTS, logits, neg)
    vals, idxs, hits = [], [], []
    for _ in range(TOP_K):
        m = jnp.max(work, axis=-1, keepdims=True)
        idx = jnp.min(jnp.where(work == m, lane_f, float(LANES)), axis=-1, keepdims=True)
        hit = lane_f == idx
        vals.append(m)
        idxs.append(idx)
        hits.append(hit)
        work = jnp.where(hit, neg, work)
    es = [jnp.exp(v - vals[0]) for v in vals]
    denom = es[0] + es[1] + es[2] + es[3]

    chosen = jnp.where(hits[0] | hits[1] | hits[2] | hits[3], 1.0, 0.0)
    cnt = jnp.sum(chosen, axis=0, keepdims=True)
    cnt_ref[0] = cnt.astype(jnp.int32)
    cpad = jnp.floor((cnt + (RUN_ALIGN - 1)) * (1.0 / RUN_ALIGN)) * RUN_ALIGN
    er = lax.broadcasted_iota(jnp.int32, (LANES, LANES), 0)
    ec = lax.broadcasted_iota(jnp.int32, (LANES, LANES), 1)
    run_off = _dot(jnp.broadcast_to(cpad, (8, LANES)).astype(BF16), (er < ec).astype(BF16))[0:1]
    rr = lax.broadcasted_iota(jnp.int32, (tm, tm), 0)
    cc = lax.broadcasted_iota(jnp.int32, (tm, tm), 1)
    slot = _dot((rr > cc).astype(BF16), chosen.astype(BF16)) + run_off
    lrows = [jnp.sum(jnp.where(hits[k], slot, 0.0), axis=-1, keepdims=True) for k in range(TOP_K)]

    ti = jnp.zeros((tm, LANES), F32)
    tp = jnp.zeros((tm, LANES), F32)
    lmat = jnp.full((tm, LANES), -1.0, F32)
    for k in range(TOP_K):
        ti = jnp.where(lane == k, idxs[k], ti)
        ti = jnp.where(lane == TOP_K + k, lrows[k], ti)
        lmat = jnp.where(lane == k, lrows[k], lmat)
        tp = jnp.where(lane == k, es[k] / denom, tp)
    ti_ref[...] = ti.astype(jnp.int32)
    tp_ref[...] = tp

    lrow_t = jnp.concatenate([lmat[j * LANES:(j + 1) * LANES, :].T for j in range(tm // LANES)],
                             axis=1)
    jrow = lax.broadcasted_iota(jnp.int32, (LOCAL_ROWS, tm), 0).astype(F32)
    sel_t = jnp.where(jrow == lrow_t[0:1, :], 1.0, 0.0)
    for k in range(1, TOP_K):
        sel_t = sel_t + jnp.where(jrow == lrow_t[k:k + 1, :], 1.0, 0.0)
    h2l_ref[...] = _dot(sel_t.astype(BF16), h2.astype(BF16))


def _post_call(x_ctx, x_lat, mod, proj, y_ctx, y_lat, norm_w, w_ssd, sc_w, w_sc, w_o, ln_g, ln_b,
               w_r, b_r, ctx_len, lat_len):
    n_ctx = x_ctx.shape[0]
    t = n_ctx + x_lat.shape[0]
    tm = TM_POST
    kern = functools.partial(_post_kernel, n_ctx=n_ctx, ctx_len=ctx_len, lat_len=lat_len)

    def colblk(off, width):
        return pl.BlockSpec((tm, width), lambda i: (i, off // width))

    def whole(a):
        return pl.BlockSpec(a.shape, lambda i: (0,) * a.ndim, pipeline_mode=pl.Buffered(1))

    return pl.pallas_call(
        kern,
        out_shape=(jax.ShapeDtypeStruct((t, D_MODEL), F32),
                   jax.ShapeDtypeStruct((t // tm * LOCAL_ROWS, D_MODEL), F32),
                   jax.ShapeDtypeStruct((t, LANES), jnp.int32),
                   jax.ShapeDtypeStruct((t, LANES), F32),
                   jax.ShapeDtypeStruct((t // tm, 1, LANES), jnp.int32)),
        grid=(t // tm,),
        in_specs=[
            *_ctx_lat_specs((tm, D_MODEL), n_ctx // tm),
            whole(mod),
            colblk(COL_Z, D_INNER),
            colblk(COL_SCB, D_MODEL), colblk(COL_SCC, D_MODEL), colblk(COL_SCV, D_MODEL),
            colblk(COL_GSSD, D_MODEL), colblk(COL_GSC, D_MODEL),
            *_ctx_lat_specs((2, tm, D_INNER), n_ctx // tm, lead=(0,)),
            whole(norm_w), whole(w_ssd), whole(sc_w), whole(w_sc), whole(w_o),
            whole(ln_g), whole(ln_b), whole(w_r), whole(b_r),
        ],
        out_specs=(pl.BlockSpec((tm, D_MODEL), lambda i: (i, 0)),
                   pl.BlockSpec((LOCAL_ROWS, D_MODEL), lambda i: (i, 0)),
                   pl.BlockSpec((tm, LANES), lambda i: (i, 0)),
                   pl.BlockSpec((tm, LANES), lambda i: (i, 0)),
                   pl.BlockSpec((1, 1, LANES), lambda i: (i, 0, 0))),
        compiler_params=_cparams(("arbitrary",)),
        name="post",
    )(x_ctx, x_lat, mod, proj, proj, proj, proj, proj, proj, y_ctx, y_lat,
      norm_w, w_ssd, sc_w, w_sc, w_o, ln_g, ln_b, w_r, b_r)


def _ffn_kernel(nlive_ref, exp_ref, nv_ref, half_ref, wslot_ref, wnext_ref,
                src_cur_ref, src_nxt_ref, dst_cur_ref,
                h2l_hbm, wgu_hbm, bgu_ref, wd_hbm, bd_ref,
                yl_hbm, wgu_f32, wd_f32, wgu_scr, wd_scr, x_buf, y_buf, sem_x, sem_y, sem_w):
    s = pl.program_id(0)
    par = s & 1
    n_live = nlive_ref[0]
    nv = nv_ref[s]
    new_expert = (s == 0) | (exp_ref[s] != exp_ref[jnp.maximum(s - 1, 0)])
    wslot = wslot_ref[s]

    def weight_copies(expert, slot):
        return (pltpu.make_async_copy(wgu_hbm.at[expert], wgu_f32.at[slot], sem_w.at[slot]),
                pltpu.make_async_copy(wd_hbm.at[expert], wd_f32.at[slot], sem_w.at[slot]))

    def gather(tbl_ref, slot):
        for c in range(CHUNKS_PER_TILE):
            src = pl.multiple_of(tbl_ref[0, 0, c] * RUN_ALIGN, RUN_ALIGN)
            pltpu.make_async_copy(h2l_hbm.at[pl.ds(src, RUN_ALIGN)],
                                  x_buf.at[slot, pl.ds(c * RUN_ALIGN, RUN_ALIGN)],
                                  sem_x.at[slot]).start()

    def out_copy(slot, c, dst):
        return pltpu.make_async_copy(y_buf.at[slot, pl.ds(c * RUN_ALIGN, RUN_ALIGN)],
                                     yl_hbm.at[pl.ds(dst, RUN_ALIGN)], sem_y.at[slot])

    def wait_out(slot, count):
        rows = pl.multiple_of(count * RUN_ALIGN, RUN_ALIGN)
        pltpu.make_async_copy(y_buf.at[slot, pl.ds(0, rows)], yl_hbm.at[pl.ds(0, rows)],
                              sem_y.at[slot]).wait()

    @pl.when(s == 0)
    def _():
        gather(src_cur_ref, 0)
        for cp in weight_copies(exp_ref[0], 0):
            cp.start()

    @pl.when(s < n_live)
    def _():
        pltpu.make_async_copy(h2l_hbm.at[pl.ds(0, TM_FFN)], x_buf.at[par], sem_x.at[par]).wait()

        @pl.when(new_expert)
        def _():
            for cp in weight_copies(0, wslot):
                cp.wait()
            wgu_scr[...] = wgu_f32[wslot].astype(BF16)
            wd_scr[...] = wd_f32[wslot].astype(BF16)

            @pl.when(wnext_ref[s] >= 0)
            def _():
                for cp in weight_copies(wnext_ref[s], 1 - wslot):
                    cp.start()

        def tile(rows):
            gather(src_nxt_ref, 1 - par)
            gu = _dot(x_buf[par, 0:rows, :].astype(BF16), wgu_scr[...]) + bgu_ref[0]
            g = jnp.minimum(gu[:, :D_EXPERT], SWIGLU_LIMIT)
            u = jnp.clip(gu[:, D_EXPERT:], -SWIGLU_LIMIT, SWIGLU_LIMIT)
            act = (u + 1.0) * g * jax.nn.sigmoid(SWIGLU_ALPHA * g)
            y = _dot(act.astype(BF16), wd_scr[...]) + bd_ref[0]

            @pl.when(s >= 2)
            def _():
                wait_out(par, nv_ref[jnp.maximum(s - 2, 0)])

            y_buf[par, 0:rows, :] = y

        @pl.when(half_ref[s] == 1)
        def _():
            tile(TM_FFN // 2)

        @pl.when(half_ref[s] == 0)
        def _():
            tile(TM_FFN)

        for grp in range(CHUNKS_PER_TILE // SCATTER_GROUP):
            @pl.when(grp * SCATTER_GROUP < nv)
            def _():
                for c in range(grp * SCATTER_GROUP, (grp + 1) * SCATTER_GROUP):
                    dst = pl.multiple_of(dst_cur_ref[0, 0, c] * RUN_ALIGN, RUN_ALIGN)
                    out_copy(par, c, dst).start()

        @pl.when(s == n_live - 1)
        def _():
            pltpu.make_async_copy(h2l_hbm.at[pl.ds(0, TM_FFN)], x_buf.at[1 - par],
                                  sem_x.at[1 - par]).wait()
            wait_out(par, nv)

            @pl.when(s >= 1)
            def _():
                wait_out(1 - par, nv_ref[jnp.maximum(s - 1, 0)])


def _ffn_call(tables, h2l, w_gu, b_gu, w_d, b_d):
    n_live, exp_id, n_valid, is_half, w_slot, w_next, src8, dst8 = tables
    n_tiles = src8.shape[0]

    def tbl_spec(index_fn):
        return pl.BlockSpec((1, 1, CHUNKS_PER_TILE), index_fn, memory_space=pltpu.SMEM)

    def per_expert(shape):
        return pl.BlockSpec((1,) + shape, lambda s, nl, ei, *_: (ei[s], 0, 0))

    grid_spec = pltpu.PrefetchScalarGridSpec(
        num_scalar_prefetch=6,
        grid=(n_tiles,),
        in_specs=[
            tbl_spec(lambda s, *_: (s, 0, 0)),
            tbl_spec(lambda s, *_: (jnp.minimum(s + 1, n_tiles - 1), 0, 0)),
            tbl_spec(lambda s, *_: (s, 0, 0)),
            pl.BlockSpec(memory_space=pl.ANY),
            pl.BlockSpec(memory_space=pl.ANY), per_expert((1, 2 * D_EXPERT)),
            pl.BlockSpec(memory_space=pl.ANY), per_expert((1, D_MODEL)),
        ],
        out_specs=pl.BlockSpec(memory_space=pl.ANY),
        scratch_shapes=[pltpu.VMEM((2, D_MODEL, 2 * D_EXPERT), F32),
                        pltpu.VMEM((2, D_EXPERT, D_MODEL), F32),
                        pltpu.VMEM((D_MODEL, 2 * D_EXPERT), BF16),
                        pltpu.VMEM((D_EXPERT, D_MODEL), BF16),
                        pltpu.VMEM((2, TM_FFN, D_MODEL), F32),
                        pltpu.VMEM((2, TM_FFN, D_MODEL), F32),
                        pltpu.SemaphoreType.DMA((2,)),
                        pltpu.SemaphoreType.DMA((2,)),
                        pltpu.SemaphoreType.DMA((2,))],
    )
    tables_and_chunks = (n_live, exp_id, n_valid, is_half, w_slot, w_next, src8, src8, dst8)
    return pl.pallas_call(
        _ffn_kernel,
        out_shape=jax.ShapeDtypeStruct(h2l.shape, F32),
        grid_spec=grid_spec,
        input_output_aliases={len(tables_and_chunks): 0},
        compiler_params=pltpu.CompilerParams(dimension_semantics=("arbitrary",),
                                             vmem_limit_bytes=VMEM_LIMIT,
                                             has_side_effects=True),
        name="ffn",
    )(*tables_and_chunks, h2l, w_gu, b_gu, w_d, b_d)


def _combine_kernel(x1_ref, ti_ref, tp_ref, mod_ref, g2_ref, b2_ref, yl_ref,
                    oc_ref, ol_ref, *, n_ctx, lat_len):
    i = pl.program_id(0)
    tb = TM_POST

    jl = lax.broadcasted_iota(jnp.int32, (tb, LOCAL_ROWS), 1)
    ti = ti_ref[...]
    tp = tp_ref[...]
    sel = jnp.where(jl == ti[:, TOP_K:TOP_K + 1], tp[:, 0:1], 0.0)
    for k in range(1, TOP_K):
        sel = sel + jnp.where(jl == ti[:, TOP_K + k:TOP_K + k + 1], tp[:, k:k + 1], 0.0)
    moe = _dot(sel.astype(BF16), yl_ref[...].astype(BF16))

    r = _mod_row(i, tb, n_ctx, lat_len)
    gate2 = mod_ref[pl.ds(r, 1), 5 * D_MODEL:6 * D_MODEL]
    res = DEEPNORM_ALPHA * x1_ref[...] + gate2 * moe
    mu = jnp.mean(res, axis=-1, keepdims=True)
    cen = res - mu
    var = jnp.mean(cen * cen, axis=-1, keepdims=True)
    out = cen * lax.rsqrt(var + LN_EPS) * g2_ref[...] + b2_ref[...]

    @pl.when(i * tb < n_ctx)
    def _():
        oc_ref[...] = out

    @pl.when(i * tb >= n_ctx)
    def _():
        ol_ref[...] = out


def _combine_call(x1, route, tp, mod, ln_g, ln_b, y_local, n_ctx, lat_len):
    t = x1.shape[0]
    tb = TM_POST
    kern = functools.partial(_combine_kernel, n_ctx=n_ctx, lat_len=lat_len)
    return pl.pallas_call(
        kern,
        out_shape=(jax.ShapeDtypeStruct((n_ctx, D_MODEL), F32),
                   jax.ShapeDtypeStruct((t - n_ctx, D_MODEL), F32)),
        grid=(t // tb,),
        in_specs=[
            pl.BlockSpec((tb, D_MODEL), lambda i: (i, 0)),
            pl.BlockSpec((tb, LANES), lambda i: (i, 0)),
            pl.BlockSpec((tb, LANES), lambda i: (i, 0)),
            pl.BlockSpec(mod.shape, lambda i: (0, 0)),
            pl.BlockSpec((1, D_MODEL), lambda i: (0, 0)),
            pl.BlockSpec((1, D_MODEL), lambda i: (0, 0)),
            pl.BlockSpec((LOCAL_ROWS, D_MODEL), lambda i: (i, 0)),
        ],
        out_specs=_ctx_lat_specs((tb, D_MODEL), n_ctx // tb),
        compiler_params=_cparams(("arbitrary",)),
        name="combine",
    )(x1, route, tp, mod, ln_g, ln_b, y_local)


def _routing_tables(cnt):
    i32 = jnp.int32
    half_rows = TM_FFN // 2
    n_blocks = cnt.shape[0]
    max_rows = n_blocks * (TM_POST * TOP_K + N_EXPERTS * (RUN_ALIGN - 1)) + N_EXPERTS * (half_rows - RUN_ALIGN)
    n_tiles = -(-max_rows // TM_FFN) + N_EXPERTS // 2 + 1

    cpad = (cnt + (RUN_ALIGN - 1)) // RUN_ALIGN * RUN_ALIGN
    run_loc = jnp.cumsum(cpad, axis=1) - cpad
    used = cpad.sum(axis=0)
    region = (used + (half_rows - 1)) // half_rows * half_rows
    incl = jnp.cumsum(region)
    offs = incl - region
    tiles_e = (region + (TM_FFN - 1)) // TM_FFN
    t_end = jnp.cumsum(tiles_e)
    n_live = t_end[-1]
    run_glob = offs[None, :] + jnp.cumsum(cpad, axis=0) - cpad

    experts = jnp.arange(N_EXPERTS, dtype=i32)

    def per_expert_row(e_idx, table):
        onehot = (e_idx[:, None] == experts).astype(F32)
        return jnp.dot(onehot, table.astype(F32), precision=lax.Precision.HIGHEST).astype(i32)

    tile = jnp.arange(n_tiles, dtype=i32)
    tcl = jnp.minimum(tile, n_live - 1)
    e_of = jnp.minimum(jnp.sum(t_end[None, :] <= tcl[:, None], axis=1), N_EXPERTS - 1).astype(i32)
    mine = e_of[:, None] == experts

    def pick(v):
        return jnp.sum(jnp.where(mine, v[None, :], 0), axis=1)

    k_in_e = tcl - pick(t_end - tiles_e)
    row0 = pick(offs) + k_in_e * TM_FFN
    tile_rows = jnp.minimum(TM_FFN, pick(region) - k_in_e * TM_FFN)
    is_half = (tile_rows == half_rows).astype(i32)
    n_run = jnp.clip((pick(offs + used) - row0) // RUN_ALIGN, 0, tile_rows // RUN_ALIGN)
    n_out = (n_run + (SCATTER_GROUP - 1)) // SCATTER_GROUP * SCATTER_GROUP
    n_out = jnp.where(tile < n_live, n_out, 0).astype(i32)

    c_in_tile = jnp.tile(jnp.arange(CHUNKS_PER_TILE, dtype=i32), n_tiles)
    first_row = jnp.where(tile < n_live, row0, incl[-1])
    g_row = jnp.repeat(first_row, CHUNKS_PER_TILE) + c_in_tile * RUN_ALIGN
    g_exp = jnp.minimum(jnp.sum(incl[None, :] <= g_row[:, None], axis=1), N_EXPERTS - 1).astype(i32)
    rs = per_expert_row(g_exp, run_glob.T)
    re = rs + per_expert_row(g_exp, cpad.T)
    ls = per_expert_row(g_exp, (jnp.arange(n_blocks, dtype=i32)[:, None] * LOCAL_ROWS + run_loc).T)
    inside = (g_row[:, None] >= rs) & (g_row[:, None] < re)
    valid = jnp.any(inside, axis=1)
    local_row = g_row + jnp.sum(jnp.where(inside, ls - rs, 0), axis=1)
    src8 = jnp.where(valid, local_row // RUN_ALIGN, LOCAL_CHUNKS - 1).astype(i32)

    filler = jnp.logical_not(valid) & (c_in_tile < jnp.repeat(n_out, CHUNKS_PER_TILE))
    u = jnp.cumsum(filler.astype(i32)) - 1
    spare = (u // 3) * LOCAL_CHUNKS + (LOCAL_CHUNKS - 4) + u % 3
    dst8 = jnp.where(valid, local_row // RUN_ALIGN, jnp.where(filler, spare, 0)).astype(i32)

    has_rows = used > 0
    ordinal = jnp.cumsum(has_rows.astype(i32)) - 1
    w_slot = (jnp.sum(jnp.where(e_of[:, None] == experts, ordinal[None, :], 0), axis=1) % 2).astype(i32)
    later = (experts[None, :] > e_of[:, None]) & has_rows[None, :]
    w_next = jnp.min(jnp.where(later, experts[None, :], N_EXPERTS), axis=1)
    w_next = jnp.where(w_next < N_EXPERTS, w_next, -1).astype(i32)
    shape = (n_tiles, 1, CHUNKS_PER_TILE)
    return (n_live.reshape(1).astype(i32), e_of, n_out, is_half, w_slot, w_next,
            src8.reshape(shape), dst8.reshape(shape))


def kernel(x_prompt, x_sample, c, state_ssd_fwd, state_ssd_bwd, c_ctx, w_ada, b_ada, w_in,
           ssd_conv_w, ssd_conv_b, ssd_a_log, ssd_dt_bias, ssd_d, ssd_norm_w, ssd_w_out,
           sc_conv_w, sc_w_out, w_o, ln1_g, ln1_b, w_router, b_router, w_gate_up, b_gate_up,
           w_down, b_down, ln2_g, ln2_b):
    n_ctx_req, ctx_len, _ = x_prompt.shape
    n_lat_req, lat_len, _ = x_sample.shape
    n_ctx = n_ctx_req * ctx_len
    n_lat = n_lat_req * lat_len
    t = n_ctx + n_lat
    assert w_ada.shape[0] == 1, "single trunk layer"
    assert ctx_len % SSD_CHUNK == 0 and lat_len % TS_SSD == 0 and SSD_CHUNK % GRID_W == 0
    assert n_ctx % TM_PROJ == 0 and lat_len % TM_PROJ == 0 and TM_POST % ctx_len == 0
    assert CONV_SUB % ctx_len == 0 and CONV_SUB % GRID_W == 0 and n_ctx % TB_CONV == 0
    assert ctx_len & (ctx_len - 1) == 0 and GRID_W & (GRID_W - 1) == 0

    x_ctx = x_prompt.reshape(n_ctx, D_MODEL)
    x_lat = x_sample.reshape(n_lat, D_MODEL)

    cvec = jnp.concatenate([c_ctx[None, :], c, jnp.zeros((7 - n_lat_req, D_MODEL), F32)], 0)
    mod = _mod_call(cvec, w_ada[0], b_ada[0])

    w = w_in[0]
    o_dt = D_INNER + XBC_DIM
    w_main = jnp.concatenate([w[:, :o_dt], w[:, o_dt + SSD_HEADS:]], axis=1).astype(BF16)
    w_dt = jnp.pad(w[:, o_dt:o_dt + SSD_HEADS], ((0, 0), (0, LANES - SSD_HEADS))).astype(BF16)
    proj, dt_raw = _inproj_call(x_ctx, x_lat, mod, w_main, w_dt, lat_len)

    xbc = _conv_call(proj, ssd_conv_w[0], ssd_conv_b[0], n_ctx, ctx_len)

    pad_h = ((0, 0), (0, LANES - SSD_HEADS))
    a_log = jnp.pad(ssd_a_log[0], pad_h).reshape(2, 1, LANES)
    dt_bias = jnp.pad(ssd_dt_bias[0], pad_h).reshape(2, 1, LANES)
    dvec = jnp.repeat(ssd_d[0], SSD_HEADDIM).reshape(1, D_INNER)
    h0_lat = jnp.stack([state_ssd_fwd[:, 0].reshape(n_lat_req, D_INNER, SSD_STATE),
                        state_ssd_bwd[:, 0].reshape(n_lat_req, D_INNER, SSD_STATE)], axis=1)
    y_ctx, h_ctx = _ssd_call(xbc, dt_raw, None, a_log, dt_bias, dvec, 0, n_ctx_req, ctx_len, True)
    (y_lat,) = _ssd_call(xbc, dt_raw, h0_lat, a_log, dt_bias, dvec, n_ctx, n_lat_req, lat_len, False)

    w_r = jnp.pad(w_router[0], ((0, 0), (0, LANES - N_EXPERTS)))
    b_r = jnp.pad(b_router[0], (0, LANES - N_EXPERTS)).reshape(1, LANES)
    x1, h2_local, route, top_p, cnt = _post_call(
        x_ctx, x_lat, mod, proj, y_ctx, y_lat, ssd_norm_w[0].reshape(1, D_INNER),
        ssd_w_out[0].astype(BF16), sc_conv_w[0], sc_w_out[0].astype(BF16), w_o[0].astype(BF16),
        ln1_g[0].reshape(1, D_MODEL), ln1_b[0].reshape(1, D_MODEL), w_r, b_r, ctx_len, lat_len)

    tables = _routing_tables(cnt[:, 0, :N_EXPERTS])
    y_local = _ffn_call(tables, h2_local,
                        w_gate_up[0], b_gate_up[0].reshape(N_EXPERTS, 1, 2 * D_EXPERT),
                        w_down[0], b_down[0].reshape(N_EXPERTS, 1, D_MODEL))
    out_ctx, out_lat = _combine_call(x1, route, top_p, mod, ln2_g[0].reshape(1, D_MODEL),
                                     ln2_b[0].reshape(1, D_MODEL), y_local, n_ctx, lat_len)

    state_shape = (n_ctx_req, 1, SSD_HEADS, SSD_HEADDIM, SSD_STATE)
    return (out_ctx.reshape(n_ctx_req, ctx_len, D_MODEL), out_lat.reshape(n_lat_req, lat_len, D_MODEL),
            h_ctx[0].reshape(state_shape), h_ctx[1].reshape(state_shape))
```

```python
import functools

import jax
import jax.numpy as jnp
from jax import lax
from jax.experimental import pallas as pl
from jax.experimental.pallas import tpu as pltpu

F32 = jnp.float32
BF16 = jnp.bfloat16

D_MODEL = 1024
GRID_W = 64
D_INNER = 2048
SSD_HEADDIM = 64
SSD_HEADS = 32
SSD_GROUPS = 4
SSD_HPG = 8
SSD_STATE = 128
SSD_CONV = 5
SSD_CHUNK = 128
XBC_DIM = D_INNER + 2 * SSD_GROUPS * SSD_STATE
SC_CONV = 3
N_EXPERTS = 32
TOP_K = 4
D_EXPERT = 1024
SWIGLU_LIMIT = 7.0
SWIGLU_ALPHA = 1.702
LN_EPS = 1e-5
RMS_EPS = 1e-5
DEEPNORM_ALPHA = 2.0 ** 0.25
LOG2E = 1.4426950408889634

LANES = 128
MAIN_COLS = 10240
COL_Z = 0
COL_XBC = 2048
COL_SCB = 5120
COL_SCC = 6144
COL_SCV = 7168
COL_GSSD = 8192
COL_GSC = 9216

VMEM_LIMIT = 56 * 1024 * 1024

TM_PROJ = 1024
TN_PROJ = 2048
TB_CONV = 1024
CONV_SUB = 256
TS_SSD = 1024
TM_POST = 256
TM_FFN = 512
RUN_ALIGN = 8
CHUNKS_PER_TILE = TM_FFN // RUN_ALIGN
SCATTER_GROUP = 8
LOCAL_ROWS = TM_POST * TOP_K + N_EXPERTS * RUN_ALIGN
LOCAL_CHUNKS = LOCAL_ROWS // RUN_ALIGN


def _cparams(sem):
    return pltpu.CompilerParams(dimension_semantics=sem, vmem_limit_bytes=VMEM_LIMIT)


def _split3(v):
    hi = v.astype(BF16)
    r1 = v - hi.astype(F32)
    mid = r1.astype(BF16)
    lo = (r1 - mid.astype(F32)).astype(BF16)
    return hi, mid, lo


def _dot(a, b):
    return jnp.dot(a, b, preferred_element_type=F32)


def _dot_exact_lhs(m_bf16, v_f32):
    hi, mid, lo = _split3(v_f32)
    return _dot(m_bf16, hi) + _dot(m_bf16, mid) + _dot(m_bf16, lo)


def _dot_x3(a_f32, b_f32):
    a_hi = a_f32.astype(BF16)
    a_lo = (a_f32 - a_hi.astype(F32)).astype(BF16)
    b_hi = b_f32.astype(BF16)
    b_lo = (b_f32 - b_hi.astype(F32)).astype(BF16)
    return _dot(a_hi, b_hi) + _dot(a_lo, b_hi) + _dot(a_hi, b_lo)


def _silu(v):
    return v * jax.nn.sigmoid(v)


def _softplus(v):
    return jnp.maximum(v, 0.0) + jnp.log(1.0 + jnp.exp(-jnp.abs(v)))


def _mod_kernel(c_ref, w_ref, b_ref, o_ref):
    o_ref[...] = _dot_x3(_silu(c_ref[...]), w_ref[...]) + b_ref[...]


def _mod_call(cvec, w_ada, b_ada):
    n = w_ada.shape[1]
    tn = 1536
    return pl.pallas_call(
        _mod_kernel,
        out_shape=jax.ShapeDtypeStruct((cvec.shape[0], n), F32),
        grid=(n // tn,),
        in_specs=[
            pl.BlockSpec(cvec.shape, lambda j: (0, 0)),
            pl.BlockSpec((D_MODEL, tn), lambda j: (0, j)),
            pl.BlockSpec((1, tn), lambda j: (0, j)),
        ],
        out_specs=pl.BlockSpec((cvec.shape[0], tn), lambda j: (0, j)),
        compiler_params=_cparams(("arbitrary",)),
        name="mod",
    )(cvec, w_ada, b_ada.reshape(1, n))


def _mod_row(block, rows_per_block, n_ctx_tokens, lat_len):
    tok = block * rows_per_block
    return jnp.where(tok < n_ctx_tokens, 0, 1 + (tok - n_ctx_tokens) // lat_len)


def _ctx_lat_specs(block_shape, n_ctx_blocks, lead=()):
    tail = (0,) * (len(block_shape) - len(lead) - 1)
    ctx = pl.BlockSpec(block_shape, lambda i, *_: lead + (jnp.minimum(i, n_ctx_blocks - 1),) + tail)
    lat = pl.BlockSpec(block_shape, lambda i, *_: lead + (jnp.maximum(i - n_ctx_blocks, 0),) + tail)
    return ctx, lat


def _shift_rows(x, off, pos, row_len):
    if off == 0:
        return x
    n = x.shape[0]
    rolled = pltpu.roll(x, (-off) % n, 0)
    ok = (pos + off >= 0) & (pos + off < row_len)
    return jnp.where(ok, rolled, 0.0)


def _row_pos(n, row_len):
    t = lax.broadcasted_iota(jnp.int32, (n, 1), 0)
    return jnp.bitwise_and(t, row_len - 1)


def _inproj_kernel(xc_ref, xl_ref, mod_ref, w_ref, wdt_ref, o_ref, dt_ref, h_scr, *, n_ctx, lat_len):
    i = pl.program_id(0)
    j = pl.program_id(1)

    @pl.when(j == 0)
    def _():
        r = _mod_row(i, TM_PROJ, n_ctx, lat_len)
        shift = mod_ref[pl.ds(r, 1), 0:D_MODEL]
        scale = mod_ref[pl.ds(r, 1), D_MODEL:2 * D_MODEL]
        x = jnp.where(i * TM_PROJ < n_ctx, xc_ref[...], xl_ref[...])
        h = (x * (1.0 + scale) + shift).astype(BF16)
        h_scr[...] = h
        dt_ref[...] = _dot(h, wdt_ref[...])

    o_ref[...] = _dot(h_scr[...], w_ref[...]).astype(BF16)


def _inproj_call(x_ctx, x_lat, mod, w_main, w_dt, lat_len):
    n_ctx = x_ctx.shape[0]
    t = n_ctx + x_lat.shape[0]
    kern = functools.partial(_inproj_kernel, n_ctx=n_ctx, lat_len=lat_len)
    return pl.pallas_call(
        kern,
        out_shape=(jax.ShapeDtypeStruct((t, MAIN_COLS), BF16),
                   jax.ShapeDtypeStruct((t, LANES), F32)),
        grid=(t // TM_PROJ, MAIN_COLS // TN_PROJ),
        in_specs=[
            *_ctx_lat_specs((TM_PROJ, D_MODEL), n_ctx // TM_PROJ),
            pl.BlockSpec(mod.shape, lambda i, j: (0, 0)),
            pl.BlockSpec((D_MODEL, TN_PROJ), lambda i, j: (0, j)),
            pl.BlockSpec((D_MODEL, LANES), lambda i, j: (0, 0)),
        ],
        out_specs=(pl.BlockSpec((TM_PROJ, TN_PROJ), lambda i, j: (i, j)),
                   pl.BlockSpec((TM_PROJ, LANES), lambda i, j: (i, 0))),
        scratch_shapes=[pltpu.VMEM((TM_PROJ, D_MODEL), BF16)],
        compiler_params=_cparams(("arbitrary", "arbitrary")),
        name="inproj",
    )(x_ctx, x_lat, mod, w_main, w_dt)


def _conv_kernel(x_ref, s_ref, w_ref, b_ref, o_ref):
    half = SSD_CONV // 2
    for r in range(0, TB_CONV, CONV_SUB):
        x = x_ref[r:r + CONV_SUB, :]
        acc = x.astype(F32) * w_ref[half:half + 1, :]
        for k in range(SSD_CONV):
            if k != half:
                tap = k if k < half else k - 1
                acc = acc + _dot(s_ref[0, tap], x) * w_ref[k:k + 1, :]
        o_ref[r:r + CONV_SUB, :] = _silu(acc + b_ref[...]).astype(BF16)


def _shift_matrices(n, row_lens, width):
    t = jnp.arange(n, dtype=jnp.int32)
    half = width // 2
    mats = []
    for row_len in row_lens:
        taps = []
        for off in [o for o in range(-half, half + 1) if o != 0]:
            pos = t % row_len + off
            hit = (t[None, :] == t[:, None] + off) & ((pos >= 0) & (pos < row_len))[:, None]
            taps.append(hit)
        mats.append(jnp.stack(taps))
    return jnp.stack(mats).astype(BF16)


def _conv_call(proj, conv_w, conv_b, n_ctx, ctx_len):
    t = proj.shape[0]
    tc = 1024
    n_ctx_blocks = n_ctx // TB_CONV
    shifts = _shift_matrices(CONV_SUB, (ctx_len, GRID_W), SSD_CONV)
    return pl.pallas_call(
        _conv_kernel,
        out_shape=jax.ShapeDtypeStruct((t, XBC_DIM), BF16),
        grid=(t // TB_CONV, XBC_DIM // tc),
        in_specs=[
            pl.BlockSpec((TB_CONV, tc), lambda i, j: (i, COL_XBC // tc + j)),
            pl.BlockSpec((1, SSD_CONV - 1, CONV_SUB, CONV_SUB),
                         lambda i, j: (jnp.where(i < n_ctx_blocks, 0, 1), 0, 0, 0)),
            pl.BlockSpec((SSD_CONV, tc), lambda i, j: (0, j)),
            pl.BlockSpec((1, tc), lambda i, j: (0, j)),
        ],
        out_specs=pl.BlockSpec((TB_CONV, tc), lambda i, j: (i, j)),
        compiler_params=_cparams(("arbitrary", "arbitrary")),
        name="conv",
    )(proj, shifts, conv_w, conv_b.reshape(1, XBC_DIM))


def _ssd_kernel(*refs, has_h0, want_final):
    xs_ref, b_ref, c_ref, dt_ref = refs[:4]
    rest = list(refs[4:])
    h0_ref = rest.pop(0) if has_h0 else None
    alog_ref, bias_ref, dvec_ref, y_ref = rest[:4]
    hfin_ref = rest[4] if want_final else None
    ht_scr = rest[-1]
    d = pl.program_id(1)
    st = pl.program_id(2)
    q = SSD_CHUNK
    n_chunks = xs_ref.shape[0] // q
    n_blk = D_INNER // LANES
    is_fwd = d == 0

    @pl.when(st == 0)
    def _():
        if has_h0:
            for j in range(n_blk):
                ht_scr[:, j * LANES:(j + 1) * LANES] = h0_ref[0, 0, j * LANES:(j + 1) * LANES, :].T
        else:
            ht_scr[...] = jnp.zeros_like(ht_scr)

    a2_neg = -jnp.exp(alog_ref[0]) * LOG2E
    bias = bias_ref[0]
    row = lax.broadcasted_iota(jnp.int32, (q, q), 0)
    col = lax.broadcasted_iota(jnp.int32, (q, q), 1)
    keep = (row - col) * jnp.where(is_fwd, 1, -1) >= 0
    tri = keep.astype(BF16)
    lane_lo = lax.broadcasted_iota(jnp.int32, (q, LANES), 1) < SSD_HEADDIM
    lane_lo1 = lane_lo[0:1, :]
    d_on = jnp.where(is_fwd, 1.0, 0.0)
    neg_inf = jnp.float32(-jnp.inf)

    def chunk_body(ci, carry):
        c = jnp.where(is_fwd, ci, n_chunks - 1 - ci)
        c0 = pl.multiple_of(c * q, q)
        dt = _softplus(dt_ref[pl.ds(c0, q), :] + bias)
        acs = _dot_exact_lhs(tri, dt * a2_neg)
        acs_t = acs.T
        dt_t = dt.T
        tot_row = jnp.where(is_fwd, acs[q - 1:q, :], acs[0:1, :])
        tot_col = jnp.where(is_fwd, acs_t[:, q - 1:q], acs_t[:, 0:1])
        e_tot = jnp.exp2(tot_row)
        src_t = acs_t - jnp.log2(dt_t)
        dte_t = jnp.exp2(tot_col - acs_t) * dt_t

        for g in range(SSD_GROUPS):
            bg = b_ref[pl.ds(c0, q), g * SSD_STATE:(g + 1) * SSD_STATE]
            cg = c_ref[pl.ds(c0, q), g * SSD_STATE:(g + 1) * SSD_STATE]
            cb = lax.dot_general(cg, bg, (((1,), (1,)), ((), ())),
                                 preferred_element_type=F32)
            cgf = cg.astype(F32)
            bg_t = bg.astype(F32).T
            for p in range(SSD_HPG // 2):
                h0 = g * SSD_HPG + 2 * p
                lo = h0 * SSD_HEADDIM
                xp = xs_ref[pl.ds(c0, q), lo:lo + LANES]
                htp = ht_scr[:, lo:lo + LANES]
                rhs = jnp.concatenate([xp, htp.astype(BF16)], axis=0)
                ys, sts = [], []
                for hh in (h0, h0 + 1):
                    tgt = jnp.broadcast_to(acs[:, hh:hh + 1], (q, q))
                    seg = jnp.where(keep, tgt - src_t[hh:hh + 1, :], neg_inf)
                    w = (cb * jnp.exp2(seg)).astype(BF16)
                    ce = (cgf * jnp.exp2(tgt)).astype(BF16)
                    ys.append(_dot(jnp.concatenate([w, ce], axis=1), rhs))
                    sts.append(_dot((bg_t * dte_t[hh:hh + 1, :]).astype(BF16), xp))
                y = (jnp.where(lane_lo, ys[0], ys[1])
                     + (d_on * dvec_ref[:, lo:lo + LANES]) * xp.astype(F32))
                y_ref[0, pl.ds(c0, q), lo:lo + LANES] = y.astype(BF16)
                dec = jnp.where(lane_lo1, e_tot[:, h0:h0 + 1], e_tot[:, h0 + 1:h0 + 2])
                ht_scr[:, lo:lo + LANES] = htp * dec + jnp.where(lane_lo, sts[0], sts[1])
        return carry

    lax.fori_loop(0, n_chunks, chunk_body, 0)

    if want_final:
        @pl.when(st == pl.num_programs(2) - 1)
        def _():
            for j in range(n_blk):
                hfin_ref[0, 0, j * LANES:(j + 1) * LANES, :] = ht_scr[:, j * LANES:(j + 1) * LANES].T


def _ssd_call(xbc, dt_raw, h0, a_log, dt_bias, dvec, tok0, n_seq, seq_len, want_final):
    ts = min(seq_len, TS_SSD)
    assert seq_len % ts == 0 and tok0 % ts == 0 and ts % SSD_CHUNK == 0
    n_steps = seq_len // ts
    blk0 = tok0 // ts
    bc_w = SSD_GROUPS * SSD_STATE

    def tok_blk(b, d, s):
        return blk0 + b * n_steps + jnp.where(d == 0, s, n_steps - 1 - s)

    state_spec = pl.BlockSpec((1, 1, D_INNER, SSD_STATE), lambda b, d, s: (b, d, 0, 0))
    in_specs = [
        pl.BlockSpec((ts, D_INNER), lambda b, d, s: (tok_blk(b, d, s), 0)),
        pl.BlockSpec((ts, bc_w), lambda b, d, s: (tok_blk(b, d, s), D_INNER // bc_w)),
        pl.BlockSpec((ts, bc_w), lambda b, d, s: (tok_blk(b, d, s), D_INNER // bc_w + 1)),
        pl.BlockSpec((ts, LANES), lambda b, d, s: (tok_blk(b, d, s), 0)),
    ]
    args = [xbc, xbc, xbc, dt_raw]
    if h0 is not None:
        in_specs.append(state_spec)
        args.append(h0)
    in_specs += [
        pl.BlockSpec((1, 1, LANES), lambda b, d, s: (d, 0, 0)),
        pl.BlockSpec((1, 1, LANES), lambda b, d, s: (d, 0, 0)),
        pl.BlockSpec((1, D_INNER), lambda b, d, s: (0, 0)),
    ]
    args += [a_log, dt_bias, dvec]
    out_shape = [jax.ShapeDtypeStruct((2, n_seq * seq_len, D_INNER), BF16)]
    out_specs = [pl.BlockSpec((1, ts, D_INNER),
                              lambda b, d, s: (d, tok_blk(b, d, s) - blk0, 0))]
    if want_final:
        out_shape.append(jax.ShapeDtypeStruct((2, n_seq, D_INNER, SSD_STATE), F32))
        out_specs.append(pl.BlockSpec((1, 1, D_INNER, SSD_STATE), lambda b, d, s: (d, b, 0, 0)))
    kern = functools.partial(_ssd_kernel, has_h0=h0 is not None, want_final=want_final)
    return pl.pallas_call(
        kern,
        out_shape=tuple(out_shape),
        grid=(n_seq, 2, n_steps),
        in_specs=in_specs,
        out_specs=tuple(out_specs),
        scratch_shapes=[pltpu.VMEM((SSD_STATE, D_INNER), F32)],
        compiler_params=_cparams(("arbitrary", "arbitrary", "arbitrary")),
        name="ssd_final" if want_final else "ssd",
    )(*args)


def _post_kernel(xc_ref, xl_ref, mod_ref, z_ref, scb_ref, scc_ref, scv_ref, gssd_ref, gsc_ref,
                 yc_ref, yl_ref, normw_ref, wssd_ref, scw_ref, wsc_ref, wo_ref, g1_ref, b1_ref,
                 wr_ref, br_ref, x1_ref, h2l_ref, ti_ref, tp_ref, cnt_ref,
                 *, n_ctx, ctx_len, lat_len):
    i = pl.program_id(0)
    tm = TM_POST
    r = _mod_row(i, tm, n_ctx, lat_len)
    is_ctx = i * tm < n_ctx

    def mod_vec(k):
        return mod_ref[pl.ds(r, 1), k * D_MODEL:(k + 1) * D_MODEL]

    gate1, shift2, scale2 = mod_vec(2), mod_vec(3), mod_vec(4)

    y_f = jnp.where(is_ctx, yc_ref[0], yl_ref[0]).astype(F32)
    y_b = jnp.where(is_ctx, yc_ref[1], yl_ref[1]).astype(F32)
    y = (y_f + y_b) * _silu(z_ref[...].astype(F32))
    gw = D_INNER // SSD_GROUPS
    parts = []
    for g in range(SSD_GROUPS):
        yg = y[:, g * gw:(g + 1) * gw]
        ms = jnp.mean(yg * yg, axis=-1, keepdims=True)
        parts.append(yg * lax.rsqrt(ms + RMS_EPS))
    yn = (jnp.concatenate(parts, axis=-1) * normw_ref[...]).astype(BF16)
    y_ssd = _dot(yn, wssd_ref[...])

    row_len = jnp.where(i * tm < n_ctx, ctx_len, GRID_W)
    pos = _row_pos(tm, row_len)
    u = scc_ref[...].astype(F32) * scv_ref[...].astype(F32)
    half = SC_CONV // 2
    cv = u * scw_ref[half:half + 1, :]
    for k in range(SC_CONV):
        if k != half:
            cv = cv + _shift_rows(u, k - half, pos, row_len) * scw_ref[k:k + 1, :]
    y_sc = _dot((scb_ref[...].astype(F32) * cv).astype(BF16), wsc_ref[...])

    mixed = (jax.nn.sigmoid(gssd_ref[...].astype(F32)) * y_ssd
             + jax.nn.sigmoid(gsc_ref[...].astype(F32)) * y_sc)
    o = _dot(mixed.astype(BF16), wo_ref[...])

    x = jnp.where(is_ctx, xc_ref[...], xl_ref[...])
    res = DEEPNORM_ALPHA * x + gate1 * o
    mu = jnp.mean(res, axis=-1, keepdims=True)
    cen = res - mu
    var = jnp.mean(cen * cen, axis=-1, keepdims=True)
    x1 = cen * lax.rsqrt(var + LN_EPS) * g1_ref[...] + b1_ref[...]
    x1_ref[...] = x1
    h2 = x1 * (1.0 + scale2) + shift2

    logits = _dot_x3(h2, wr_ref[...]) + br_ref[...]
    lane = lax.broadcasted_iota(jnp.int32, (tm, LANES), 1)
    lane_f = lane.astype(F32)
    neg = jnp.float32(-jnp.inf)
    work = jnp.where(lane < N_EXPERTS, logits, neg)
    vals, idxs, hits = [], [], []
    for _ in range(TOP_K):
        m = jnp.max(work, axis=-1, keepdims=True)
        idx = jnp.min(jnp.where(work == m, lane_f, float(LANES)), axis=-1, keepdims=True)
        hit = lane_f == idx
        vals.append(m)
        idxs.append(idx)
        hits.append(hit)
        work = jnp.where(hit, neg, work)
    es = [jnp.exp(v - vals[0]) for v in vals]
    denom = es[0] + es[1] + es[2] + es[3]

    chosen = jnp.where(hits[0] | hits[1] | hits[2] | hits[3], 1.0, 0.0)
    cnt = jnp.sum(chosen, axis=0, keepdims=True)
    cnt_ref[0] = cnt.astype(jnp.int32)
    cpad = jnp.floor((cnt + (RUN_ALIGN - 1)) * (1.0 / RUN_ALIGN)) * RUN_ALIGN
    er = lax.broadcasted_iota(jnp.int32, (LANES, LANES), 0)
    ec = lax.broadcasted_iota(jnp.int32, (LANES, LANES), 1)
    run_off = _dot(jnp.broadcast_to(cpad, (8, LANES)).astype(BF16), (er < ec).astype(BF16))[0:1]
    rr = lax.broadcasted_iota(jnp.int32, (tm, tm), 0)
    cc = lax.broadcasted_iota(jnp.int32, (tm, tm), 1)
    slot = _dot((rr > cc).astype(BF16), chosen.astype(BF16)) + run_off
    lrows = [jnp.sum(jnp.where(hits[k], slot, 0.0), axis=-1, keepdims=True) for k in range(TOP_K)]

    ti = jnp.zeros((tm, LANES), F32)
    tp = jnp.zeros((tm, LANES), F32)
    lmat = jnp.full((tm, LANES), -1.0, F32)
    for k in range(TOP_K):
        ti = jnp.where(lane == k, idxs[k], ti)
        ti = jnp.where(lane == TOP_K + k, lrows[k], ti)
        lmat = jnp.where(lane == k, lrows[k], lmat)
        tp = jnp.where(lane == k, es[k] / denom, tp)
    ti_ref[...] = ti.astype(jnp.int32)
    tp_ref[...] = tp

    lrow_t = jnp.concatenate([lmat[j * LANES:(j + 1) * LANES, :].T for j in range(tm // LANES)],
                             axis=1)
    jrow = lax.broadcasted_iota(jnp.int32, (LOCAL_ROWS, tm), 0).astype(F32)
    sel_t = jnp.zeros((LOCAL_ROWS, tm), F32)
    for k in range(TOP_K):
        sel_t = jnp.where(jrow == lrow_t[k:k + 1, :], 1.0, sel_t)
    h2l_ref[...] = _dot(sel_t.astype(BF16), h2.astype(BF16))


def _post_call(x_ctx, x_lat, mod, proj, y_ctx, y_lat, norm_w, w_ssd, sc_w, w_sc, w_o, ln_g, ln_b,
               w_r, b_r, ctx_len, lat_len):
    n_ctx = x_ctx.shape[0]
    t = n_ctx + x_lat.shape[0]
    tm = TM_POST
    kern = functools.partial(_post_kernel, n_ctx=n_ctx, ctx_len=ctx_len, lat_len=lat_len)

    def colblk(off, width):
        return pl.BlockSpec((tm, width), lambda i: (i, off // width))

    def whole(a):
        return pl.BlockSpec(a.shape, lambda i: (0,) * a.ndim, pipeline_mode=pl.Buffered(1))

    return pl.pallas_call(
        kern,
        out_shape=(jax.ShapeDtypeStruct((t, D_MODEL), F32),
                   jax.ShapeDtypeStruct((t // tm * LOCAL_ROWS, D_MODEL), F32),
                   jax.ShapeDtypeStruct((t, LANES), jnp.int32),
                   jax.ShapeDtypeStruct((t, LANES), F32),
                   jax.ShapeDtypeStruct((t // tm, 1, LANES), jnp.int32)),
        grid=(t // tm,),
        in_specs=[
            *_ctx_lat_specs((tm, D_MODEL), n_ctx // tm),
            whole(mod),
            colblk(COL_Z, D_INNER),
            colblk(COL_SCB, D_MODEL), colblk(COL_SCC, D_MODEL), colblk(COL_SCV, D_MODEL),
            colblk(COL_GSSD, D_MODEL), colblk(COL_GSC, D_MODEL),
            *_ctx_lat_specs((2, tm, D_INNER), n_ctx // tm, lead=(0,)),
            whole(norm_w), whole(w_ssd), whole(sc_w), whole(w_sc), whole(w_o),
            whole(ln_g), whole(ln_b), whole(w_r), whole(b_r),
        ],
        out_specs=(pl.BlockSpec((tm, D_MODEL), lambda i: (i, 0)),
                   pl.BlockSpec((LOCAL_ROWS, D_MODEL), lambda i: (i, 0)),
                   pl.BlockSpec((tm, LANES), lambda i: (i, 0)),
                   pl.BlockSpec((tm, LANES), lambda i: (i, 0)),
                   pl.BlockSpec((1, 1, LANES), lambda i: (i, 0, 0))),
        compiler_params=_cparams(("arbitrary",)),
        name="post",
    )(x_ctx, x_lat, mod, proj, proj, proj, proj, proj, proj, y_ctx, y_lat,
      norm_w, w_ssd, sc_w, w_sc, w_o, ln_g, ln_b, w_r, b_r)


def _ffn_kernel(nlive_ref, exp_ref, nv_ref, half_ref, wslot_ref, wnext_ref,
                src_cur_ref, src_nxt_ref, dst_cur_ref,
                h2l_hbm, wgu_hbm, bgu_ref, wd_hbm, bd_ref,
                yl_hbm, wgu_f32, wd_f32, wgu_scr, wd_scr, x_buf, y_buf, sem_x, sem_y, sem_w):
    s = pl.program_id(0)
    par = s & 1
    n_live = nlive_ref[0]
    nv = nv_ref[s]
    new_expert = (s == 0) | (exp_ref[s] != exp_ref[jnp.maximum(s - 1, 0)])
    wslot = wslot_ref[s]

    def weight_copies(expert, slot):
        return (pltpu.make_async_copy(wgu_hbm.at[expert], wgu_f32.at[slot], sem_w.at[slot]),
                pltpu.make_async_copy(wd_hbm.at[expert], wd_f32.at[slot], sem_w.at[slot]))

    def gather(tbl_ref, slot):
        for c in range(CHUNKS_PER_TILE):
            src = pl.multiple_of(tbl_ref[0, 0, c] * RUN_ALIGN, RUN_ALIGN)
            pltpu.make_async_copy(h2l_hbm.at[pl.ds(src, RUN_ALIGN)],
                                  x_buf.at[slot, pl.ds(c * RUN_ALIGN, RUN_ALIGN)],
                                  sem_x.at[slot]).start()

    def out_copy(slot, c, dst):
        return pltpu.make_async_copy(y_buf.at[slot, pl.ds(c * RUN_ALIGN, RUN_ALIGN)],
                                     yl_hbm.at[pl.ds(dst, RUN_ALIGN)], sem_y.at[slot])

    def wait_out(slot, count):
        rows = pl.multiple_of(count * RUN_ALIGN, RUN_ALIGN)
        pltpu.make_async_copy(y_buf.at[slot, pl.ds(0, rows)], yl_hbm.at[pl.ds(0, rows)],
                              sem_y.at[slot]).wait()

    @pl.when(s == 0)
    def _():
        gather(src_cur_ref, 0)
        for cp in weight_copies(exp_ref[0], 0):
            cp.start()

    @pl.when(s < n_live)
    def _():
        pltpu.make_async_copy(h2l_hbm.at[pl.ds(0, TM_FFN)], x_buf.at[par], sem_x.at[par]).wait()

        @pl.when(new_expert)
        def _():
            for cp in weight_copies(0, wslot):
                cp.wait()
            wgu_scr[...] = wgu_f32[wslot].astype(BF16)
            wd_scr[...] = wd_f32[wslot].astype(BF16)

            @pl.when(wnext_ref[s] >= 0)
            def _():
                for cp in weight_copies(wnext_ref[s], 1 - wslot):
                    cp.start()

        def tile(rows):
            gather(src_nxt_ref, 1 - par)
            gu = _dot(x_buf[par, 0:rows, :].astype(BF16), wgu_scr[...]) + bgu_ref[0]
            g = jnp.minimum(gu[:, :D_EXPERT], SWIGLU_LIMIT)
            u = jnp.clip(gu[:, D_EXPERT:], -SWIGLU_LIMIT, SWIGLU_LIMIT)
            act = (u + 1.0) * g * jax.nn.sigmoid(SWIGLU_ALPHA * g)
            y = _dot(act.astype(BF16), wd_scr[...]) + bd_ref[0]

            @pl.when(s >= 2)
            def _():
                wait_out(par, nv_ref[jnp.maximum(s - 2, 0)])

            y_buf[par, 0:rows, :] = y

        @pl.when(half_ref[s] == 1)
        def _():
            tile(TM_FFN // 2)

        @pl.when(half_ref[s] == 0)
        def _():
            tile(TM_FFN)

        for grp in range(CHUNKS_PER_TILE // SCATTER_GROUP):
            @pl.when(grp * SCATTER_GROUP < nv)
            def _():
                for c in range(grp * SCATTER_GROUP, (grp + 1) * SCATTER_GROUP):
                    dst = pl.multiple_of(dst_cur_ref[0, 0, c] * RUN_ALIGN, RUN_ALIGN)
                    out_copy(par, c, dst).start()

        @pl.when(s == n_live - 1)
        def _():
            pltpu.make_async_copy(h2l_hbm.at[pl.ds(0, TM_FFN)], x_buf.at[1 - par],
                                  sem_x.at[1 - par]).wait()
            wait_out(par, nv)

            @pl.when(s >= 1)
            def _():
                wait_out(1 - par, nv_ref[jnp.maximum(s - 1, 0)])


def _ffn_call(tables, h2l, w_gu, b_gu, w_d, b_d):
    n_live, exp_id, n_valid, is_half, w_slot, w_next, src8, dst8 = tables
    n_tiles = src8.shape[0]

    def tbl_spec(index_fn):
        return pl.BlockSpec((1, 1, CHUNKS_PER_TILE), index_fn, memory_space=pltpu.SMEM)

    def per_expert(shape):
        return pl.BlockSpec((1,) + shape, lambda s, nl, ei, *_: (ei[s], 0, 0))

    grid_spec = pltpu.PrefetchScalarGridSpec(
        num_scalar_prefetch=6,
        grid=(n_tiles,),
        in_specs=[
            tbl_spec(lambda s, *_: (s, 0, 0)),
            tbl_spec(lambda s, *_: (jnp.minimum(s + 1, n_tiles - 1), 0, 0)),
            tbl_spec(lambda s, *_: (s, 0, 0)),
            pl.BlockSpec(memory_space=pl.ANY),
            pl.BlockSpec(memory_space=pl.ANY), per_expert((1, 2 * D_EXPERT)),
            pl.BlockSpec(memory_space=pl.ANY), per_expert((1, D_MODEL)),
        ],
        out_specs=pl.BlockSpec(memory_space=pl.ANY),
        scratch_shapes=[pltpu.VMEM((2, D_MODEL, 2 * D_EXPERT), F32),
                        pltpu.VMEM((2, D_EXPERT, D_MODEL), F32),
                        pltpu.VMEM((D_MODEL, 2 * D_EXPERT), BF16),
                        pltpu.VMEM((D_EXPERT, D_MODEL), BF16),
                        pltpu.VMEM((2, TM_FFN, D_MODEL), F32),
                        pltpu.VMEM((2, TM_FFN, D_MODEL), F32),
                        pltpu.SemaphoreType.DMA((2,)),
                        pltpu.SemaphoreType.DMA((2,)),
                        pltpu.SemaphoreType.DMA((2,))],
    )
    tables_and_chunks = (n_live, exp_id, n_valid, is_half, w_slot, w_next, src8, src8, dst8)
    return pl.pallas_call(
        _ffn_kernel,
        out_shape=jax.ShapeDtypeStruct(h2l.shape, F32),
        grid_spec=grid_spec,
        input_output_aliases={len(tables_and_chunks): 0},
        compiler_params=pltpu.CompilerParams(dimension_semantics=("arbitrary",),
                                             vmem_limit_bytes=VMEM_LIMIT,
                                             has_side_effects=True),
        name="ffn",
    )(*tables_and_chunks, h2l, w_gu, b_gu, w_d, b_d)


def _combine_kernel(x1_ref, ti_ref, tp_ref, mod_ref, g2_ref, b2_ref, yl_ref,
                    oc_ref, ol_ref, *, n_ctx, lat_len):
    i = pl.program_id(0)
    tb = TM_POST

    jl = lax.broadcasted_iota(jnp.int32, (tb, LOCAL_ROWS), 1)
    ti = ti_ref[...]
    tp = tp_ref[...]
    sel = jnp.zeros((tb, LOCAL_ROWS), F32)
    for k in range(TOP_K):
        sel = jnp.where(jl == ti[:, TOP_K + k:TOP_K + k + 1], tp[:, k:k + 1], sel)
    moe = _dot(sel.astype(BF16), yl_ref[...].astype(BF16))

    r = _mod_row(i, tb, n_ctx, lat_len)
    gate2 = mod_ref[pl.ds(r, 1), 5 * D_MODEL:6 * D_MODEL]
    res = DEEPNORM_ALPHA * x1_ref[...] + gate2 * moe
    mu = jnp.mean(res, axis=-1, keepdims=True)
    cen = res - mu
    var = jnp.mean(cen * cen, axis=-1, keepdims=True)
    out = cen * lax.rsqrt(var + LN_EPS) * g2_ref[...] + b2_ref[...]

    @pl.when(i * tb < n_ctx)
    def _():
        oc_ref[...] = out

    @pl.when(i * tb >= n_ctx)
    def _():
        ol_ref[...] = out


def _combine_call(x1, route, tp, mod, ln_g, ln_b, y_local, n_ctx, lat_len):
    t = x1.shape[0]
    tb = TM_POST
    kern = functools.partial(_combine_kernel, n_ctx=n_ctx, lat_len=lat_len)
    return pl.pallas_call(
        kern,
        out_shape=(jax.ShapeDtypeStruct((n_ctx, D_MODEL), F32),
                   jax.ShapeDtypeStruct((t - n_ctx, D_MODEL), F32)),
        grid=(t // tb,),
        in_specs=[
            pl.BlockSpec((tb, D_MODEL), lambda i: (i, 0)),
            pl.BlockSpec((tb, LANES), lambda i: (i, 0)),
            pl.BlockSpec((tb, LANES), lambda i: (i, 0)),
            pl.BlockSpec(mod.shape, lambda i: (0, 0)),
            pl.BlockSpec((1, D_MODEL), lambda i: (0, 0)),
            pl.BlockSpec((1, D_MODEL), lambda i: (0, 0)),
            pl.BlockSpec((LOCAL_ROWS, D_MODEL), lambda i: (i, 0)),
        ],
        out_specs=_ctx_lat_specs((tb, D_MODEL), n_ctx // tb),
        compiler_params=_cparams(("arbitrary",)),
        name="combine",
    )(x1, route, tp, mod, ln_g, ln_b, y_local)


def _routing_tables(cnt):
    i32 = jnp.int32
    half_rows = TM_FFN // 2
    n_blocks = cnt.shape[0]
    max_rows = n_blocks * (TM_POST * TOP_K + N_EXPERTS * (RUN_ALIGN - 1)) + N_EXPERTS * (half_rows - RUN_ALIGN)
    n_tiles = -(-max_rows // TM_FFN) + N_EXPERTS // 2 + 1

    cpad = (cnt + (RUN_ALIGN - 1)) // RUN_ALIGN * RUN_ALIGN
    run_loc = jnp.cumsum(cpad, axis=1) - cpad
    used = cpad.sum(axis=0)
    region = (used + (half_rows - 1)) // half_rows * half_rows
    incl = jnp.cumsum(region)
    offs = incl - region
    tiles_e = (region + (TM_FFN - 1)) // TM_FFN
    t_end = jnp.cumsum(tiles_e)
    n_live = t_end[-1]
    run_glob = offs[None, :] + jnp.cumsum(cpad, axis=0) - cpad

    experts = jnp.arange(N_EXPERTS, dtype=i32)

    def per_expert_row(e_idx, table):
        onehot = (e_idx[:, None] == experts).astype(F32)
        return jnp.dot(onehot, table.astype(F32), precision=lax.Precision.HIGHEST).astype(i32)

    tile = jnp.arange(n_tiles, dtype=i32)
    tcl = jnp.minimum(tile, n_live - 1)
    e_of = jnp.minimum(jnp.sum(t_end[None, :] <= tcl[:, None], axis=1), N_EXPERTS - 1).astype(i32)
    mine = e_of[:, None] == experts

    def pick(v):
        return jnp.sum(jnp.where(mine, v[None, :], 0), axis=1)

    k_in_e = tcl - pick(t_end - tiles_e)
    row0 = pick(offs) + k_in_e * TM_FFN
    tile_rows = jnp.minimum(TM_FFN, pick(region) - k_in_e * TM_FFN)
    is_half = (tile_rows == half_rows).astype(i32)
    n_run = jnp.clip((pick(offs + used) - row0) // RUN_ALIGN, 0, tile_rows // RUN_ALIGN)
    n_out = (n_run + (SCATTER_GROUP - 1)) // SCATTER_GROUP * SCATTER_GROUP
    n_out = jnp.where(tile < n_live, n_out, 0).astype(i32)

    c_in_tile = jnp.tile(jnp.arange(CHUNKS_PER_TILE, dtype=i32), n_tiles)
    first_row = jnp.where(tile < n_live, row0, incl[-1])
    g_row = jnp.repeat(first_row, CHUNKS_PER_TILE) + c_in_tile * RUN_ALIGN
    g_exp = jnp.minimum(jnp.sum(incl[None, :] <= g_row[:, None], axis=1), N_EXPERTS - 1).astype(i32)
    rs = per_expert_row(g_exp, run_glob.T)
    re = rs + per_expert_row(g_exp, cpad.T)
    ls = per_expert_row(g_exp, (jnp.arange(n_blocks, dtype=i32)[:, None] * LOCAL_ROWS + run_loc).T)
    inside = (g_row[:, None] >= rs) & (g_row[:, None] < re)
    valid = jnp.any(inside, axis=1)
    local_row = g_row + jnp.sum(jnp.where(inside, ls - rs, 0), axis=1)
    src8 = jnp.where(valid, local_row // RUN_ALIGN, LOCAL_CHUNKS - 1).astype(i32)

    filler = jnp.logical_not(valid) & (c_in_tile < jnp.repeat(n_out, CHUNKS_PER_TILE))
    u = jnp.cumsum(filler.astype(i32)) - 1
    spare = (u // 3) * LOCAL_CHUNKS + (LOCAL_CHUNKS - 4) + u % 3
    dst8 = jnp.where(valid, local_row // RUN_ALIGN, jnp.where(filler, spare, 0)).astype(i32)

    has_rows = used > 0
    ordinal = jnp.cumsum(has_rows.astype(i32)) - 1
    w_slot = (jnp.sum(jnp.where(e_of[:, None] == experts, ordinal[None, :], 0), axis=1) % 2).astype(i32)
    later = (experts[None, :] > e_of[:, None]) & has_rows[None, :]
    w_next = jnp.min(jnp.where(later, experts[None, :], N_EXPERTS), axis=1)
    w_next = jnp.where(w_next < N_EXPERTS, w_next, -1).astype(i32)
    shape = (n_tiles, 1, CHUNKS_PER_TILE)
    return (n_live.reshape(1).astype(i32), e_of, n_out, is_half, w_slot, w_next,
            src8.reshape(shape), dst8.reshape(shape))


def kernel(x_prompt, x_sample, c, state_ssd_fwd, state_ssd_bwd, c_ctx, w_ada, b_ada, w_in,
           ssd_conv_w, ssd_conv_b, ssd_a_log, ssd_dt_bias, ssd_d, ssd_norm_w, ssd_w_out,
           sc_conv_w, sc_w_out, w_o, ln1_g, ln1_b, w_router, b_router, w_gate_up, b_gate_up,
           w_down, b_down, ln2_g, ln2_b):
    n_ctx_req, ctx_len, _ = x_prompt.shape
    n_lat_req, lat_len, _ = x_sample.shape
    n_ctx = n_ctx_req * ctx_len
    n_lat = n_lat_req * lat_len
    t = n_ctx + n_lat
    assert w_ada.shape[0] == 1, "single trunk layer"
    assert ctx_len % SSD_CHUNK == 0 and lat_len % TS_SSD == 0 and SSD_CHUNK % GRID_W == 0
    assert n_ctx % TM_PROJ == 0 and lat_len % TM_PROJ == 0 and TM_POST % ctx_len == 0
    assert CONV_SUB % ctx_len == 0 and CONV_SUB % GRID_W == 0 and n_ctx % TB_CONV == 0
    assert ctx_len & (ctx_len - 1) == 0 and GRID_W & (GRID_W - 1) == 0

    x_ctx = x_prompt.reshape(n_ctx, D_MODEL)
    x_lat = x_sample.reshape(n_lat, D_MODEL)

    cvec = jnp.concatenate([c_ctx[None, :], c, jnp.zeros((7 - n_lat_req, D_MODEL), F32)], 0)
    mod = _mod_call(cvec, w_ada[0], b_ada[0])

    w = w_in[0]
    o_dt = D_INNER + XBC_DIM
    w_main = jnp.concatenate([w[:, :o_dt], w[:, o_dt + SSD_HEADS:]], axis=1).astype(BF16)
    w_dt = jnp.pad(w[:, o_dt:o_dt + SSD_HEADS], ((0, 0), (0, LANES - SSD_HEADS))).astype(BF16)
    proj, dt_raw = _inproj_call(x_ctx, x_lat, mod, w_main, w_dt, lat_len)

    xbc = _conv_call(proj, ssd_conv_w[0], ssd_conv_b[0], n_ctx, ctx_len)

    pad_h = ((0, 0), (0, LANES - SSD_HEADS))
    a_log = jnp.pad(ssd_a_log[0], pad_h).reshape(2, 1, LANES)
    dt_bias = jnp.pad(ssd_dt_bias[0], pad_h).reshape(2, 1, LANES)
    dvec = jnp.repeat(ssd_d[0], SSD_HEADDIM).reshape(1, D_INNER)
    h0_lat = jnp.stack([state_ssd_fwd[:, 0].reshape(n_lat_req, D_INNER, SSD_STATE),
                        state_ssd_bwd[:, 0].reshape(n_lat_req, D_INNER, SSD_STATE)], axis=1)
    y_ctx, h_ctx = _ssd_call(xbc, dt_raw, None, a_log, dt_bias, dvec, 0, n_ctx_req, ctx_len, True)
    (y_lat,) = _ssd_call(xbc, dt_raw, h0_lat, a_log, dt_bias, dvec, n_ctx, n_lat_req, lat_len, False)

    w_r = jnp.pad(w_router[0], ((0, 0), (0, LANES - N_EXPERTS)))
    b_r = jnp.pad(b_router[0], (0, LANES - N_EXPERTS)).reshape(1, LANES)
    x1, h2_local, route, top_p, cnt = _post_call(
        x_ctx, x_lat, mod, proj, y_ctx, y_lat, ssd_norm_w[0].reshape(1, D_INNER),
        ssd_w_out[0].astype(BF16), sc_conv_w[0], sc_w_out[0].astype(BF16), w_o[0].astype(BF16),
        ln1_g[0].reshape(1, D_MODEL), ln1_b[0].reshape(1, D_MODEL), w_r, b_r, ctx_len, lat_len)

    tables = _routing_tables(cnt[:, 0, :N_EXPERTS])
    y_local = _ffn_call(tables, h2_local,
                        w_gate_up[0], b_gate_up[0].reshape(N_EXPERTS, 1, 2 * D_EXPERT),
                        w_down[0], b_down[0].reshape(N_EXPERTS, 1, D_MODEL))
    out_ctx, out_lat = _combine_call(x1, route, top_p, mod, ln2_g[0].reshape(1, D_MODEL),
                                     ln2_b[0].reshape(1, D_MODEL), y_local, n_ctx, lat_len)

    state_shape = (n_ctx_req, 1, SSD_HEADS, SSD_HEADDIM, SSD_STATE)
    return (out_ctx.reshape(n_ctx_req, ctx_len, D_MODEL), out_lat.reshape(n_lat_req, lat_len, D_MODEL),
            h_ctx[0].reshape(state_shape), h_ctx[1].reshape(state_shape))
```

```python
import functools

import jax
import jax.numpy as jnp
from jax import lax
from jax.experimental import pallas as pl
from jax.experimental.pallas import tpu as pltpu

F32 = jnp.float32
BF16 = jnp.bfloat16

D_MODEL = 1024
GRID_W = 64
D_INNER = 2048
SSD_HEADDIM = 64
SSD_HEADS = 32
SSD_GROUPS = 4
SSD_HPG = 8
SSD_STATE = 128
SSD_CONV = 5
SSD_CHUNK = 128
XBC_DIM = D_INNER + 2 * SSD_GROUPS * SSD_STATE
SC_CONV = 3
N_EXPERTS = 32
TOP_K = 4
D_EXPERT = 1024
SWIGLU_LIMIT = 7.0
SWIGLU_ALPHA = 1.702
LN_EPS = 1e-5
RMS_EPS = 1e-5
DEEPNORM_ALPHA = 2.0 ** 0.25
LOG2E = 1.4426950408889634

LANES = 128
MAIN_COLS = 10240
COL_Z = 0
COL_XBC = 2048
COL_SCB = 5120
COL_SCC = 6144
COL_SCV = 7168
COL_GSSD = 8192
COL_GSC = 9216

VMEM_LIMIT = 56 * 1024 * 1024

TM_PROJ = 1024
TN_PROJ = 2048
TB_CONV = 2048
CONV_SUB = 256
TS_SSD = 1024
TM_POST = 256
TM_FFN = 512
RUN_ALIGN = 8
CHUNKS_PER_TILE = TM_FFN // RUN_ALIGN
SCATTER_GROUP = 8
LOCAL_ROWS = TM_POST * TOP_K + N_EXPERTS * RUN_ALIGN
LOCAL_CHUNKS = LOCAL_ROWS // RUN_ALIGN


def _cparams(sem):
    return pltpu.CompilerParams(dimension_semantics=sem, vmem_limit_bytes=VMEM_LIMIT)


def _split3(v):
    hi = v.astype(BF16)
    r1 = v - hi.astype(F32)
    mid = r1.astype(BF16)
    lo = (r1 - mid.astype(F32)).astype(BF16)
    return hi, mid, lo


def _dot(a, b):
    return jnp.dot(a, b, preferred_element_type=F32)


def _dot_exact_lhs(m_bf16, v_f32):
    hi, mid, lo = _split3(v_f32)
    return _dot(m_bf16, hi) + _dot(m_bf16, mid) + _dot(m_bf16, lo)


def _dot_x3(a_f32, b_f32):
    a_hi = a_f32.astype(BF16)
    a_lo = (a_f32 - a_hi.astype(F32)).astype(BF16)
    b_hi = b_f32.astype(BF16)
    b_lo = (b_f32 - b_hi.astype(F32)).astype(BF16)
    return _dot(a_hi, b_hi) + _dot(a_lo, b_hi) + _dot(a_hi, b_lo)


def _silu(v):
    return v * jax.nn.sigmoid(v)


def _softplus(v):
    return jnp.maximum(v, 0.0) + jnp.log(1.0 + jnp.exp(-jnp.abs(v)))


def _mod_kernel(c_ref, w_ref, b_ref, o_ref):
    o_ref[...] = _dot_x3(_silu(c_ref[...]), w_ref[...]) + b_ref[...]


def _mod_call(cvec, w_ada, b_ada):
    n = w_ada.shape[1]
    tn = 1536
    return pl.pallas_call(
        _mod_kernel,
        out_shape=jax.ShapeDtypeStruct((cvec.shape[0], n), F32),
        grid=(n // tn,),
        in_specs=[
            pl.BlockSpec(cvec.shape, lambda j: (0, 0)),
            pl.BlockSpec((D_MODEL, tn), lambda j: (0, j)),
            pl.BlockSpec((1, tn), lambda j: (0, j)),
        ],
        out_specs=pl.BlockSpec((cvec.shape[0], tn), lambda j: (0, j)),
        compiler_params=_cparams(("arbitrary",)),
        name="mod",
    )(cvec, w_ada, b_ada.reshape(1, n))


def _mod_row(block, rows_per_block, n_ctx_tokens, lat_len):
    tok = block * rows_per_block
    return jnp.where(tok < n_ctx_tokens, 0, 1 + (tok - n_ctx_tokens) // lat_len)


def _ctx_lat_specs(block_shape, n_ctx_blocks, lead=()):
    tail = (0,) * (len(block_shape) - len(lead) - 1)
    ctx = pl.BlockSpec(block_shape, lambda i, *_: lead + (jnp.minimum(i, n_ctx_blocks - 1),) + tail)
    lat = pl.BlockSpec(block_shape, lambda i, *_: lead + (jnp.maximum(i - n_ctx_blocks, 0),) + tail)
    return ctx, lat


def _shift_rows(x, off, pos, row_len):
    if off == 0:
        return x
    n = x.shape[0]
    rolled = pltpu.roll(x, (-off) % n, 0)
    ok = (pos + off >= 0) & (pos + off < row_len)
    return jnp.where(ok, rolled, 0.0)


def _row_pos(n, row_len):
    t = lax.broadcasted_iota(jnp.int32, (n, 1), 0)
    return jnp.bitwise_and(t, row_len - 1)


def _inproj_kernel(xc_ref, xl_ref, mod_ref, w_ref, wdt_ref, o_ref, dt_ref, h_scr, *, n_ctx, lat_len):
    i = pl.program_id(0)
    j = pl.program_id(1)

    @pl.when(j == 0)
    def _():
        r = _mod_row(i, TM_PROJ, n_ctx, lat_len)
        shift = mod_ref[pl.ds(r, 1), 0:D_MODEL]
        scale = mod_ref[pl.ds(r, 1), D_MODEL:2 * D_MODEL]
        x = jnp.where(i * TM_PROJ < n_ctx, xc_ref[...], xl_ref[...])
        h = (x * (1.0 + scale) + shift).astype(BF16)
        h_scr[...] = h
        dt_ref[...] = _dot(h, wdt_ref[...])

    o_ref[...] = _dot(h_scr[...], w_ref[...]).astype(BF16)


def _inproj_call(x_ctx, x_lat, mod, w_main, w_dt, lat_len):
    n_ctx = x_ctx.shape[0]
    t = n_ctx + x_lat.shape[0]
    kern = functools.partial(_inproj_kernel, n_ctx=n_ctx, lat_len=lat_len)
    return pl.pallas_call(
        kern,
        out_shape=(jax.ShapeDtypeStruct((t, MAIN_COLS), BF16),
                   jax.ShapeDtypeStruct((t, LANES), F32)),
        grid=(t // TM_PROJ, MAIN_COLS // TN_PROJ),
        in_specs=[
            *_ctx_lat_specs((TM_PROJ, D_MODEL), n_ctx // TM_PROJ),
            pl.BlockSpec(mod.shape, lambda i, j: (0, 0)),
            pl.BlockSpec((D_MODEL, TN_PROJ), lambda i, j: (0, j)),
            pl.BlockSpec((D_MODEL, LANES), lambda i, j: (0, 0)),
        ],
        out_specs=(pl.BlockSpec((TM_PROJ, TN_PROJ), lambda i, j: (i, j)),
                   pl.BlockSpec((TM_PROJ, LANES), lambda i, j: (i, 0))),
        scratch_shapes=[pltpu.VMEM((TM_PROJ, D_MODEL), BF16)],
        compiler_params=_cparams(("arbitrary", "arbitrary")),
        name="inproj",
    )(x_ctx, x_lat, mod, w_main, w_dt)


def _conv_kernel(x_ref, s_ref, w_ref, b_ref, o_ref):
    half = SSD_CONV // 2
    for r in range(0, TB_CONV, CONV_SUB):
        x = x_ref[r:r + CONV_SUB, :]
        acc = x.astype(F32) * w_ref[half:half + 1, :]
        for k in range(SSD_CONV):
            if k != half:
                tap = k if k < half else k - 1
                acc = acc + _dot(s_ref[0, tap], x) * w_ref[k:k + 1, :]
        o_ref[r:r + CONV_SUB, :] = _silu(acc + b_ref[...]).astype(BF16)


def _shift_matrices(n, row_lens, width):
    t = jnp.arange(n, dtype=jnp.int32)
    half = width // 2
    mats = []
    for row_len in row_lens:
        taps = []
        for off in [o for o in range(-half, half + 1) if o != 0]:
            pos = t % row_len + off
            hit = (t[None, :] == t[:, None] + off) & ((pos >= 0) & (pos < row_len))[:, None]
            taps.append(hit)
        mats.append(jnp.stack(taps))
    return jnp.stack(mats).astype(BF16)


def _conv_call(proj, conv_w, conv_b, n_ctx, ctx_len):
    t = proj.shape[0]
    tc = 1024
    n_ctx_blocks = n_ctx // TB_CONV
    shifts = _shift_matrices(CONV_SUB, (ctx_len, GRID_W), SSD_CONV)
    return pl.pallas_call(
        _conv_kernel,
        out_shape=jax.ShapeDtypeStruct((t, XBC_DIM), BF16),
        grid=(t // TB_CONV, XBC_DIM // tc),
        in_specs=[
            pl.BlockSpec((TB_CONV, tc), lambda i, j: (i, COL_XBC // tc + j)),
            pl.BlockSpec((1, SSD_CONV - 1, CONV_SUB, CONV_SUB),
                         lambda i, j: (jnp.where(i < n_ctx_blocks, 0, 1), 0, 0, 0)),
            pl.BlockSpec((SSD_CONV, tc), lambda i, j: (0, j)),
            pl.BlockSpec((1, tc), lambda i, j: (0, j)),
        ],
        out_specs=pl.BlockSpec((TB_CONV, tc), lambda i, j: (i, j)),
        compiler_params=_cparams(("arbitrary", "arbitrary")),
        name="conv",
    )(proj, shifts, conv_w, conv_b.reshape(1, XBC_DIM))


def _ssd_kernel(*refs, has_h0, want_final):
    xs_ref, b_ref, c_ref, dt_ref = refs[:4]
    rest = list(refs[4:])
    h0_ref = rest.pop(0) if has_h0 else None
    alog_ref, bias_ref, dvec_ref, y_ref = rest[:4]
    hfin_ref = rest[4] if want_final else None
    ht_scr = rest[-1]
    d = pl.program_id(1)
    st = pl.program_id(2)
    q = SSD_CHUNK
    n_chunks = xs_ref.shape[0] // q
    n_blk = D_INNER // LANES
    is_fwd = d == 0

    @pl.when(st == 0)
    def _():
        if has_h0:
            for j in range(n_blk):
                ht_scr[:, j * LANES:(j + 1) * LANES] = h0_ref[0, 0, j * LANES:(j + 1) * LANES, :].T
        else:
            ht_scr[...] = jnp.zeros_like(ht_scr)

    a2_neg = -jnp.exp(alog_ref[0]) * LOG2E
    bias = bias_ref[0]
    row = lax.broadcasted_iota(jnp.int32, (q, q), 0)
    col = lax.broadcasted_iota(jnp.int32, (q, q), 1)
    keep = (row - col) * jnp.where(is_fwd, 1, -1) >= 0
    tri = keep.astype(BF16)
    lane_lo = lax.broadcasted_iota(jnp.int32, (q, LANES), 1) < SSD_HEADDIM
    lane_lo1 = lane_lo[0:1, :]
    d_on = jnp.where(is_fwd, 1.0, 0.0)
    neg_inf = jnp.float32(-jnp.inf)

    def chunk_body(ci, carry):
        c = jnp.where(is_fwd, ci, n_chunks - 1 - ci)
        c0 = pl.multiple_of(c * q, q)
        dt = _softplus(dt_ref[pl.ds(c0, q), :] + bias)
        acs = _dot_exact_lhs(tri, dt * a2_neg)
        acs_t = acs.T
        dt_t = dt.T
        tot_row = jnp.where(is_fwd, acs[q - 1:q, :], acs[0:1, :])
        tot_col = jnp.where(is_fwd, acs_t[:, q - 1:q], acs_t[:, 0:1])
        e_tot = jnp.exp2(tot_row)
        src_t = acs_t - jnp.log2(dt_t)
        dte_t = jnp.exp2(tot_col - acs_t) * dt_t

        for g in range(SSD_GROUPS):
            bg = b_ref[pl.ds(c0, q), g * SSD_STATE:(g + 1) * SSD_STATE]
            cg = c_ref[pl.ds(c0, q), g * SSD_STATE:(g + 1) * SSD_STATE]
            cb = lax.dot_general(cg, bg, (((1,), (1,)), ((), ())),
                                 preferred_element_type=F32)
            cgf = cg.astype(F32)
            bg_t = bg.astype(F32).T
            for p in range(SSD_HPG // 2):
                h0 = g * SSD_HPG + 2 * p
                lo = h0 * SSD_HEADDIM
                xp = xs_ref[pl.ds(c0, q), lo:lo + LANES]
                htp = ht_scr[:, lo:lo + LANES]
                rhs = jnp.concatenate([xp, htp.astype(BF16)], axis=0)
                ys, sts = [], []
                for hh in (h0, h0 + 1):
                    tgt = jnp.broadcast_to(acs[:, hh:hh + 1], (q, q))
                    seg = jnp.where(keep, tgt - src_t[hh:hh + 1, :], neg_inf)
                    w = (cb * jnp.exp2(seg)).astype(BF16)
                    ce = (cgf * jnp.exp2(tgt)).astype(BF16)
                    ys.append(_dot(jnp.concatenate([w, ce], axis=1), rhs))
                    sts.append(_dot((bg_t * dte_t[hh:hh + 1, :]).astype(BF16), xp))
                y = (jnp.where(lane_lo, ys[0], ys[1])
                     + (d_on * dvec_ref[:, lo:lo + LANES]) * xp.astype(F32))
                y_ref[0, pl.ds(c0, q), lo:lo + LANES] = y.astype(BF16)
                dec = jnp.where(lane_lo1, e_tot[:, h0:h0 + 1], e_tot[:, h0 + 1:h0 + 2])
                ht_scr[:, lo:lo + LANES] = htp * dec + jnp.where(lane_lo, sts[0], sts[1])
        return carry

    lax.fori_loop(0, n_chunks, chunk_body, 0)

    if want_final:
        @pl.when(st == pl.num_programs(2) - 1)
        def _():
            for j in range(n_blk):
                hfin_ref[0, 0, j * LANES:(j + 1) * LANES, :] = ht_scr[:, j * LANES:(j + 1) * LANES].T


def _ssd_call(xbc, dt_raw, h0, a_log, dt_bias, dvec, tok0, n_seq, seq_len, want_final):
    ts = min(seq_len, TS_SSD)
    assert seq_len % ts == 0 and tok0 % ts == 0 and ts % SSD_CHUNK == 0
    n_steps = seq_len // ts
    blk0 = tok0 // ts
    bc_w = SSD_GROUPS * SSD_STATE

    def tok_blk(b, d, s):
        return blk0 + b * n_steps + jnp.where(d == 0, s, n_steps - 1 - s)

    state_spec = pl.BlockSpec((1, 1, D_INNER, SSD_STATE), lambda b, d, s: (b, d, 0, 0))
    in_specs = [
        pl.BlockSpec((ts, D_INNER), lambda b, d, s: (tok_blk(b, d, s), 0)),
        pl.BlockSpec((ts, bc_w), lambda b, d, s: (tok_blk(b, d, s), D_INNER // bc_w)),
        pl.BlockSpec((ts, bc_w), lambda b, d, s: (tok_blk(b, d, s), D_INNER // bc_w + 1)),
        pl.BlockSpec((ts, LANES), lambda b, d, s: (tok_blk(b, d, s), 0)),
    ]
    args = [xbc, xbc, xbc, dt_raw]
    if h0 is not None:
        in_specs.append(state_spec)
        args.append(h0)
    in_specs += [
        pl.BlockSpec((1, 1, LANES), lambda b, d, s: (d, 0, 0)),
        pl.BlockSpec((1, 1, LANES), lambda b, d, s: (d, 0, 0)),
        pl.BlockSpec((1, D_INNER), lambda b, d, s: (0, 0)),
    ]
    args += [a_log, dt_bias, dvec]
    out_shape = [jax.ShapeDtypeStruct((2, n_seq * seq_len, D_INNER), BF16)]
    out_specs = [pl.BlockSpec((1, ts, D_INNER),
                              lambda b, d, s: (d, tok_blk(b, d, s) - blk0, 0))]
    if want_final:
        out_shape.append(jax.ShapeDtypeStruct((2, n_seq, D_INNER, SSD_STATE), F32))
        out_specs.append(pl.BlockSpec((1, 1, D_INNER, SSD_STATE), lambda b, d, s: (d, b, 0, 0)))
    kern = functools.partial(_ssd_kernel, has_h0=h0 is not None, want_final=want_final)
    return pl.pallas_call(
        kern,
        out_shape=tuple(out_shape),
        grid=(n_seq, 2, n_steps),
        in_specs=in_specs,
        out_specs=tuple(out_specs),
        scratch_shapes=[pltpu.VMEM((SSD_STATE, D_INNER), F32)],
        compiler_params=_cparams(("arbitrary", "arbitrary", "arbitrary")),
        name="ssd_final" if want_final else "ssd",
    )(*args)


def _post_kernel(xc_ref, xl_ref, mod_ref, z_ref, scb_ref, scc_ref, scv_ref, gssd_ref, gsc_ref,
                 yc_ref, yl_ref, normw_ref, wssd_ref, scw_ref, wsc_ref, wo_ref, g1_ref, b1_ref,
                 wr_ref, br_ref, x1_ref, h2l_ref, ti_ref, tp_ref, cnt_ref,
                 *, n_ctx, ctx_len, lat_len):
    i = pl.program_id(0)
    tm = TM_POST
    r = _mod_row(i, tm, n_ctx, lat_len)
    is_ctx = i * tm < n_ctx

    def mod_vec(k):
        return mod_ref[pl.ds(r, 1), k * D_MODEL:(k + 1) * D_MODEL]

    gate1, shift2, scale2 = mod_vec(2), mod_vec(3), mod_vec(4)

    y_f = jnp.where(is_ctx, yc_ref[0], yl_ref[0]).astype(F32)
    y_b = jnp.where(is_ctx, yc_ref[1], yl_ref[1]).astype(F32)
    y = (y_f + y_b) * _silu(z_ref[...].astype(F32))
    gw = D_INNER // SSD_GROUPS
    parts = []
    for g in range(SSD_GROUPS):
        yg = y[:, g * gw:(g + 1) * gw]
        ms = jnp.mean(yg * yg, axis=-1, keepdims=True)
        parts.append(yg * lax.rsqrt(ms + RMS_EPS))
    yn = (jnp.concatenate(parts, axis=-1) * normw_ref[...]).astype(BF16)
    y_ssd = _dot(yn, wssd_ref[...])

    row_len = jnp.where(i * tm < n_ctx, ctx_len, GRID_W)
    pos = _row_pos(tm, row_len)
    u = scc_ref[...].astype(F32) * scv_ref[...].astype(F32)
    half = SC_CONV // 2
    cv = u * scw_ref[half:half + 1, :]
    for k in range(SC_CONV):
        if k != half:
            cv = cv + _shift_rows(u, k - half, pos, row_len) * scw_ref[k:k + 1, :]
    y_sc = _dot((scb_ref[...].astype(F32) * cv).astype(BF16), wsc_ref[...])

    mixed = (jax.nn.sigmoid(gssd_ref[...].astype(F32)) * y_ssd
             + jax.nn.sigmoid(gsc_ref[...].astype(F32)) * y_sc)
    o = _dot(mixed.astype(BF16), wo_ref[...])

    x = jnp.where(is_ctx, xc_ref[...], xl_ref[...])
    res = DEEPNORM_ALPHA * x + gate1 * o
    mu = jnp.mean(res, axis=-1, keepdims=True)
    cen = res - mu
    var = jnp.mean(cen * cen, axis=-1, keepdims=True)
    x1 = cen * lax.rsqrt(var + LN_EPS) * g1_ref[...] + b1_ref[...]
    x1_ref[...] = x1
    h2 = x1 * (1.0 + scale2) + shift2

    logits = _dot_x3(h2, wr_ref[...]) + br_ref[...]
    lane = lax.broadcasted_iota(jnp.int32, (tm, LANES), 1)
    lane_f = lane.astype(F32)
    neg = jnp.float32(-jnp.inf)
    work = jnp.where(lane < N_EXPERTS, logits, neg)
    vals, idxs, hits = [], [], []
    for _ in range(TOP_K):
        m = jnp.max(work, axis=-1, keepdims=True)
        idx = jnp.min(jnp.where(work == m, lane_f, float(LANES)), axis=-1, keepdims=True)
        hit = lane_f == idx
        vals.append(m)
        idxs.append(idx)
        hits.append(hit)
        work = jnp.where(hit, neg, work)
    es = [jnp.exp(v - vals[0]) for v in vals]
    denom = es[0] + es[1] + es[2] + es[3]

    chosen = jnp.where(hits[0] | hits[1] | hits[2] | hits[3], 1.0, 0.0)
    cnt = jnp.sum(chosen, axis=0, keepdims=True)
    cnt_ref[0] = cnt.astype(jnp.int32)
    cpad = jnp.floor((cnt + (RUN_ALIGN - 1)) * (1.0 / RUN_ALIGN)) * RUN_ALIGN
    er = lax.broadcasted_iota(jnp.int32, (LANES, LANES), 0)
    ec = lax.broadcasted_iota(jnp.int32, (LANES, LANES), 1)
    run_off = _dot(jnp.broadcast_to(cpad, (8, LANES)).astype(BF16), (er < ec).astype(BF16))[0:1]
    rr = lax.broadcasted_iota(jnp.int32, (tm, tm), 0)
    cc = lax.broadcasted_iota(jnp.int32, (tm, tm), 1)
    slot = _dot((rr > cc).astype(BF16), chosen.astype(BF16)) + run_off
    lrows = [jnp.sum(jnp.where(hits[k], slot, 0.0), axis=-1, keepdims=True) for k in range(TOP_K)]

    ti = jnp.zeros((tm, LANES), F32)
    tp = jnp.zeros((tm, LANES), F32)
    lmat = jnp.full((tm, LANES), -1.0, F32)
    for k in range(TOP_K):
        ti = jnp.where(lane == k, idxs[k], ti)
        ti = jnp.where(lane == TOP_K + k, lrows[k], ti)
        lmat = jnp.where(lane == k, lrows[k], lmat)
        tp = jnp.where(lane == k, es[k] / denom, tp)
    ti_ref[...] = ti.astype(jnp.int32)
    tp_ref[...] = tp

    lrow_t = jnp.concatenate([lmat[j * LANES:(j + 1) * LANES, :].T for j in range(tm // LANES)],
                             axis=1)
    jrow = lax.broadcasted_iota(jnp.int32, (LOCAL_ROWS, tm), 0).astype(F32)
    sel_t = jnp.zeros((LOCAL_ROWS, tm), F32)
    for k in range(TOP_K):
        sel_t = jnp.where(jrow == lrow_t[k:k + 1, :], 1.0, sel_t)
    h2l_ref[...] = _dot(sel_t.astype(BF16), h2.astype(BF16))


def _post_call(x_ctx, x_lat, mod, proj, y_ctx, y_lat, norm_w, w_ssd, sc_w, w_sc, w_o, ln_g, ln_b,
               w_r, b_r, ctx_len, lat_len):
    n_ctx = x_ctx.shape[0]
    t = n_ctx + x_lat.shape[0]
    tm = TM_POST
    kern = functools.partial(_post_kernel, n_ctx=n_ctx, ctx_len=ctx_len, lat_len=lat_len)

    def colblk(off, width):
        return pl.BlockSpec((tm, width), lambda i: (i, off // width))

    def whole(a):
        return pl.BlockSpec(a.shape, lambda i: (0,) * a.ndim, pipeline_mode=pl.Buffered(1))

    return pl.pallas_call(
        kern,
        out_shape=(jax.ShapeDtypeStruct((t, D_MODEL), F32),
                   jax.ShapeDtypeStruct((t // tm * LOCAL_ROWS, D_MODEL), F32),
                   jax.ShapeDtypeStruct((t, LANES), jnp.int32),
                   jax.ShapeDtypeStruct((t, LANES), F32),
                   jax.ShapeDtypeStruct((t // tm, 1, LANES), jnp.int32)),
        grid=(t // tm,),
        in_specs=[
            *_ctx_lat_specs((tm, D_MODEL), n_ctx // tm),
            whole(mod),
            colblk(COL_Z, D_INNER),
            colblk(COL_SCB, D_MODEL), colblk(COL_SCC, D_MODEL), colblk(COL_SCV, D_MODEL),
            colblk(COL_GSSD, D_MODEL), colblk(COL_GSC, D_MODEL),
            *_ctx_lat_specs((2, tm, D_INNER), n_ctx // tm, lead=(0,)),
            whole(norm_w), whole(w_ssd), whole(sc_w), whole(w_sc), whole(w_o),
            whole(ln_g), whole(ln_b), whole(w_r), whole(b_r),
        ],
        out_specs=(pl.BlockSpec((tm, D_MODEL), lambda i: (i, 0)),
                   pl.BlockSpec((LOCAL_ROWS, D_MODEL), lambda i: (i, 0)),
                   pl.BlockSpec((tm, LANES), lambda i: (i, 0)),
                   pl.BlockSpec((tm, LANES), lambda i: (i, 0)),
                   pl.BlockSpec((1, 1, LANES), lambda i: (i, 0, 0))),
        compiler_params=_cparams(("arbitrary",)),
        name="post",
    )(x_ctx, x_lat, mod, proj, proj, proj, proj, proj, proj, y_ctx, y_lat,
      norm_w, w_ssd, sc_w, w_sc, w_o, ln_g, ln_b, w_r, b_r)


def _ffn_kernel(nlive_ref, exp_ref, nv_ref, half_ref, wslot_ref, wnext_ref,
                src_cur_ref, src_nxt_ref, dst_cur_ref,
                h2l_hbm, wgu_hbm, bgu_ref, wd_hbm, bd_ref,
                yl_hbm, wgu_f32, wd_f32, wgu_scr, wd_scr, x_buf, y_buf, sem_x, sem_y, sem_w):
    s = pl.program_id(0)
    par = s & 1
    n_live = nlive_ref[0]
    nv = nv_ref[s]
    new_expert = (s == 0) | (exp_ref[s] != exp_ref[jnp.maximum(s - 1, 0)])
    wslot = wslot_ref[s]

    def weight_copies(expert, slot):
        return (pltpu.make_async_copy(wgu_hbm.at[expert], wgu_f32.at[slot], sem_w.at[slot]),
                pltpu.make_async_copy(wd_hbm.at[expert], wd_f32.at[slot], sem_w.at[slot]))

    def gather(tbl_ref, slot):
        for c in range(CHUNKS_PER_TILE):
            src = pl.multiple_of(tbl_ref[0, 0, c] * RUN_ALIGN, RUN_ALIGN)
            pltpu.make_async_copy(h2l_hbm.at[pl.ds(src, RUN_ALIGN)],
                                  x_buf.at[slot, pl.ds(c * RUN_ALIGN, RUN_ALIGN)],
                                  sem_x.at[slot]).start()

    def out_copy(slot, c, dst):
        return pltpu.make_async_copy(y_buf.at[slot, pl.ds(c * RUN_ALIGN, RUN_ALIGN)],
                                     yl_hbm.at[pl.ds(dst, RUN_ALIGN)], sem_y.at[slot])

    def wait_out(slot, count):
        rows = pl.multiple_of(count * RUN_ALIGN, RUN_ALIGN)
        pltpu.make_async_copy(y_buf.at[slot, pl.ds(0, rows)], yl_hbm.at[pl.ds(0, rows)],
                              sem_y.at[slot]).wait()

    @pl.when(s == 0)
    def _():
        gather(src_cur_ref, 0)
        for cp in weight_copies(exp_ref[0], 0):
            cp.start()

    @pl.when(s < n_live)
    def _():
        pltpu.make_async_copy(h2l_hbm.at[pl.ds(0, TM_FFN)], x_buf.at[par], sem_x.at[par]).wait()

        @pl.when(new_expert)
        def _():
            for cp in weight_copies(0, wslot):
                cp.wait()
            wgu_scr[...] = wgu_f32[wslot].astype(BF16)
            wd_scr[...] = wd_f32[wslot].astype(BF16)

            @pl.when(wnext_ref[s] >= 0)
            def _():
                for cp in weight_copies(wnext_ref[s], 1 - wslot):
                    cp.start()

        def tile(rows):
            gather(src_nxt_ref, 1 - par)
            gu = _dot(x_buf[par, 0:rows, :].astype(BF16), wgu_scr[...]) + bgu_ref[0]
            g = jnp.minimum(gu[:, :D_EXPERT], SWIGLU_LIMIT)
            u = jnp.clip(gu[:, D_EXPERT:], -SWIGLU_LIMIT, SWIGLU_LIMIT)
            act = (u + 1.0) * g * jax.nn.sigmoid(SWIGLU_ALPHA * g)
            y = _dot(act.astype(BF16), wd_scr[...]) + bd_ref[0]

            @pl.when(s >= 2)
            def _():
                wait_out(par, nv_ref[jnp.maximum(s - 2, 0)])

            y_buf[par, 0:rows, :] = y

        @pl.when(half_ref[s] == 1)
        def _():
            tile(TM_FFN // 2)

        @pl.when(half_ref[s] == 0)
        def _():
            tile(TM_FFN)

        for grp in range(CHUNKS_PER_TILE // SCATTER_GROUP):
            @pl.when(grp * SCATTER_GROUP < nv)
            def _():
                for c in range(grp * SCATTER_GROUP, (grp + 1) * SCATTER_GROUP):
                    dst = pl.multiple_of(dst_cur_ref[0, 0, c] * RUN_ALIGN, RUN_ALIGN)
                    out_copy(par, c, dst).start()

        @pl.when(s == n_live - 1)
        def _():
            pltpu.make_async_copy(h2l_hbm.at[pl.ds(0, TM_FFN)], x_buf.at[1 - par],
                                  sem_x.at[1 - par]).wait()
            wait_out(par, nv)

            @pl.when(s >= 1)
            def _():
                wait_out(1 - par, nv_ref[jnp.maximum(s - 1, 0)])


def _ffn_call(tables, h2l, w_gu, b_gu, w_d, b_d):
    n_live, exp_id, n_valid, is_half, w_slot, w_next, src8, dst8 = tables
    n_tiles = src8.shape[0]

    def tbl_spec(index_fn):
        return pl.BlockSpec((1, 1, CHUNKS_PER_TILE), index_fn, memory_space=pltpu.SMEM)

    def per_expert(shape):
        return pl.BlockSpec((1,) + shape, lambda s, nl, ei, *_: (ei[s], 0, 0))

    grid_spec = pltpu.PrefetchScalarGridSpec(
        num_scalar_prefetch=6,
        grid=(n_tiles,),
        in_specs=[
            tbl_spec(lambda s, *_: (s, 0, 0)),
            tbl_spec(lambda s, *_: (jnp.minimum(s + 1, n_tiles - 1), 0, 0)),
            tbl_spec(lambda s, *_: (s, 0, 0)),
            pl.BlockSpec(memory_space=pl.ANY),
            pl.BlockSpec(memory_space=pl.ANY), per_expert((1, 2 * D_EXPERT)),
            pl.BlockSpec(memory_space=pl.ANY), per_expert((1, D_MODEL)),
        ],
        out_specs=pl.BlockSpec(memory_space=pl.ANY),
        scratch_shapes=[pltpu.VMEM((2, D_MODEL, 2 * D_EXPERT), F32),
                        pltpu.VMEM((2, D_EXPERT, D_MODEL), F32),
                        pltpu.VMEM((D_MODEL, 2 * D_EXPERT), BF16),
                        pltpu.VMEM((D_EXPERT, D_MODEL), BF16),
                        pltpu.VMEM((2, TM_FFN, D_MODEL), F32),
                        pltpu.VMEM((2, TM_FFN, D_MODEL), F32),
                        pltpu.SemaphoreType.DMA((2,)),
                        pltpu.SemaphoreType.DMA((2,)),
                        pltpu.SemaphoreType.DMA((2,))],
    )
    tables_and_chunks = (n_live, exp_id, n_valid, is_half, w_slot, w_next, src8, src8, dst8)
    return pl.pallas_call(
        _ffn_kernel,
        out_shape=jax.ShapeDtypeStruct(h2l.shape, F32),
        grid_spec=grid_spec,
        input_output_aliases={len(tables_and_chunks): 0},
        compiler_params=pltpu.CompilerParams(dimension_semantics=("arbitrary",),
                                             vmem_limit_bytes=VMEM_LIMIT,
                                             has_side_effects=True),
        name="ffn",
    )(*tables_and_chunks, h2l, w_gu, b_gu, w_d, b_d)


def _combine_kernel(x1_ref, ti_ref, tp_ref, mod_ref, g2_ref, b2_ref, yl_ref,
                    oc_ref, ol_ref, *, n_ctx, lat_len):
    i = pl.program_id(0)
    tb = TM_POST

    jl = lax.broadcasted_iota(jnp.int32, (tb, LOCAL_ROWS), 1)
    ti = ti_ref[...]
    tp = tp_ref[...]
    sel = jnp.zeros((tb, LOCAL_ROWS), F32)
    for k in range(TOP_K):
        sel = jnp.where(jl == ti[:, TOP_K + k:TOP_K + k + 1], tp[:, k:k + 1], sel)
    moe = _dot(sel.astype(BF16), yl_ref[...].astype(BF16))

    r = _mod_row(i, tb, n_ctx, lat_len)
    gate2 = mod_ref[pl.ds(r, 1), 5 * D_MODEL:6 * D_MODEL]
    res = DEEPNORM_ALPHA * x1_ref[...] + gate2 * moe
    mu = jnp.mean(res, axis=-1, keepdims=True)
    cen = res - mu
    var = jnp.mean(cen * cen, axis=-1, keepdims=True)
    out = cen * lax.rsqrt(var + LN_EPS) * g2_ref[...] + b2_ref[...]

    @pl.when(i * tb < n_ctx)
    def _():
        oc_ref[...] = out

    @pl.when(i * tb >= n_ctx)
    def _():
        ol_ref[...] = out


def _combine_call(x1, route, tp, mod, ln_g, ln_b, y_local, n_ctx, lat_len):
    t = x1.shape[0]
    tb = TM_POST
    kern = functools.partial(_combine_kernel, n_ctx=n_ctx, lat_len=lat_len)
    return pl.pallas_call(
        kern,
        out_shape=(jax.ShapeDtypeStruct((n_ctx, D_MODEL), F32),
                   jax.ShapeDtypeStruct((t - n_ctx, D_MODEL), F32)),
        grid=(t // tb,),
        in_specs=[
            pl.BlockSpec((tb, D_MODEL), lambda i: (i, 0)),
            pl.BlockSpec((tb, LANES), lambda i: (i, 0)),
            pl.BlockSpec((tb, LANES), lambda i: (i, 0)),
            pl.BlockSpec(mod.shape, lambda i: (0, 0)),
            pl.BlockSpec((1, D_MODEL), lambda i: (0, 0)),
            pl.BlockSpec((1, D_MODEL), lambda i: (0, 0)),
            pl.BlockSpec((LOCAL_ROWS, D_MODEL), lambda i: (i, 0)),
        ],
        out_specs=_ctx_lat_specs((tb, D_MODEL), n_ctx // tb),
        compiler_params=_cparams(("arbitrary",)),
        name="combine",
    )(x1, route, tp, mod, ln_g, ln_b, y_local)


def _routing_tables(cnt):
    i32 = jnp.int32
    half_rows = TM_FFN // 2
    n_blocks = cnt.shape[0]
    max_rows = n_blocks * (TM_POST * TOP_K + N_EXPERTS * (RUN_ALIGN - 1)) + N_EXPERTS * (half_rows - RUN_ALIGN)
    n_tiles = -(-max_rows // TM_FFN) + N_EXPERTS // 2 + 1

    cpad = (cnt + (RUN_ALIGN - 1)) // RUN_ALIGN * RUN_ALIGN
    run_loc = jnp.cumsum(cpad, axis=1) - cpad
    used = cpad.sum(axis=0)
    region = (used + (half_rows - 1)) // half_rows * half_rows
    incl = jnp.cumsum(region)
    offs = incl - region
    tiles_e = (region + (TM_FFN - 1)) // TM_FFN
    t_end = jnp.cumsum(tiles_e)
    n_live = t_end[-1]
    run_glob = offs[None, :] + jnp.cumsum(cpad, axis=0) - cpad

    experts = jnp.arange(N_EXPERTS, dtype=i32)

    def per_expert_row(e_idx, table):
        onehot = (e_idx[:, None] == experts).astype(F32)
        return jnp.dot(onehot, table.astype(F32), precision=lax.Precision.HIGHEST).astype(i32)

    tile = jnp.arange(n_tiles, dtype=i32)
    tcl = jnp.minimum(tile, n_live - 1)
    e_of = jnp.minimum(jnp.sum(t_end[None, :] <= tcl[:, None], axis=1), N_EXPERTS - 1).astype(i32)
    mine = e_of[:, None] == experts

    def pick(v):
        return jnp.sum(jnp.where(mine, v[None, :], 0), axis=1)

    k_in_e = tcl - pick(t_end - tiles_e)
    row0 = pick(offs) + k_in_e * TM_FFN
    tile_rows = jnp.minimum(TM_FFN, pick(region) - k_in_e * TM_FFN)
    is_half = (tile_rows == half_rows).astype(i32)
    n_run = jnp.clip((pick(offs + used) - row0) // RUN_ALIGN, 0, tile_rows // RUN_ALIGN)
    n_out = (n_run + (SCATTER_GROUP - 1)) // SCATTER_GROUP * SCATTER_GROUP
    n_out = jnp.where(tile < n_live, n_out, 0).astype(i32)

    c_in_tile = jnp.tile(jnp.arange(CHUNKS_PER_TILE, dtype=i32), n_tiles)
    first_row = jnp.where(tile < n_live, row0, incl[-1])
    g_row = jnp.repeat(first_row, CHUNKS_PER_TILE) + c_in_tile * RUN_ALIGN
    g_exp = jnp.minimum(jnp.sum(incl[None, :] <= g_row[:, None], axis=1), N_EXPERTS - 1).astype(i32)
    rs = per_expert_row(g_exp, run_glob.T)
    re = rs + per_expert_row(g_exp, cpad.T)
    ls = per_expert_row(g_exp, (jnp.arange(n_blocks, dtype=i32)[:, None] * LOCAL_ROWS + run_loc).T)
    inside = (g_row[:, None] >= rs) & (g_row[:, None] < re)
    valid = jnp.any(inside, axis=1)
    local_row = g_row + jnp.sum(jnp.where(inside, ls - rs, 0), axis=1)
    src8 = jnp.where(valid, local_row // RUN_ALIGN, LOCAL_CHUNKS - 1).astype(i32)

    filler = jnp.logical_not(valid) & (c_in_tile < jnp.repeat(n_out, CHUNKS_PER_TILE))
    u = jnp.cumsum(filler.astype(i32)) - 1
    spare = (u // 3) * LOCAL_CHUNKS + (LOCAL_CHUNKS - 4) + u % 3
    dst8 = jnp.where(valid, local_row // RUN_ALIGN, jnp.where(filler, spare, 0)).astype(i32)

    has_rows = used > 0
    ordinal = jnp.cumsum(has_rows.astype(i32)) - 1
    w_slot = (jnp.sum(jnp.where(e_of[:, None] == experts, ordinal[None, :], 0), axis=1) % 2).astype(i32)
    later = (experts[None, :] > e_of[:, None]) & has_rows[None, :]
    w_next = jnp.min(jnp.where(later, experts[None, :], N_EXPERTS), axis=1)
    w_next = jnp.where(w_next < N_EXPERTS, w_next, -1).astype(i32)
    shape = (n_tiles, 1, CHUNKS_PER_TILE)
    return (n_live.reshape(1).astype(i32), e_of, n_out, is_half, w_slot, w_next,
            src8.reshape(shape), dst8.reshape(shape))


def kernel(x_prompt, x_sample, c, state_ssd_fwd, state_ssd_bwd, c_ctx, w_ada, b_ada, w_in,
           ssd_conv_w, ssd_conv_b, ssd_a_log, ssd_dt_bias, ssd_d, ssd_norm_w, ssd_w_out,
           sc_conv_w, sc_w_out, w_o, ln1_g, ln1_b, w_router, b_router, w_gate_up, b_gate_up,
           w_down, b_down, ln2_g, ln2_b):
    n_ctx_req, ctx_len, _ = x_prompt.shape
    n_lat_req, lat_len, _ = x_sample.shape
    n_ctx = n_ctx_req * ctx_len
    n_lat = n_lat_req * lat_len
    t = n_ctx + n_lat
    assert w_ada.shape[0] == 1, "single trunk layer"
    assert ctx_len % SSD_CHUNK == 0 and lat_len % TS_SSD == 0 and SSD_CHUNK % GRID_W == 0
    assert n_ctx % TM_PROJ == 0 and lat_len % TM_PROJ == 0 and TM_POST % ctx_len == 0
    assert CONV_SUB % ctx_len == 0 and CONV_SUB % GRID_W == 0 and n_ctx % TB_CONV == 0
    assert ctx_len & (ctx_len - 1) == 0 and GRID_W & (GRID_W - 1) == 0

    x_ctx = x_prompt.reshape(n_ctx, D_MODEL)
    x_lat = x_sample.reshape(n_lat, D_MODEL)

    cvec = jnp.concatenate([c_ctx[None, :], c, jnp.zeros((7 - n_lat_req, D_MODEL), F32)], 0)
    mod = _mod_call(cvec, w_ada[0], b_ada[0])

    w = w_in[0]
    o_dt = D_INNER + XBC_DIM
    w_main = jnp.concatenate([w[:, :o_dt], w[:, o_dt + SSD_HEADS:]], axis=1).astype(BF16)
    w_dt = jnp.pad(w[:, o_dt:o_dt + SSD_HEADS], ((0, 0), (0, LANES - SSD_HEADS))).astype(BF16)
    proj, dt_raw = _inproj_call(x_ctx, x_lat, mod, w_main, w_dt, lat_len)

    xbc = _conv_call(proj, ssd_conv_w[0], ssd_conv_b[0], n_ctx, ctx_len)

    pad_h = ((0, 0), (0, LANES - SSD_HEADS))
    a_log = jnp.pad(ssd_a_log[0], pad_h).reshape(2, 1, LANES)
    dt_bias = jnp.pad(ssd_dt_bias[0], pad_h).reshape(2, 1, LANES)
    dvec = jnp.repeat(ssd_d[0], SSD_HEADDIM).reshape(1, D_INNER)
    h0_lat = jnp.stack([state_ssd_fwd[:, 0].reshape(n_lat_req, D_INNER, SSD_STATE),
                        state_ssd_bwd[:, 0].reshape(n_lat_req, D_INNER, SSD_STATE)], axis=1)
    y_ctx, h_ctx = _ssd_call(xbc, dt_raw, None, a_log, dt_bias, dvec, 0, n_ctx_req, ctx_len, True)
    (y_lat,) = _ssd_call(xbc, dt_raw, h0_lat, a_log, dt_bias, dvec, n_ctx, n_lat_req, lat_len, False)

    w_r = jnp.pad(w_router[0], ((0, 0), (0, LANES - N_EXPERTS)))
    b_r = jnp.pad(b_router[0], (0, LANES - N_EXPERTS)).reshape(1, LANES)
    x1, h2_local, route, top_p, cnt = _post_call(
        x_ctx, x_lat, mod, proj, y_ctx, y_lat, ssd_norm_w[0].reshape(1, D_INNER),
        ssd_w_out[0].astype(BF16), sc_conv_w[0], sc_w_out[0].astype(BF16), w_o[0].astype(BF16),
        ln1_g[0].reshape(1, D_MODEL), ln1_b[0].reshape(1, D_MODEL), w_r, b_r, ctx_len, lat_len)

    tables = _routing_tables(cnt[:, 0, :N_EXPERTS])
    y_local = _ffn_call(tables, h2_local,
                        w_gate_up[0], b_gate_up[0].reshape(N_EXPERTS, 1, 2 * D_EXPERT),
                        w_down[0], b_down[0].reshape(N_EXPERTS, 1, D_MODEL))
    out_ctx, out_lat = _combine_call(x1, route, top_p, mod, ln2_g[0].reshape(1, D_MODEL),
                                     ln2_b[0].reshape(1, D_MODEL), y_local, n_ctx, lat_len)

    state_shape = (n_ctx_req, 1, SSD_HEADS, SSD_HEADDIM, SSD_STATE)
    return (out_ctx.reshape(n_ctx_req, ctx_len, D_MODEL), out_lat.reshape(n_lat_req, lat_len, D_MODEL),
            h_ctx[0].reshape(state_shape), h_ctx[1].reshape(state_shape))
```
